```python
import jax
import jax.numpy as jnp
from jax import lax
import numpy as np

D_MODEL = 1024
BATCH = 8
SEQ = 2048
DEPTH = 2

HEAD_DIM = 64
A_GROUPS = 4
A_GROUP_CH = 128
A_WIDTH = A_GROUPS * A_GROUP_CH
CHUNK = 128
DILATED_PATTERNS = ((128, 1), (512, 4), (2048, 16))
B_HEADS_PER_GROUP = 4
B_HEADS = len(DILATED_PATTERNS) * B_HEADS_PER_GROUP
B_WIDTH = B_HEADS * HEAD_DIM
B_OUT = B_HEADS_PER_GROUP * HEAD_DIM
BLOCK = 128
ROPE_THETA = 500000.0
ROPE_DIMS = HEAD_DIM // 4
C_HEADS = 8
C_WIDTH = C_HEADS * HEAD_DIM
W_LORA = 64
A_LORA = 64
V_LORA = 32
G_LORA = 128
RWKV_GN_EPS = HEAD_DIM * 1e-5
IN_SPLIT_WIDTHS = (A_WIDTH, A_WIDTH, B_WIDTH, B_WIDTH, B_WIDTH, C_WIDTH, C_WIDTH, C_WIDTH)
IN_COLS = 2 * A_WIDTH + 3 * B_WIDTH + 3 * C_WIDTH
N_BRANCHES = 3
N_GROUPS = 4
EXPERTS_PER_GROUP = 4
N_EXPERTS = N_GROUPS * EXPERTS_PER_GROUP
EXPERT_FF = 512
TOP_K_IN_GROUP = 2
RMS_EPS = 1e-6
LN_EPS = 1e-5

kernel_name = 'hybrid_gated_mixers_hier_moe'


def rmsnorm(x, g):
    xf = x.astype(jnp.float32)
    y = xf * lax.rsqrt(jnp.mean(xf * xf, axis=-1, keepdims=True) + RMS_EPS)
    return (y * g.astype(jnp.float32)).astype(x.dtype)


def shift_prev(t):
    return jnp.pad(t, ((0, 0), (1, 0), (0, 0)))[:, :-1]


def split_heads(t, n_heads):
    return t.reshape(t.shape[:-1] + (n_heads, HEAD_DIM))


def partial_rotary(t, positions):
    half = ROPE_DIMS // 2
    inv_freq = ROPE_THETA ** (-jnp.arange(half, dtype=jnp.float32) / half)
    ang = positions.astype(jnp.float32)[..., None] * inv_freq
    cos = jnp.cos(ang)[:, :, None, :]
    sin = jnp.sin(ang)[:, :, None, :]
    tr = t[..., :ROPE_DIMS].astype(jnp.float32)
    t1, t2 = tr[..., :half], tr[..., half:]
    rot = jnp.concatenate([t1 * cos - t2 * sin, t2 * cos + t1 * sin], axis=-1).astype(t.dtype)
    return jnp.concatenate([rot, t[..., ROPE_DIMS:]], axis=-1)


def chunked_spatial_gating(u, v, ln_g, ln_b, w_s, b_s):
    Bn, S, _ = u.shape
    vf = v.astype(jnp.float32)
    mu = jnp.mean(vf, axis=-1, keepdims=True)
    var = jnp.mean(jnp.square(vf - mu), axis=-1, keepdims=True)
    vn = ((vf - mu) * lax.rsqrt(var + LN_EPS) * ln_g + ln_b).astype(v.dtype)
    vn = vn.reshape(Bn, S // CHUNK, CHUNK, A_GROUPS, A_GROUP_CH)
    causal = jnp.tril(jnp.ones((CHUNK, CHUNK), dtype=bool))
    w_causal = jnp.where(causal[None], w_s, jnp.zeros_like(w_s))
    mixed = jnp.einsum('gts,bnsgc->bntgc', w_causal, vn) + b_s.T[:, :, None]
    return u * mixed.reshape(Bn, S, A_WIDTH)


def dilated_window_attention(q, k, v, window, dilation):
    Bn, S, H, Dh = q.shape
    L = S // dilation
    nb = -(-L // BLOCK)
    Lp = nb * BLOCK
    steps = window // dilation

    def to_blocks(t):
        t = t.reshape(Bn, L, dilation, H, Dh).transpose(0, 2, 1, 3, 4)
        t = jnp.pad(t, ((0, 0), (0, 0), (0, Lp - L), (0, 0), (0, 0)))
        return t.reshape(Bn, dilation, nb, BLOCK, H, Dh)

    def with_prev(t):
        prev = jnp.pad(t, ((0, 0), (0, 0), (1, 0), (0, 0), (0, 0), (0, 0)))[:, :, :-1]
        return jnp.concatenate([prev, t], axis=3)

    qb = to_blocks(q)
    kw = with_prev(to_blocks(k))
    vw = with_prev(to_blocks(v))
    scores = jnp.einsum('brnqhd,brnkhd->brnhqk', qb, kw,
                        preferred_element_type=jnp.float32) * (Dh ** -0.5)
    qi = jnp.arange(BLOCK)[:, None]
    kj = jnp.arange(2 * BLOCK)[None, :]
    dist = BLOCK + qi - kj
    in_band = (dist >= 0) & (dist <= steps)
    has_key = (jnp.arange(nb) > 0)[:, None, None] | (kj >= BLOCK)[None]
    valid = in_band[None] & has_key
    scores = jnp.where(valid[None, None, :, None], scores, -jnp.inf)
    lse = jax.nn.logsumexp(scores, axis=-1)
    p = jnp.exp(scores - lse[..., None]).astype(v.dtype)
    out = jnp.einsum('brnhqk,brnkhd->brnqhd', p, vw)

    def from_blocks(t):
        tail = t.shape[4:]
        t = t.reshape((Bn, dilation, Lp) + tail)[:, :, :L]
        return jnp.swapaxes(t, 1, 2).reshape((Bn, S) + tail)

    return from_blocks(out), from_blocks(jnp.swapaxes(lse, -1, -2))


def dilated_attention_branch(q, k, v, positions):
    Bn, S = q.shape[:2]
    q = partial_rotary(q, positions)
    k = partial_rotary(k, positions)
    outs, lses = [], []
    for g, (window, dilation) in enumerate(DILATED_PATTERNS):
        hs = slice(g * B_HEADS_PER_GROUP, (g + 1) * B_HEADS_PER_GROUP)
        o, l = dilated_window_attention(q[:, :, hs], k[:, :, hs], v[:, :, hs], window, dilation)
        outs.append(o)
        lses.append(l)
    weights = jax.nn.softmax(jnp.stack(lses, axis=0), axis=0)
    out = jnp.sum(weights[..., None] * jnp.stack(outs, axis=0).astype(jnp.float32), axis=0)
    return out.reshape(Bn, S, B_OUT).astype(q.dtype)


def rwkv7_time_mix(h, r, k, v, v_first, mu_rkv, mu_wag, w0, w1, w2, a0, a1, a2, g1, g2,
                   k_k, k_a, r_k, ln_w, ln_b, vres):
    Bn, S, _ = h.shape
    f32 = jnp.float32
    r = r + (shift_prev(r) - r) * mu_rkv[0]
    k = k + (shift_prev(k) - k) * mu_rkv[1]
    v = v + (shift_prev(v) - v) * mu_rkv[2]
    dh = shift_prev(h) - h
    xw = h + dh * mu_wag[0]
    xa = h + dh * mu_wag[1]
    xg = h + dh * mu_wag[2]
    w_log = -jax.nn.softplus(-(w0 + jnp.tanh(xw @ w1) @ w2).astype(f32)) - 0.5
    decay = jnp.exp(-jnp.exp(w_log))
    a = jax.nn.sigmoid(a0 + (xa @ a1) @ a2)
    g = jax.nn.sigmoid(xg @ g1) @ g2
    if vres is not None:
        mu_v, v0, v1, v2 = vres
        xv = h + dh * mu_v
        v = v + (v_first - v) * jax.nn.sigmoid(v0 + (xv @ v1) @ v2)
    kk = split_heads(k * k_k, C_HEADS).astype(f32)
    kk = kk / jnp.maximum(jnp.sqrt(jnp.sum(kk * kk, axis=-1, keepdims=True)), 1e-12)
    k = k * (1 + (a - 1) * k_a)
    rh = split_heads(r, C_HEADS).astype(f32)
    kh = split_heads(k, C_HEADS).astype(f32)
    vh = split_heads(v, C_HEADS).astype(f32)
    ah = split_heads(a, C_HEADS).astype(f32)
    wh = split_heads(decay, C_HEADS)

    def step(state, inp):
        r_t, w_t, k_t, v_t, kk_t, a_t = inp
        sa = jnp.einsum('bhvk,bhk->bhv', state, -kk_t)
        state = (state * w_t[:, :, None, :] + sa[..., None] * (kk_t * a_t)[:, :, None, :]
                 + v_t[..., None] * k_t[:, :, None, :])
        return state, jnp.einsum('bhvk,bhk->bhv', state, r_t)

    xs = tuple(jnp.moveaxis(t, 1, 0) for t in (rh, wh, kh, vh, kk, ah))
    state0 = jnp.zeros((Bn, C_HEADS, HEAD_DIM, HEAD_DIM), f32)
    _, y = lax.scan(step, state0, xs)
    y = jnp.moveaxis(y, 0, 1)
    mu = jnp.mean(y, axis=-1, keepdims=True)
    var = jnp.mean(jnp.square(y - mu), axis=-1, keepdims=True)
    y = (y - mu) * lax.rsqrt(var + RWKV_GN_EPS)
    y = y * split_heads(ln_w, C_HEADS).astype(f32) + split_heads(ln_b, C_HEADS).astype(f32)
    bonus = jnp.sum(rh * kh * r_k.astype(f32), axis=-1, keepdims=True) * vh
    y = (y + bonus).reshape(Bn, S, C_WIDTH).astype(h.dtype)
    return y * g, v


def hierarchical_moe(x, wg, bg, we, be, w_gate, w_up, w_down):
    Bn, S, D = x.shape
    xt = x.reshape(Bn * S, D)
    group_prob = jax.nn.softmax((xt @ wg + bg).astype(jnp.float32), axis=-1)
    gp, gi = lax.top_k(group_prob, 1)
    expert_logits = (xt @ we + be).astype(jnp.float32).reshape(-1, N_GROUPS, EXPERTS_PER_GROUP)
    sel_logits = jnp.take_along_axis(expert_logits, gi[:, :, None], axis=1)[:, 0]
    ep, ei = lax.top_k(jax.nn.softmax(sel_logits, axis=-1), TOP_K_IN_GROUP)
    ep = ep / jnp.sum(ep, axis=-1, keepdims=True)
    in_group = jnp.sum(jax.nn.one_hot(ei, EXPERTS_PER_GROUP) * ep[..., None], axis=1)
    combine = jax.nn.one_hot(gi[:, 0], N_GROUPS)[:, :, None] * (gp * in_group)[:, None, :]
    combine = combine.reshape(-1, N_EXPERTS).astype(x.dtype)
    out = jnp.zeros_like(xt)
    for g in range(N_GROUPS):
        es = slice(g * EXPERTS_PER_GROUP, (g + 1) * EXPERTS_PER_GROUP)
        hid = (jax.nn.silu(jnp.einsum('nd,edf->nef', xt, w_gate[es]))
               * jnp.einsum('nd,edf->nef', xt, w_up[es]))
        out = out + jnp.einsum('nef,efd->nd', hid * combine[:, es, None], w_down[es])
    return out.reshape(Bn, S, D)


def setup_inputs(seed: int = 0) -> dict:
    key = jax.random.key(seed)
    ks = iter(jax.random.split(key, 48))
    f32 = jnp.float32
    D = D_MODEL
    L = DEPTH

    def normal(shape, scale):
        return jax.random.normal(next(ks), shape, f32) * scale

    def uniform(shape, lo, hi):
        return jax.random.uniform(next(ks), shape, f32, lo, hi)

    x = normal((BATCH, SEQ, D), 1.0)
    start = jax.random.randint(next(ks), (BATCH, 1), 0, 4096, dtype=jnp.int32)
    positions = start + jnp.arange(SEQ, dtype=jnp.int32)[None, :]
    return {
        'x': x,
        'positions': positions,
        'norm_mix_g': 1.0 + normal((L, D), 0.01),
        'w_in': normal((L, D, IN_COLS), D ** -0.5),
        'gmlp_ln_g': 1.0 + normal((L, A_WIDTH), 0.01),
        'gmlp_ln_b': normal((L, A_WIDTH), 0.01),
        'gmlp_ws': normal((L, A_GROUPS, CHUNK, CHUNK), CHUNK ** -0.5),
        'gmlp_bs': 1.0 + normal((L, A_GROUPS, CHUNK), 0.01),
        'rwkv_mu_rkv': uniform((L, 3, C_WIDTH), 0.0, 1.0),
        'rwkv_mu_wag': uniform((L, 3, D), 0.0, 1.0),
        'rwkv_w0': uniform((L, C_WIDTH), -4.0, 1.0),
        'rwkv_w1': normal((L, D, W_LORA), D ** -0.5),
        'rwkv_w2': normal((L, W_LORA, C_WIDTH), 0.1 * W_LORA ** -0.5),
        'rwkv_a0': normal((L, C_WIDTH), 0.5),
        'rwkv_a1': normal((L, D, A_LORA), D ** -0.5),
        'rwkv_a2': normal((L, A_LORA, C_WIDTH), 0.5 * A_LORA ** -0.5),
        'rwkv_g1': normal((L, D, G_LORA), D ** -0.5),
        'rwkv_g2': normal((L, G_LORA, C_WIDTH), G_LORA ** -0.5),
        'rwkv_k_k': 0.85 + normal((L, C_WIDTH), 0.05),
        'rwkv_k_a': 1.0 + normal((L, C_WIDTH), 0.05),
        'rwkv_r_k': normal((L, C_HEADS, HEAD_DIM), 0.1),
        'rwkv_ln_w': 1.0 + normal((L, C_WIDTH), 0.01),
        'rwkv_ln_b': normal((L, C_WIDTH), 0.01),
        'rwkv_mu_v': uniform((L - 1, D), 0.0, 1.0),
        'rwkv_v0': normal((L - 1, C_WIDTH), 0.5),
        'rwkv_v1': normal((L - 1, D, V_LORA), D ** -0.5),
        'rwkv_v2': normal((L - 1, V_LORA, C_WIDTH), 0.5 * V_LORA ** -0.5),
        'w_branch_a': normal((L, A_WIDTH, D), A_WIDTH ** -0.5),
        'w_branch_b': normal((L, B_OUT, D), B_OUT ** -0.5),
        'w_branch_c': normal((L, C_WIDTH, D), C_WIDTH ** -0.5),
        'w_gate': normal((L, D, N_BRANCHES * D), D ** -0.5),
        'b_gate': normal((L, N_BRANCHES * D), 0.01),
        'w_out': normal((L, D, D), D ** -0.5),
        'norm_ffn_g': 1.0 + normal((L, D), 0.01),
        'router_group_w': normal((L, D, N_GROUPS), D ** -0.5),
        'router_group_b': normal((L, N_GROUPS), 0.01),
        'router_expert_w': normal((L, D, N_EXPERTS), D ** -0.5),
        'router_expert_b': normal((L, N_EXPERTS), 0.01),
        'expert_w_gate': normal((L, N_EXPERTS, D, EXPERT_FF), D ** -0.5),
        'expert_w_up': normal((L, N_EXPERTS, D, EXPERT_FF), D ** -0.5),
        'expert_w_down': normal((L, N_EXPERTS, EXPERT_FF, D), EXPERT_FF ** -0.5),
        'final_norm_g': 1.0 + normal((D,), 0.01),
    }


def reference(x, positions, norm_mix_g, w_in, gmlp_ln_g, gmlp_ln_b, gmlp_ws, gmlp_bs,
              rwkv_mu_rkv, rwkv_mu_wag, rwkv_w0, rwkv_w1, rwkv_w2, rwkv_a0, rwkv_a1, rwkv_a2,
              rwkv_g1, rwkv_g2, rwkv_k_k, rwkv_k_a, rwkv_r_k, rwkv_ln_w, rwkv_ln_b,
              rwkv_mu_v, rwkv_v0, rwkv_v1, rwkv_v2, w_branch_a, w_branch_b, w_branch_c,
              w_gate, b_gate, w_out, norm_ffn_g, router_group_w, router_group_b,
              router_expert_w, router_expert_b, expert_w_gate, expert_w_up, expert_w_down,
              final_norm_g):
    Bn, S, D = x.shape
    split_at = np.cumsum(IN_SPLIT_WIDTHS)[:-1].tolist()
    v_first = None
    for l in range(DEPTH):
        h = rmsnorm(x, norm_mix_g[l])
        proj = h @ w_in[l]
        u_a, v_a, q_b, k_b, v_b, r_c, k_c, v_c = jnp.split(proj, split_at, axis=-1)
        y_a = chunked_spatial_gating(jax.nn.gelu(u_a), jax.nn.gelu(v_a), gmlp_ln_g[l],
                                     gmlp_ln_b[l], gmlp_ws[l], gmlp_bs[l])
        y_b = dilated_attention_branch(split_heads(q_b, B_HEADS), split_heads(k_b, B_HEADS),
                                       split_heads(v_b, B_HEADS), positions)
        vres = None if l == 0 else (rwkv_mu_v[l - 1], rwkv_v0[l - 1], rwkv_v1[l - 1], rwkv_v2[l - 1])
        y_c, v_c = rwkv7_time_mix(h, r_c, k_c, v_c, v_first, rwkv_mu_rkv[l], rwkv_mu_wag[l],
                                  rwkv_w0[l], rwkv_w1[l], rwkv_w2[l], rwkv_a0[l], rwkv_a1[l],
                                  rwkv_a2[l], rwkv_g1[l], rwkv_g2[l], rwkv_k_k[l], rwkv_k_a[l],
                                  rwkv_r_k[l], rwkv_ln_w[l], rwkv_ln_b[l], vres)
        if l == 0:
            v_first = v_c
        gates = jax.nn.sigmoid(h @ w_gate[l] + b_gate[l]).reshape(Bn, S, N_BRANCHES, D)
        merged = (gates[:, :, 0] * (y_a @ w_branch_a[l])
                  + gates[:, :, 1] * (y_b @ w_branch_b[l])
                  + gates[:, :, 2] * (y_c @ w_branch_c[l]))
        x = x + merged @ w_out[l]
        x = x + hierarchical_moe(rmsnorm(x, norm_ffn_g[l]), router_group_w[l], router_group_b[l],
                                 router_expert_w[l], router_expert_b[l], expert_w_gate[l],
                                 expert_w_up[l], expert_w_down[l])
    return rmsnorm(x, final_norm_g)
```

```python
import functools

import numpy as np
import jax
import jax.numpy as jnp
from jax import lax
from jax.experimental import pallas as pl
from jax.experimental.pallas import tpu as pltpu

F32 = jnp.float32
BF16 = jnp.bfloat16
HIGHEST = lax.Precision.HIGHEST

D_MODEL = 1024
HEAD_DIM = 64
A_GROUPS = 4
A_WIDTH = 512
CHUNK = 128
DILATED_PATTERNS = ((128, 1), (512, 4), (2048, 16))
B_HEADS_PER_GROUP = 4
B_WIDTH = 768
B_OUT = 256
BLOCK = 128
ROPE_THETA = 500000.0
ROPE_DIMS = 16
C_HEADS = 8
C_WIDTH = 512
W_LORA, A_LORA, V_LORA, G_LORA = 64, 64, 32, 128
RWKV_GN_EPS = HEAD_DIM * 1e-5
IN_COLS = 2 * A_WIDTH + 3 * B_WIDTH + 3 * C_WIDTH
LORA_HALF = 384
LORA_COLS = 2 * LORA_HALF
N_GROUPS = 4
EXPERTS_PER_GROUP = 4
N_EXPERTS = 16
EXPERT_FF = 512
RMS_EPS = 1e-6
LN_EPS = 1e-5

LANES = 128
ROW_TILE = 256
RWKV_CHUNK = 64
MOE_ROW_TILE = 1024
VMEM_LIMIT = 56 * 1024 * 1024
NEG_BIG = -1e30


def _params(*sem):
    return pltpu.CompilerParams(dimension_semantics=sem, vmem_limit_bytes=VMEM_LIMIT)


def _full(shape):
    nd = len(shape)
    return pl.BlockSpec(shape, lambda *_: (0,) * nd)


def _gelu_tanh(x):
    return 0.5 * x * (1.0 + jnp.tanh(0.7978845608028654 * (x + 0.044715 * (x * x * x))))


def _sigmoid(x):
    return 1.0 / (1.0 + jnp.exp(-x))


def _rmsnorm(x, g):
    return x * lax.rsqrt(jnp.mean(x * x, axis=-1, keepdims=True) + RMS_EPS) * g


def _rope_table_kernel(pos_ref, consts_ref, c_ref, sa_ref, sb_ref):
    ang = pos_ref[...].astype(F32) * consts_ref[0:1, :]
    c_ref[...] = jnp.cos(ang)
    s = jnp.sin(ang)
    sa_ref[...] = -s * consts_ref[1:2, :]
    sb_ref[...] = s * consts_ref[2:3, :]


def _rope_tables(positions):
    n = positions.size
    half = ROPE_DIMS // 2
    inv_freq = ROPE_THETA ** (-jnp.arange(half, dtype=F32) / half)
    lane = np.arange(LANES) % HEAD_DIM
    in_rope = lane < ROPE_DIMS
    invf_row = jnp.where(in_rope, inv_freq[lane % half], 0.0)
    consts = jnp.zeros((8, LANES), F32)
    consts = consts.at[0].set(invf_row)
    consts = consts.at[1].set(jnp.asarray(lane < half, F32))
    consts = consts.at[2].set(jnp.asarray((lane >= half) & in_rope, F32))
    tm = 1024
    out = jax.ShapeDtypeStruct((n, LANES), F32)
    return pl.pallas_call(
        _rope_table_kernel,
        grid=(n // tm,),
        in_specs=[pl.BlockSpec((tm, 1), lambda i: (i, 0)), _full((8, LANES))],
        out_specs=[pl.BlockSpec((tm, LANES), lambda i: (i, 0))] * 3,
        out_shape=[out, out, out],
        compiler_params=_params("parallel"),
        name="rope_tables",
    )(positions.reshape(n, 1), consts)


def _inproj_kernel(x_ref, g_ref, w_ref, lng_ref, lnb_ref, ws_ref, bst_ref, c_ref, sa_ref, sb_ref,
                   ya_ref, qkv_ref, rkv_ref, lora_ref):
    tm = x_ref.shape[0]
    hb = _rmsnorm(x_ref[...], g_ref[...]).astype(BF16)

    def proj(lo, width):
        return jnp.dot(hb, w_ref[:, lo:lo + width], preferred_element_type=F32)

    uv = proj(0, 2 * A_WIDTH)
    u = _gelu_tanh(uv[:, :A_WIDTH])
    v = _gelu_tanh(uv[:, A_WIDTH:])
    mu = jnp.mean(v, axis=-1, keepdims=True)
    d = v - mu
    var = jnp.mean(d * d, axis=-1, keepdims=True)
    vn = (d * lax.rsqrt(var + LN_EPS) * lng_ref[...] + lnb_ref[...]).astype(BF16)
    row = lax.broadcasted_iota(jnp.int32, (CHUNK, CHUNK), 0)
    col = lax.broadcasted_iota(jnp.int32, (CHUNK, CHUNK), 1)
    for g in range(A_GROUPS):
        wg = jnp.where(row >= col, ws_ref[g], 0.0).astype(BF16)
        bias = bst_ref[:, g:g + 1]
        cs = slice(g * LANES, (g + 1) * LANES)
        for c in range(tm // CHUNK):
            rs = slice(c * CHUNK, (c + 1) * CHUNK)
            mixed = jnp.dot(wg, vn[rs, cs], preferred_element_type=F32) + bias
            ya_ref[rs, cs] = (u[rs, cs] * mixed).astype(BF16)

    cos, sa, sb = c_ref[...], sa_ref[...], sb_ref[...]
    q_off = 2 * A_WIDTH
    for part, scale in ((0, HEAD_DIM ** -0.5), (1, 1.0)):
        acc = proj(q_off + part * B_WIDTH, B_WIDTH)
        for c in range(B_WIDTH // LANES):
            t = acc[:, c * LANES:(c + 1) * LANES]
            rot = t * cos + pltpu.roll(t, LANES - 8, 1) * sa + pltpu.roll(t, 8, 1) * sb
            lo = part * B_WIDTH + c * LANES
            qkv_ref[:, lo:lo + LANES] = (rot * scale).astype(BF16)
    qkv_ref[:, 2 * B_WIDTH:] = proj(q_off + 2 * B_WIDTH, B_WIDTH).astype(BF16)

    rkv_ref[...] = proj(q_off + 3 * B_WIDTH, 3 * C_WIDTH)
    lora_ref[...] = proj(IN_COLS, LORA_COLS)


def _inproj(x2, g, w_all, ln_g, ln_b, ws, bs_t, rope):
    n = x2.shape[0]
    tm = ROW_TILE
    wcols = w_all.shape[1]
    row = lambda width: pl.BlockSpec((tm, width), lambda i: (i, 0))
    return pl.pallas_call(
        _inproj_kernel,
        grid=(n // tm,),
        in_specs=[row(D_MODEL), _full((1, D_MODEL)), _full((D_MODEL, wcols)),
                  _full((1, A_WIDTH)), _full((1, A_WIDTH)), _full((A_GROUPS, CHUNK, CHUNK)),
                  _full((CHUNK, A_GROUPS)), row(LANES), row(LANES), row(LANES)],
        out_specs=[row(A_WIDTH), row(3 * B_WIDTH), row(3 * C_WIDTH), row(LORA_COLS)],
        out_shape=[jax.ShapeDtypeStruct((n, A_WIDTH), BF16),
                   jax.ShapeDtypeStruct((n, 3 * B_WIDTH), BF16),
                   jax.ShapeDtypeStruct((n, 3 * C_WIDTH), F32),
                   jax.ShapeDtypeStruct((n, LORA_COLS), F32)],
        compiler_params=_params("parallel"),
        name="inproj",
    )(x2, g, w_all, ln_g, ln_b, ws, bs_t, *rope)


def _attn_kernel(q_ref, k_ref, v_ref, o_ref, l_ref):
    seq = q_ref.shape[1]
    nb = seq // BLOCK
    hd = HEAD_DIM

    def block(qb, kw, vw, valid):
        outs, lses = [], []
        for j in range(B_HEADS_PER_GROUP):
            hs = slice(j * hd, (j + 1) * hd)
            s = lax.dot_general(qb[:, hs], kw[:, hs], (((1,), (1,)), ((), ())),
                                preferred_element_type=F32)
            s = jnp.where(valid, s, NEG_BIG)
            m = jnp.max(s, axis=-1, keepdims=True)
            p = jnp.exp(s - m)
            l = jnp.sum(p, axis=-1, keepdims=True)
            o = jnp.dot(p.astype(BF16), vw[:, hs], preferred_element_type=F32) / l
            outs.append(o)
            lses.append(jnp.broadcast_to(m + jnp.log(l), (BLOCK, hd)))
        return jnp.concatenate(outs, axis=-1), jnp.concatenate(lses, axis=-1)

    qi = lax.broadcasted_iota(jnp.int32, (BLOCK, BLOCK), 0)
    kj = lax.broadcasted_iota(jnp.int32, (BLOCK, BLOCK), 1)
    o, l = block(q_ref[0, 0:BLOCK, :], k_ref[0, 0:BLOCK, :], v_ref[0, 0:BLOCK, :], kj <= qi)
    o_ref[0, 0:BLOCK, :] = o
    l_ref[0, 0:BLOCK, :] = l

    if nb > 1:
        qi2 = lax.broadcasted_iota(jnp.int32, (BLOCK, 2 * BLOCK), 0)
        kj2 = lax.broadcasted_iota(jnp.int32, (BLOCK, 2 * BLOCK), 1)
        valid2 = (kj2 >= qi2) & (kj2 <= qi2 + BLOCK)

        def body(n, carry):
            start = pl.multiple_of(n * BLOCK, BLOCK)
            prev = pl.multiple_of(start - BLOCK, BLOCK)
            o, l = block(q_ref[0, pl.ds(start, BLOCK), :], k_ref[0, pl.ds(prev, 2 * BLOCK), :],
                         v_ref[0, pl.ds(prev, 2 * BLOCK), :], valid2)
            o_ref[0, pl.ds(start, BLOCK), :] = o
            l_ref[0, pl.ds(start, BLOCK), :] = l
            return carry

        lax.fori_loop(1, nb, body, 0)


def _dilated_attention(qkv, batch, seq, group, dilation):
    sub = seq // dilation
    ncb = 3 * B_WIDTH // B_OUT
    view = qkv.reshape(batch, sub, dilation * 3 * B_WIDTH)
    spec = lambda part: pl.BlockSpec(
        (1, sub, B_OUT), lambda b, r: (b, 0, r * ncb + part * (B_WIDTH // B_OUT) + group))
    ospec = pl.BlockSpec((1, sub, B_OUT), lambda b, r: (b, 0, r))
    oshape = jax.ShapeDtypeStruct((batch, sub, dilation * B_OUT), F32)
    o, l = pl.pallas_call(
        _attn_kernel,
        grid=(batch, dilation),
        in_specs=[spec(0), spec(1), spec(2)],
        out_specs=[ospec, ospec],
        out_shape=[oshape, oshape],
        compiler_params=_params("parallel", "parallel"),
        name=f"dilated_attn_d{dilation}",
    )(view, view, view)
    return o.reshape(batch * seq, B_OUT), l.reshape(batch * seq, B_OUT)


def _shift_rows(x, prev_row):
    rolled = pltpu.roll(x, 1, 0)
    first = lax.broadcasted_iota(jnp.int32, x.shape, 0) == 0
    return jnp.where(first, prev_row, rolled)


def _dot_hi(a, b):
    return jnp.dot(a, b, precision=HIGHEST, preferred_element_type=F32)


def _rwkv_kernel(*refs, has_vres):
    if has_vres:
        (rkv_ref, lora_ref, vf_ref, mu_ref, vec_ref, w2_ref, a2_ref, g2_ref, v2_ref, bd_ref,
         yc_ref, vout_ref, state_ref, prkv_ref, plora_ref, y_ref) = refs
    else:
        (rkv_ref, lora_ref, mu_ref, vec_ref, w2_ref, a2_ref, g2_ref, bd_ref,
         yc_ref, vout_ref, state_ref, prkv_ref, plora_ref, y_ref) = refs
    ch = rkv_ref.shape[0]
    hd = HEAD_DIM

    @pl.when(pl.program_id(1) == 0)
    def _():
        state_ref[...] = jnp.zeros_like(state_ref)
        prkv_ref[...] = jnp.zeros_like(prkv_ref)
        plora_ref[...] = jnp.zeros_like(plora_ref)

    rkv = rkv_ref[...]
    rkv_prev = _shift_rows(rkv, prkv_ref[0:1, :])
    prkv_ref[0:1, :] = rkv[ch - 1:ch, :]
    lora2 = lora_ref[:, LORA_HALF:]
    lora = lora_ref[:, :LORA_HALF] + _shift_rows(lora2, plora_ref[0:1, :])
    plora_ref[0:1, :] = lora2[ch - 1:ch, :]

    w0, a0, v0 = vec_ref[0:1, :], vec_ref[1:2, :], vec_ref[2:3, :]
    k_k, k_a, r_k = vec_ref[3:4, :], vec_ref[4:5, :], vec_ref[5:6, :]
    ln_w, ln_b = vec_ref[6:7, :], vec_ref[7:8, :]

    def mix(i):
        cur = rkv[:, i * C_WIDTH:(i + 1) * C_WIDTH]
        prev = rkv_prev[:, i * C_WIDTH:(i + 1) * C_WIDTH]
        return cur + (prev - cur) * mu_ref[i:i + 1, :]

    r, k, v = mix(0), mix(1), mix(2)
    wa = lora[:, 0:LANES]
    z = w0 + _dot_hi(jnp.tanh(wa), w2_ref[...])
    w_log = -(jnp.maximum(-z, 0.0) + jnp.log(1.0 + jnp.exp(-jnp.abs(z)))) - 0.5
    log_decay = -jnp.exp(w_log)
    a = _sigmoid(a0 + _dot_hi(wa, a2_ref[...]))
    gate = _dot_hi(_sigmoid(lora[:, LANES:2 * LANES]), g2_ref[...])
    if has_vres:
        vmix = _sigmoid(v0 + _dot_hi(lora[:, 2 * LANES:3 * LANES], v2_ref[...]))
        v = v + (vf_ref[...] - v) * vmix
    vout_ref[...] = v

    head_sum = lambda t: _dot_hi(t, bd_ref[...])
    kk = k * k_k
    kk = kk / jnp.maximum(jnp.sqrt(head_sum(kk * kk)), 1e-12)
    k = k * (1.0 + (a - 1.0) * k_a)
    bonus = head_sum(r * k * r_k) * v
    b = kk * a

    ti = lax.broadcasted_iota(jnp.int32, (ch, ch), 0)
    si = lax.broadcasted_iota(jnp.int32, (ch, ch), 1)
    cum = _dot_hi((si <= ti).astype(F32), log_decay)
    cum_end = cum[ch - 1:ch, :]
    e_neg = jnp.exp(-cum)
    e_end = jnp.exp(cum_end - cum)
    kt = kk * jnp.exp(cum - log_decay)
    rt = r * jnp.exp(cum)
    bt = b * e_neg
    kq = k * e_neg
    kh = k * e_end
    bh = b * e_end
    unit = (lax.broadcasted_iota(jnp.int32, (8, hd), 0) == 0).astype(F32)
    w_end_t = lax.dot_general(jnp.broadcast_to(jnp.exp(cum_end), (8, C_WIDTH)), unit,
                              (((0,), (0,)), ((), ())), precision=HIGHEST,
                              preferred_element_type=F32)

    tj = lax.broadcasted_iota(jnp.int32, (ch, 2 * ch), 0)
    sj = lax.broadcasted_iota(jnp.int32, (ch, 2 * ch), 1) % ch
    strict = sj < tj
    incl = sj <= tj
    eye = (si == ti).astype(F32)
    zeros_v = jnp.zeros((ch, hd), F32)
    nt = (((1,), (1,)), ((), ()))
    tn = (((0,), (0,)), ((), ()))

    for h in range(C_HEADS):
        hs = slice(h * hd, (h + 1) * hd)
        s_t = state_ref[h]
        v_h = v[:, hs]
        rhs = jnp.concatenate([bt[:, hs], kq[:, hs]], axis=0)
        g1 = jnp.where(strict, lax.dot_general(kt[:, hs], rhs, nt, precision=HIGHEST,
                                               preferred_element_type=F32), 0.0)
        g2 = jnp.where(incl, lax.dot_general(rt[:, hs], rhs, nt, precision=HIGHEST,
                                             preferred_element_type=F32), 0.0)
        npow = -g1[:, :ch]
        tinv = eye + npow
        for _ in range(int(np.log2(ch)) - 1):
            npow = _dot_hi(npow, npow)
            tinv = tinv + _dot_hi(tinv, npow)
        x = _dot_hi(kt[:, hs], s_t) + _dot_hi(g1, jnp.concatenate([zeros_v, v_h], axis=0))
        u = _dot_hi(tinv, x)
        y_ref[:, hs] = _dot_hi(rt[:, hs], s_t) + _dot_hi(g2, jnp.concatenate([-u, v_h], axis=0))
        upd = lax.dot_general(jnp.concatenate([kh[:, hs], bh[:, hs]], axis=0),
                              jnp.concatenate([v_h, -u], axis=0), tn, precision=HIGHEST,
                              preferred_element_type=F32)
        state_ref[h] = w_end_t[hs, :] * s_t + upd

    y = y_ref[...]
    mean = head_sum(y) * (1.0 / hd)
    d = y - mean
    var = head_sum(d * d) * (1.0 / hd)
    yn = d * lax.rsqrt(var + RWKV_GN_EPS) * ln_w + ln_b
    yc_ref[...] = ((yn + bonus) * gate).astype(BF16)


def _rwkv(rkv, lora, v_first, mu_rkv, vecs, w2, a2, g2, v2, bd, batch, seq):
    n = rkv.shape[0]
    ch = RWKV_CHUNK
    nc = seq // ch
    has_vres = v_first is not None
    row = lambda width: pl.BlockSpec((ch, width), lambda b, c: (b * nc + c, 0))
    ins = [rkv, lora] + ([v_first] if has_vres else []) + [mu_rkv, vecs, w2, a2, g2] \
        + ([v2] if has_vres else []) + [bd]
    specs = [row(3 * C_WIDTH), row(LORA_COLS)] + ([row(C_WIDTH)] if has_vres else []) \
        + [_full((3, C_WIDTH)), _full((8, C_WIDTH)), _full((LANES, C_WIDTH)),
           _full((LANES, C_WIDTH)), _full((LANES, C_WIDTH))] \
        + ([_full((LANES, C_WIDTH))] if has_vres else []) + [_full((C_WIDTH, C_WIDTH))]
    return pl.pallas_call(
        functools.partial(_rwkv_kernel, has_vres=has_vres),
        grid=(batch, nc),
        in_specs=specs,
        out_specs=[row(C_WIDTH), row(C_WIDTH)],
        out_shape=[jax.ShapeDtypeStruct((n, C_WIDTH), BF16),
                   jax.ShapeDtypeStruct((n, C_WIDTH), F32)],
        scratch_shapes=[pltpu.VMEM((C_HEADS, HEAD_DIM, HEAD_DIM), F32),
                        pltpu.VMEM((8, 3 * C_WIDTH), F32),
                        pltpu.VMEM((8, LORA_HALF), F32),
                        pltpu.VMEM((ch, C_WIDTH), F32)],
        compiler_params=_params("parallel", "arbitrary"),
        name="rwkv7_chunked",
    )(*ins)


def _merge_kernel(x_ref, g_ref, ya_ref, o0_ref, l0_ref, o1_ref, l1_ref, o2_ref, l2_ref, yc_ref,
                  wg_ref, bg_ref, wa_ref, wb_ref, wc_ref, wo_ref, g2_ref, wr_ref, br_ref,
                  xm_ref, h2_ref, comb_ref):
    x = x_ref[...]
    hb = _rmsnorm(x, g_ref[...]).astype(BF16)

    l0, l1, l2 = l0_ref[...], l1_ref[...], l2_ref[...]
    lm = jnp.maximum(jnp.maximum(l0, l1), l2)
    e0, e1, e2 = jnp.exp(l0 - lm), jnp.exp(l1 - lm), jnp.exp(l2 - lm)
    yb = ((e0 * o0_ref[...] + e1 * o1_ref[...] + e2 * o2_ref[...]) / (e0 + e1 + e2)).astype(BF16)

    merged = None
    for j, (y, w_ref) in enumerate(((ya_ref[...], wa_ref), (yb, wb_ref), (yc_ref[...], wc_ref))):
        cs = slice(j * D_MODEL, (j + 1) * D_MODEL)
        gate = _sigmoid(jnp.dot(hb, wg_ref[:, cs], preferred_element_type=F32) + bg_ref[:, cs])
        term = gate * jnp.dot(y, w_ref[...], preferred_element_type=F32)
        merged = term if merged is None else merged + term
    xm = x + jnp.dot(merged.astype(BF16), wo_ref[...], preferred_element_type=F32)
    xm_ref[...] = xm

    h2 = _rmsnorm(xm, g2_ref[...])
    h2_ref[...] = h2.astype(BF16)

    logits = lax.dot_general(wr_ref[...], h2, (((1,), (1,)), ((), ())), precision=HIGHEST,
                             preferred_element_type=F32) + br_ref[:, 0:1]
    gl = [logits[g:g + 1, :] for g in range(N_GROUPS)]
    gmax = functools.reduce(jnp.maximum, gl)
    gsum = functools.reduce(lambda s, t: s + t, [jnp.exp(t - gmax) for t in gl])
    gp = 1.0 / gsum
    taken = jnp.zeros_like(gmax, dtype=jnp.bool_)
    sel = []
    for g in range(N_GROUPS):
        pick = (gl[g] == gmax) & jnp.logical_not(taken)
        sel.append(pick)
        taken = taken | pick
    el = []
    for e in range(EXPERTS_PER_GROUP):
        acc = jnp.zeros_like(gmax)
        for g in range(N_GROUPS):
            r0 = N_GROUPS + g * EXPERTS_PER_GROUP + e
            acc = jnp.where(sel[g], logits[r0:r0 + 1, :], acc)
        el.append(acc)
    emax = functools.reduce(jnp.maximum, el)
    ex = [jnp.exp(t - emax) for t in el]
    esum = functools.reduce(lambda s, t: s + t, ex)
    p = [t / esum for t in ex]
    p1 = functools.reduce(jnp.maximum, p)
    taken = jnp.zeros_like(gmax, dtype=jnp.bool_)
    first = []
    for e in range(EXPERTS_PER_GROUP):
        pick = (p[e] == p1) & jnp.logical_not(taken)
        first.append(pick)
        taken = taken | pick
    rest = [jnp.where(first[e], -1.0, p[e]) for e in range(EXPERTS_PER_GROUP)]
    p2 = functools.reduce(jnp.maximum, rest)
    taken = jnp.zeros_like(gmax, dtype=jnp.bool_)
    second = []
    for e in range(EXPERTS_PER_GROUP):
        pick = (rest[e] == p2) & jnp.logical_not(taken)
        second.append(pick)
        taken = taken | pick
    denom = p1 + p2
    for e in range(EXPERTS_PER_GROUP):
        in_group = jnp.where(first[e], p1 / denom, jnp.where(second[e], p2 / denom, 0.0))
        for g in range(N_GROUPS):
            r0 = g * EXPERTS_PER_GROUP + e
            comb_ref[r0:r0 + 1, :] = jnp.where(sel[g], gp * in_group, 0.0)


def _merge(x2, g, ya, attn, yc, wg, bg, wa, wb, wc, wo, g2, wr, br):
    n = x2.shape[0]
    tm = ROW_TILE
    row = lambda width: pl.BlockSpec((tm, width), lambda i: (i, 0))
    attn_flat = [t for pair in attn for t in pair]
    return pl.pallas_call(
        _merge_kernel,
        grid=(n // tm,),
        in_specs=[row(D_MODEL), _full((1, D_MODEL)), row(A_WIDTH)] + [row(B_OUT)] * 6
        + [row(C_WIDTH), _full((D_MODEL, 3 * D_MODEL)), _full((1, 3 * D_MODEL)),
           _full((A_WIDTH, D_MODEL)), _full((B_OUT, D_MODEL)), _full((C_WIDTH, D_MODEL)),
           _full((D_MODEL, D_MODEL)), _full((1, D_MODEL)), _full((32, D_MODEL)), _full((32, LANES))],
        out_specs=[row(D_MODEL), row(D_MODEL), pl.BlockSpec((N_EXPERTS, tm), lambda i: (0, i))],
        out_shape=[jax.ShapeDtypeStruct((n, D_MODEL), F32),
                   jax.ShapeDtypeStruct((n, D_MODEL), BF16),
                   jax.ShapeDtypeStruct((N_EXPERTS, n), F32)],
        compiler_params=_params("parallel"),
        name="merge_router",
    )(x2, g, ya, *attn_flat, yc, wg, bg, wa, wb, wc, wo, g2, wr, br)


def _moe_kernel(xm_ref, h2_ref, comb_ref, wgu_ref, wd_ref, gf_ref, out_ref, *, final_norm):
    e = pl.program_id(1)

    @pl.when(e == 0)
    def _():
        out_ref[...] = xm_ref[...]

    gu = jnp.dot(h2_ref[...], wgu_ref[0], preferred_element_type=F32)
    lane = lax.broadcasted_iota(jnp.int32, comb_ref.shape, 1)
    cw = jnp.sum(jnp.where(lane == e, comb_ref[...], 0.0), axis=-1, keepdims=True)
    gate = gu[:, :EXPERT_FF]
    hid = gate * _sigmoid(gate) * gu[:, EXPERT_FF:] * cw
    out_ref[...] += jnp.dot(hid.astype(BF16), wd_ref[0], preferred_element_type=F32)

    if final_norm:
        @pl.when(e == N_EXPERTS - 1)
        def _():
            out_ref[...] = _rmsnorm(out_ref[...], gf_ref[...])


def _moe(xm, h2, comb, wgu, wd, g_final, final_norm):
    n = xm.shape[0]
    tm = MOE_ROW_TILE
    return pl.pallas_call(
        functools.partial(_moe_kernel, final_norm=final_norm),
        grid=(n // tm, N_EXPERTS),
        in_specs=[pl.BlockSpec((tm, D_MODEL), lambda i, e: (i, 0)),
                  pl.BlockSpec((tm, D_MODEL), lambda i, e: (i, 0)),
                  pl.BlockSpec((tm, N_EXPERTS), lambda i, e: (i, 0)),
                  pl.BlockSpec((1, D_MODEL, 2 * EXPERT_FF), lambda i, e: (e, 0, 0)),
                  pl.BlockSpec((1, EXPERT_FF, D_MODEL), lambda i, e: (e, 0, 0)),
                  _full((1, D_MODEL))],
        out_specs=pl.BlockSpec((tm, D_MODEL), lambda i, e: (i, 0)),
        out_shape=jax.ShapeDtypeStruct((n, D_MODEL), F32),
        compiler_params=_params("parallel", "arbitrary"),
        name="moe_ffn",
    )(xm, h2, comb, wgu, wd, g_final)


def _pack_lora(mu_wag, w1, a1, g1, mu_v, v1):
    d = D_MODEL
    if v1 is None:
        mu_v = jnp.zeros((d,), F32)
        v1 = jnp.zeros((d, V_LORA), F32)
    pad = jnp.zeros((d, LORA_HALF - (W_LORA + A_LORA + G_LORA + V_LORA)), F32)
    mus = (mu_wag[0], mu_wag[1], mu_wag[2], mu_v)
    ws = (w1, a1, g1, v1)
    cur = [w * (1.0 - m)[:, None] for w, m in zip(ws, mus)]
    prev = [w * m[:, None] for w, m in zip(ws, mus)]
    return jnp.concatenate(cur + [pad] + prev + [pad], axis=1)


def _pad_rows(w, lo, total=LANES):
    return jnp.zeros((total, w.shape[1]), F32).at[lo:lo + w.shape[0]].set(w)


def kernel(x, positions, norm_mix_g, w_in, gmlp_ln_g, gmlp_ln_b, gmlp_ws, gmlp_bs, rwkv_mu_rkv, rwkv_mu_wag, rwkv_w0, rwkv_w1, rwkv_w2, rwkv_a0, rwkv_a1, rwkv_a2, rwkv_g1, rwkv_g2, rwkv_k_k, rwkv_k_a, rwkv_r_k, rwkv_ln_w, rwkv_ln_b, rwkv_mu_v, rwkv_v0, rwkv_v1, rwkv_v2, w_branch_a, w_branch_b, w_branch_c, w_gate, b_gate, w_out, norm_ffn_g, router_group_w, router_group_b, router_expert_w, router_expert_b, expert_w_gate, expert_w_up, expert_w_down, final_norm_g):
    batch, seq, d = x.shape
    depth = w_in.shape[0]
    n = batch * seq
    x2 = x.reshape(n, d)
    rope = _rope_tables(positions)
    head_block = jnp.asarray(
        (np.arange(C_WIDTH)[:, None] // HEAD_DIM) == (np.arange(C_WIDTH)[None, :] // HEAD_DIM), F32)
    v_first = None
    for l in range(depth):
        vres = l > 0
        lora_w = _pack_lora(rwkv_mu_wag[l], rwkv_w1[l], rwkv_a1[l], rwkv_g1[l],
                            rwkv_mu_v[l - 1] if vres else None, rwkv_v1[l - 1] if vres else None)
        w_all = jnp.concatenate([w_in[l], lora_w], axis=1).astype(BF16)
        ya, qkv, rkv, lora = _inproj(
            x2, norm_mix_g[l][None], w_all, gmlp_ln_g[l][None], gmlp_ln_b[l][None], gmlp_ws[l],
            gmlp_bs[l].T, rope)

        attn = [_dilated_attention(qkv, batch, seq, g, dil)
                for g, (_, dil) in enumerate(DILATED_PATTERNS)]

        zero = jnp.zeros((C_WIDTH,), F32)
        vecs = jnp.stack([rwkv_w0[l], rwkv_a0[l], rwkv_v0[l - 1] if vres else zero, rwkv_k_k[l],
                          rwkv_k_a[l], rwkv_r_k[l].reshape(C_WIDTH), rwkv_ln_w[l], rwkv_ln_b[l]])
        yc, v_c = _rwkv(rkv, lora, v_first, rwkv_mu_rkv[l], vecs,
                        _pad_rows(rwkv_w2[l], 0), _pad_rows(rwkv_a2[l], W_LORA), rwkv_g2[l],
                        _pad_rows(rwkv_v2[l - 1], 0) if vres else None, head_block, batch, seq)
        if l == 0:
            v_first = v_c

        wr = jnp.concatenate([router_group_w[l], router_expert_w[l]], axis=1).T
        wr = jnp.zeros((32, d), F32).at[:wr.shape[0]].set(wr)
        br = jnp.concatenate([router_group_b[l], router_expert_b[l]])
        br = jnp.zeros((32, LANES), F32).at[:br.shape[0], 0].set(br)
        xm, h2, comb_t = _merge(
            x2, norm_mix_g[l][None], ya, attn, yc, w_gate[l].astype(BF16), b_gate[l][None],
            w_branch_a[l].astype(BF16), w_branch_b[l].astype(BF16), w_branch_c[l].astype(BF16),
            w_out[l].astype(BF16), norm_ffn_g[l][None], wr, br)

        wgu = jnp.concatenate([expert_w_gate[l], expert_w_up[l]], axis=2).astype(BF16)
        x2 = _moe(xm, h2, comb_t.T, wgu, expert_w_down[l].astype(BF16), final_norm_g[None],
                  final_norm=(l == depth - 1))
    return x2.reshape(batch, seq, d)
```

```python
import functools

import numpy as np
import jax
import jax.numpy as jnp
from jax import lax
from jax.experimental import pallas as pl
from jax.experimental.pallas import tpu as pltpu

F32 = jnp.float32
BF16 = jnp.bfloat16
HIGHEST = lax.Precision.HIGHEST

D_MODEL = 1024
HEAD_DIM = 64
A_GROUPS = 4
A_WIDTH = 512
CHUNK = 128
DILATED_PATTERNS = ((128, 1), (512, 4), (2048, 16))
B_HEADS_PER_GROUP = 4
B_WIDTH = 768
B_OUT = 256
BLOCK = 128
ROPE_THETA = 500000.0
ROPE_DIMS = 16
C_HEADS = 8
C_WIDTH = 512
W_LORA, A_LORA, V_LORA, G_LORA = 64, 64, 32, 128
RWKV_GN_EPS = HEAD_DIM * 1e-5
IN_COLS = 2 * A_WIDTH + 3 * B_WIDTH + 3 * C_WIDTH
LORA_HALF = 384
LORA_COLS = 2 * LORA_HALF
N_GROUPS = 4
EXPERTS_PER_GROUP = 4
N_EXPERTS = 16
EXPERT_FF = 512
RMS_EPS = 1e-6
LN_EPS = 1e-5

LANES = 128
ROW_TILE = 256
RWKV_CHUNK = 64
MOE_ROW_TILE = 1024
VMEM_LIMIT = 56 * 1024 * 1024
NEG_BIG = -1e30


def _params(*sem):
    return pltpu.CompilerParams(dimension_semantics=sem, vmem_limit_bytes=VMEM_LIMIT)


def _full(shape):
    nd = len(shape)
    return pl.BlockSpec(shape, lambda *_: (0,) * nd)


def _gelu_tanh(x):
    return 0.5 * x * (1.0 + jnp.tanh(0.7978845608028654 * (x + 0.044715 * (x * x * x))))


def _sigmoid(x):
    return 1.0 / (1.0 + jnp.exp(-x))


def _rmsnorm(x, g):
    return x * lax.rsqrt(jnp.mean(x * x, axis=-1, keepdims=True) + RMS_EPS) * g


def _rope_table_kernel(pos_ref, consts_ref, c_ref, sa_ref, sb_ref):
    ang = pos_ref[...].astype(F32) * consts_ref[0:1, :]
    c_ref[...] = jnp.cos(ang)
    s = jnp.sin(ang)
    sa_ref[...] = -s * consts_ref[1:2, :]
    sb_ref[...] = s * consts_ref[2:3, :]


def _rope_tables(positions):
    n = positions.size
    half = ROPE_DIMS // 2
    inv_freq = ROPE_THETA ** (-jnp.arange(half, dtype=F32) / half)
    lane = np.arange(LANES) % HEAD_DIM
    in_rope = lane < ROPE_DIMS
    invf_row = jnp.where(in_rope, inv_freq[lane % half], 0.0)
    consts = jnp.zeros((8, LANES), F32)
    consts = consts.at[0].set(invf_row)
    consts = consts.at[1].set(jnp.asarray(lane < half, F32))
    consts = consts.at[2].set(jnp.asarray((lane >= half) & in_rope, F32))
    tm = 1024
    out = jax.ShapeDtypeStruct((n, LANES), F32)
    return pl.pallas_call(
        _rope_table_kernel,
        grid=(n // tm,),
        in_specs=[pl.BlockSpec((tm, 1), lambda i: (i, 0)), _full((8, LANES))],
        out_specs=[pl.BlockSpec((tm, LANES), lambda i: (i, 0))] * 3,
        out_shape=[out, out, out],
        compiler_params=_params("parallel"),
        name="rope_tables",
    )(positions.reshape(n, 1), consts)


def _inproj_kernel(x_ref, g_ref, w_ref, lng_ref, lnb_ref, ws_ref, bst_ref, c_ref, sa_ref, sb_ref,
                   ya_ref, qkv_ref, rkv_ref, lora_ref):
    tm = x_ref.shape[0]
    hb = _rmsnorm(x_ref[...], g_ref[...]).astype(BF16)

    def proj(lo, width):
        return jnp.dot(hb, w_ref[:, lo:lo + width], preferred_element_type=F32)

    uv = proj(0, 2 * A_WIDTH)
    u = _gelu_tanh(uv[:, :A_WIDTH])
    v = _gelu_tanh(uv[:, A_WIDTH:])
    mu = jnp.mean(v, axis=-1, keepdims=True)
    d = v - mu
    var = jnp.mean(d * d, axis=-1, keepdims=True)
    vn = (d * lax.rsqrt(var + LN_EPS) * lng_ref[...] + lnb_ref[...]).astype(BF16)
    row = lax.broadcasted_iota(jnp.int32, (CHUNK, CHUNK), 0)
    col = lax.broadcasted_iota(jnp.int32, (CHUNK, CHUNK), 1)
    for g in range(A_GROUPS):
        wg = jnp.where(row >= col, ws_ref[g], 0.0).astype(BF16)
        bias = bst_ref[:, g:g + 1]
        cs = slice(g * LANES, (g + 1) * LANES)
        for c in range(tm // CHUNK):
            rs = slice(c * CHUNK, (c + 1) * CHUNK)
            mixed = jnp.dot(wg, vn[rs, cs], preferred_element_type=F32) + bias
            ya_ref[rs, cs] = (u[rs, cs] * mixed).astype(BF16)

    cos, sa, sb = c_ref[...], sa_ref[...], sb_ref[...]
    q_off = 2 * A_WIDTH
    for part, scale in ((0, HEAD_DIM ** -0.5), (1, 1.0)):
        acc = proj(q_off + part * B_WIDTH, B_WIDTH)
        for c in range(B_WIDTH // LANES):
            t = acc[:, c * LANES:(c + 1) * LANES]
            rot = t * cos + pltpu.roll(t, LANES - 8, 1) * sa + pltpu.roll(t, 8, 1) * sb
            lo = part * B_WIDTH + c * LANES
            qkv_ref[:, lo:lo + LANES] = (rot * scale).astype(BF16)
    qkv_ref[:, 2 * B_WIDTH:] = proj(q_off + 2 * B_WIDTH, B_WIDTH).astype(BF16)

    rkv_ref[...] = proj(q_off + 3 * B_WIDTH, 3 * C_WIDTH)
    lora_ref[...] = proj(IN_COLS, LORA_COLS)


def _inproj(x2, g, w_all, ln_g, ln_b, ws, bs_t, rope):
    n = x2.shape[0]
    tm = ROW_TILE
    wcols = w_all.shape[1]
    row = lambda width: pl.BlockSpec((tm, width), lambda i: (i, 0))
    return pl.pallas_call(
        _inproj_kernel,
        grid=(n // tm,),
        in_specs=[row(D_MODEL), _full((1, D_MODEL)), _full((D_MODEL, wcols)),
                  _full((1, A_WIDTH)), _full((1, A_WIDTH)), _full((A_GROUPS, CHUNK, CHUNK)),
                  _full((CHUNK, A_GROUPS)), row(LANES), row(LANES), row(LANES)],
        out_specs=[row(A_WIDTH), row(3 * B_WIDTH), row(3 * C_WIDTH), row(LORA_COLS)],
        out_shape=[jax.ShapeDtypeStruct((n, A_WIDTH), BF16),
                   jax.ShapeDtypeStruct((n, 3 * B_WIDTH), BF16),
                   jax.ShapeDtypeStruct((n, 3 * C_WIDTH), F32),
                   jax.ShapeDtypeStruct((n, LORA_COLS), F32)],
        compiler_params=_params("parallel"),
        name="inproj",
    )(x2, g, w_all, ln_g, ln_b, ws, bs_t, *rope)


def _attn_kernel(q_ref, k_ref, v_ref, o_ref, l_ref):
    seq = q_ref.shape[1]
    nb = seq // BLOCK
    hd = HEAD_DIM

    def block(qb, kw, vw, valid):
        outs, lses = [], []
        for j in range(B_HEADS_PER_GROUP):
            hs = slice(j * hd, (j + 1) * hd)
            s = lax.dot_general(qb[:, hs], kw[:, hs], (((1,), (1,)), ((), ())),
                                preferred_element_type=F32)
            s = jnp.where(valid, s, NEG_BIG)
            m = jnp.max(s, axis=-1, keepdims=True)
            p = jnp.exp(s - m)
            l = jnp.sum(p, axis=-1, keepdims=True)
            o = jnp.dot(p.astype(BF16), vw[:, hs], preferred_element_type=F32) / l
            outs.append(o)
            lses.append(jnp.broadcast_to(m + jnp.log(l), (BLOCK, hd)))
        return jnp.concatenate(outs, axis=-1), jnp.concatenate(lses, axis=-1)

    qi = lax.broadcasted_iota(jnp.int32, (BLOCK, BLOCK), 0)
    kj = lax.broadcasted_iota(jnp.int32, (BLOCK, BLOCK), 1)
    o, l = block(q_ref[0, 0:BLOCK, :], k_ref[0, 0:BLOCK, :], v_ref[0, 0:BLOCK, :], kj <= qi)
    o_ref[0, 0:BLOCK, :] = o
    l_ref[0, 0:BLOCK, :] = l

    if nb > 1:
        qi2 = lax.broadcasted_iota(jnp.int32, (BLOCK, 2 * BLOCK), 0)
        kj2 = lax.broadcasted_iota(jnp.int32, (BLOCK, 2 * BLOCK), 1)
        valid2 = (kj2 >= qi2) & (kj2 <= qi2 + BLOCK)

        def body(n, carry):
            start = pl.multiple_of(n * BLOCK, BLOCK)
            prev = pl.multiple_of(start - BLOCK, BLOCK)
            o, l = block(q_ref[0, pl.ds(start, BLOCK), :], k_ref[0, pl.ds(prev, 2 * BLOCK), :],
                         v_ref[0, pl.ds(prev, 2 * BLOCK), :], valid2)
            o_ref[0, pl.ds(start, BLOCK), :] = o
            l_ref[0, pl.ds(start, BLOCK), :] = l
            return carry

        lax.fori_loop(1, nb, body, 0)


def _dilated_attention(qkv, batch, seq, group, dilation):
    sub = seq // dilation
    ncb = 3 * B_WIDTH // B_OUT
    view = qkv.reshape(batch, sub, dilation * 3 * B_WIDTH)
    spec = lambda part: pl.BlockSpec(
        (1, sub, B_OUT), lambda b, r: (b, 0, r * ncb + part * (B_WIDTH // B_OUT) + group))
    ospec = pl.BlockSpec((1, sub, B_OUT), lambda b, r: (b, 0, r))
    oshape = jax.ShapeDtypeStruct((batch, sub, dilation * B_OUT), F32)
    o, l = pl.pallas_call(
        _attn_kernel,
        grid=(batch, dilation),
        in_specs=[spec(0), spec(1), spec(2)],
        out_specs=[ospec, ospec],
        out_shape=[oshape, oshape],
        compiler_params=_params("parallel", "parallel"),
        name=f"dilated_attn_d{dilation}",
    )(view, view, view)
    return o.reshape(batch * seq, B_OUT), l.reshape(batch * seq, B_OUT)


def _shift_rows(x, prev_row):
    rolled = pltpu.roll(x, 1, 0)
    first = lax.broadcasted_iota(jnp.int32, x.shape, 0) == 0
    return jnp.where(first, prev_row, rolled)


NN = (((1,), (0,)), ((), ()))
NT = (((1,), (1,)), ((), ()))
TN = (((0,), (0,)), ((), ()))


def _dot(a, b, dims=NN):
    return lax.dot_general(a.astype(BF16), b.astype(BF16), dims, preferred_element_type=F32)


def _split(a, pieces):
    out = []
    for _ in range(pieces):
        p = a.astype(BF16)
        out.append(p)
        a = a - p.astype(F32)
    return out


def _dot3(a, b):
    (ah, al), (bh, bl) = _split(a, 2), _split(b, 2)
    return _dot(ah, bh) + _dot(ah, bl) + _dot(al, bh)


def _head_sum(x):
    lo = lax.broadcasted_iota(jnp.int32, (x.shape[0], LANES), 1) < HEAD_DIM
    outs = []
    for c in range(x.shape[1] // LANES):
        xc = x[:, c * LANES:(c + 1) * LANES]
        s_lo = jnp.sum(jnp.where(lo, xc, 0.0), axis=-1, keepdims=True)
        s_hi = jnp.sum(jnp.where(lo, 0.0, xc), axis=-1, keepdims=True)
        outs.append(jnp.where(lo, s_lo, s_hi))
    return jnp.concatenate(outs, axis=-1)


def _rwkv_kernel(*refs, has_vres):
    if has_vres:
        (rkv_ref, lora_ref, vf_ref, mu_ref, vec_ref, w2_ref, a2_ref, g2_ref, v2_ref,
         yc_ref, vout_ref, state_ref, prkv_ref, plora_ref, y_ref) = refs
    else:
        (rkv_ref, lora_ref, mu_ref, vec_ref, w2_ref, a2_ref, g2_ref,
         yc_ref, vout_ref, state_ref, prkv_ref, plora_ref, y_ref) = refs
    ch = rkv_ref.shape[0]
    hd = HEAD_DIM

    @pl.when(pl.program_id(1) == 0)
    def _():
        state_ref[...] = jnp.zeros_like(state_ref)
        prkv_ref[...] = jnp.zeros_like(prkv_ref)
        plora_ref[...] = jnp.zeros_like(plora_ref)

    rkv = rkv_ref[...]
    rkv_prev = _shift_rows(rkv, prkv_ref[0:1, :])
    prkv_ref[0:1, :] = rkv[ch - 1:ch, :]
    lora2 = lora_ref[:, LORA_HALF:]
    lora = lora_ref[:, :LORA_HALF] + _shift_rows(lora2, plora_ref[0:1, :])
    plora_ref[0:1, :] = lora2[ch - 1:ch, :]

    w0, a0, v0 = vec_ref[0:1, :], vec_ref[1:2, :], vec_ref[2:3, :]
    k_k, k_a, r_k = vec_ref[3:4, :], vec_ref[4:5, :], vec_ref[5:6, :]
    ln_w, ln_b = vec_ref[6:7, :], vec_ref[7:8, :]

    def mix(i):
        cur = rkv[:, i * C_WIDTH:(i + 1) * C_WIDTH]
        prev = rkv_prev[:, i * C_WIDTH:(i + 1) * C_WIDTH]
        return cur + (prev - cur) * mu_ref[i:i + 1, :]

    r, k, v = mix(0), mix(1), mix(2)
    wa = lora[:, 0:LANES]
    z = w0 + _dot3(jnp.tanh(wa), w2_ref[...])
    w_log = -(jnp.maximum(-z, 0.0) + jnp.log(1.0 + jnp.exp(-jnp.abs(z)))) - 0.5
    log_decay = -jnp.exp(w_log)
    a = _sigmoid(a0 + _dot(wa, a2_ref[...]))
    gate = _dot(_sigmoid(lora[:, LANES:2 * LANES]), g2_ref[...])
    if has_vres:
        vmix = _sigmoid(v0 + _dot(lora[:, 2 * LANES:3 * LANES], v2_ref[...]))
        v = v + (vf_ref[...] - v) * vmix
    vout_ref[...] = v

    kk = k * k_k
    kk = kk / jnp.maximum(jnp.sqrt(_head_sum(kk * kk)), 1e-12)
    k = k * (1.0 + (a - 1.0) * k_a)
    bonus = _head_sum(r * k * r_k) * v
    b = kk * a

    ti = lax.broadcasted_iota(jnp.int32, (ch, ch), 0)
    si = lax.broadcasted_iota(jnp.int32, (ch, ch), 1)
    tri = (si <= ti).astype(BF16)
    cum = functools.reduce(lambda s, t: s + t, [_dot(tri, p) for p in _split(log_decay, 3)])
    cum_end = cum[ch - 1:ch, :]
    e_neg = jnp.exp(-cum)
    e_end = jnp.exp(cum_end - cum)
    kt = kk * jnp.exp(cum - log_decay)
    rt = r * jnp.exp(cum)
    bt = b * e_neg
    kq = k * e_neg
    kh = k * e_end
    bh = b * e_end
    unit = (lax.broadcasted_iota(jnp.int32, (8, LANES), 0) == 0).astype(F32)
    w_end_t = lax.dot_general(jnp.broadcast_to(jnp.exp(cum_end), (8, C_WIDTH)), unit, TN,
                              precision=HIGHEST, preferred_element_type=F32)

    lane = lax.broadcasted_iota(jnp.int32, (ch, LANES), 1)
    lo = lane < hd
    lo2 = lax.broadcasted_iota(jnp.int32, (2 * ch, LANES), 1) < hd
    row4 = lax.broadcasted_iota(jnp.int32, (4 * ch, LANES), 0)
    col4 = lax.broadcasted_iota(jnp.int32, (4 * ch, LANES), 1) % ch
    t4 = row4 % ch
    tri_mask = (col4 < t4) | ((col4 == t4) & ((row4 // ch) % 2 == 1))
    eye_right = (lane - hd == lax.broadcasted_iota(jnp.int32, (ch, LANES), 0)).astype(F32)
    brow = lax.broadcasted_iota(jnp.int32, (LANES, LANES), 0) // hd
    bcol = lax.broadcasted_iota(jnp.int32, (LANES, LANES), 1) // hd
    block_diag = brow == bcol
    zeros = jnp.zeros((ch, LANES), F32)

    pairs = range(C_HEADS // 2)
    lanes_of = [slice(p * LANES, (p + 1) * LANES) for p in pairs]
    s_bd = [state_ref[p] for p in pairs]
    l2 = [jnp.concatenate([kt[:, ps], rt[:, ps]], axis=0) for ps in lanes_of]
    ss = [_dot(l2[p], s_bd[p]) for p in pairs]
    g = [jnp.where(tri_mask,
                   _dot(jnp.concatenate([jnp.where(lo2, l2[p], 0.0), jnp.where(lo2, 0.0, l2[p])],
                                        axis=0),
                        jnp.concatenate([bt[:, ps], kq[:, ps]], axis=0), NT), 0.0)
         for p, ps in zip(pairs, lanes_of)]
    xv = [_dot(jnp.concatenate([g[p][0:ch], g[p][2 * ch:3 * ch]], axis=0),
               jnp.concatenate([zeros, v[:, ps]], axis=0))
          for p, ps in zip(pairs, lanes_of)]
    heads = [(p, i) for p in pairs for i in range(2)]
    n0 = [-g[p][2 * i * ch:(2 * i + 1) * ch] for p, i in heads]
    w = [_dot(n[:, :ch], jnp.where(lo, n, eye_right)) + jnp.where(lo, 0.0, eye_right) for n in n0]
    for _ in range(int(np.log2(ch)) - 1):
        w = [_dot(t[:, :ch], t) + jnp.where(lo, 0.0, t) for t in w]
    us = [_dot(w[2 * p + i],
               jnp.concatenate([zeros, ss[p][0:ch] + xv[p][i * ch:(i + 1) * ch]], axis=0))
          for p, i in heads]
    u = [jnp.where(lo, us[2 * p], us[2 * p + 1]) for p in pairs]
    yv = [_dot(jnp.concatenate([g[p][ch:2 * ch], g[p][3 * ch:4 * ch]], axis=0),
               jnp.concatenate([-u[p], v[:, ps]], axis=0))
          for p, ps in zip(pairs, lanes_of)]
    upd = [_dot(jnp.concatenate([kh[:, ps], bh[:, ps]], axis=0),
                jnp.concatenate([v[:, ps], -u[p]], axis=0), TN)
           for p, ps in zip(pairs, lanes_of)]
    for p, ps in zip(pairs, lanes_of):
        y_ref[:, ps] = ss[p][ch:2 * ch] + jnp.where(lo, yv[p][0:ch], yv[p][ch:2 * ch])
        state_ref[p] = w_end_t[ps, :] * s_bd[p] + jnp.where(block_diag, upd[p], 0.0)

    y = y_ref[...]
    d = y - _head_sum(y) * (1.0 / hd)
    var = _head_sum(d * d) * (1.0 / hd)
    yn = d * lax.rsqrt(var + RWKV_GN_EPS) * ln_w + ln_b
    yc_ref[...] = ((yn + bonus) * gate).astype(BF16)


def _rwkv(rkv, lora, v_first, mu_rkv, vecs, w2, a2, g2, v2, batch, seq):
    n = rkv.shape[0]
    ch = RWKV_CHUNK
    nc = seq // ch
    has_vres = v_first is not None
    row = lambda width: pl.BlockSpec((ch, width), lambda b, c: (b * nc + c, 0))
    ins = [rkv, lora] + ([v_first] if has_vres else []) + [mu_rkv, vecs, w2, a2, g2] \
        + ([v2] if has_vres else [])
    specs = [row(3 * C_WIDTH), row(LORA_COLS)] + ([row(C_WIDTH)] if has_vres else []) \
        + [_full((3, C_WIDTH)), _full((8, C_WIDTH)), _full((LANES, C_WIDTH)),
           _full((LANES, C_WIDTH)), _full((LANES, C_WIDTH))] \
        + ([_full((LANES, C_WIDTH))] if has_vres else [])
    return pl.pallas_call(
        functools.partial(_rwkv_kernel, has_vres=has_vres),
        grid=(batch, nc),
        in_specs=specs,
        out_specs=[row(C_WIDTH), row(C_WIDTH)],
        out_shape=[jax.ShapeDtypeStruct((n, C_WIDTH), BF16),
                   jax.ShapeDtypeStruct((n, C_WIDTH), F32)],
        scratch_shapes=[pltpu.VMEM((C_HEADS // 2, LANES, LANES), F32),
                        pltpu.VMEM((8, 3 * C_WIDTH), F32),
                        pltpu.VMEM((8, LORA_HALF), F32),
                        pltpu.VMEM((ch, C_WIDTH), F32)],
        compiler_params=_params("parallel", "arbitrary"),
        name="rwkv7_chunked",
    )(*ins)


def _merge_kernel(x_ref, g_ref, ya_ref, o0_ref, l0_ref, o1_ref, l1_ref, o2_ref, l2_ref, yc_ref,
                  wg_ref, bg_ref, wa_ref, wb_ref, wc_ref, wo_ref, g2_ref, wr_ref, br_ref,
                  xm_ref, h2_ref, comb_ref):
    x = x_ref[...]
    hb = _rmsnorm(x, g_ref[...]).astype(BF16)

    l0, l1, l2 = l0_ref[...], l1_ref[...], l2_ref[...]
    lm = jnp.maximum(jnp.maximum(l0, l1), l2)
    e0, e1, e2 = jnp.exp(l0 - lm), jnp.exp(l1 - lm), jnp.exp(l2 - lm)
    yb = ((e0 * o0_ref[...] + e1 * o1_ref[...] + e2 * o2_ref[...]) / (e0 + e1 + e2)).astype(BF16)

    merged = None
    for j, (y, w_ref) in enumerate(((ya_ref[...], wa_ref), (yb, wb_ref), (yc_ref[...], wc_ref))):
        cs = slice(j * D_MODEL, (j + 1) * D_MODEL)
        gate = _sigmoid(jnp.dot(hb, wg_ref[:, cs], preferred_element_type=F32) + bg_ref[:, cs])
        term = gate * jnp.dot(y, w_ref[...], preferred_element_type=F32)
        merged = term if merged is None else merged + term
    xm = x + jnp.dot(merged.astype(BF16), wo_ref[...], preferred_element_type=F32)
    xm_ref[...] = xm

    h2 = _rmsnorm(xm, g2_ref[...])
    h2_ref[...] = h2.astype(BF16)

    logits = lax.dot_general(wr_ref[...], h2, (((1,), (1,)), ((), ())), precision=HIGHEST,
                             preferred_element_type=F32) + br_ref[:, 0:1]
    gl = [logits[g:g + 1, :] for g in range(N_GROUPS)]
    gmax = functools.reduce(jnp.maximum, gl)
    gsum = functools.reduce(lambda s, t: s + t, [jnp.exp(t - gmax) for t in gl])
    gp = 1.0 / gsum
    taken = jnp.zeros_like(gmax, dtype=jnp.bool_)
    sel = []
    for g in range(N_GROUPS):
        pick = (gl[g] == gmax) & jnp.logical_not(taken)
        sel.append(pick)
        taken = taken | pick
    el = []
    for e in range(EXPERTS_PER_GROUP):
        acc = jnp.zeros_like(gmax)
        for g in range(N_GROUPS):
            r0 = N_GROUPS + g * EXPERTS_PER_GROUP + e
            acc = jnp.where(sel[g], logits[r0:r0 + 1, :], acc)
        el.append(acc)
    emax = functools.reduce(jnp.maximum, el)
    ex = [jnp.exp(t - emax) for t in el]
    esum = functools.reduce(lambda s, t: s + t, ex)
    p = [t / esum for t in ex]
    p1 = functools.reduce(jnp.maximum, p)
    taken = jnp.zeros_like(gmax, dtype=jnp.bool_)
    first = []
    for e in range(EXPERTS_PER_GROUP):
        pick = (p[e] == p1) & jnp.logical_not(taken)
        first.append(pick)
        taken = taken | pick
    rest = [jnp.where(first[e], -1.0, p[e]) for e in range(EXPERTS_PER_GROUP)]
    p2 = functools.reduce(jnp.maximum, rest)
    taken = jnp.zeros_like(gmax, dtype=jnp.bool_)
    second = []
    for e in range(EXPERTS_PER_GROUP):
        pick = (rest[e] == p2) & jnp.logical_not(taken)
        second.append(pick)
        taken = taken | pick
    denom = p1 + p2
    for e in range(EXPERTS_PER_GROUP):
        in_group = jnp.where(first[e], p1 / denom, jnp.where(second[e], p2 / denom, 0.0))
        for g in range(N_GROUPS):
            r0 = g * EXPERTS_PER_GROUP + e
            comb_ref[r0:r0 + 1, :] = jnp.where(sel[g], gp * in_group, 0.0)


def _merge(x2, g, ya, attn, yc, wg, bg, wa, wb, wc, wo, g2, wr, br):
    n = x2.shape[0]
    tm = ROW_TILE
    row = lambda width: pl.BlockSpec((tm, width), lambda i: (i, 0))
    attn_flat = [t for pair in attn for t in pair]
    return pl.pallas_call(
        _merge_kernel,
        grid=(n // tm,),
        in_specs=[row(D_MODEL), _full((1, D_MODEL)), row(A_WIDTH)] + [row(B_OUT)] * 6
        + [row(C_WIDTH), _full((D_MODEL, 3 * D_MODEL)), _full((1, 3 * D_MODEL)),
           _full((A_WIDTH, D_MODEL)), _full((B_OUT, D_MODEL)), _full((C_WIDTH, D_MODEL)),
           _full((D_MODEL, D_MODEL)), _full((1, D_MODEL)), _full((32, D_MODEL)), _full((32, LANES))],
        out_specs=[row(D_MODEL), row(D_MODEL), pl.BlockSpec((N_EXPERTS, tm), lambda i: (0, i))],
        out_shape=[jax.ShapeDtypeStruct((n, D_MODEL), F32),
                   jax.ShapeDtypeStruct((n, D_MODEL), BF16),
                   jax.ShapeDtypeStruct((N_EXPERTS, n), F32)],
        compiler_params=_params("parallel"),
        name="merge_router",
    )(x2, g, ya, *attn_flat, yc, wg, bg, wa, wb, wc, wo, g2, wr, br)


def _moe_kernel(xm_ref, h2_ref, comb_ref, wgu_ref, wd_ref, gf_ref, out_ref, *, final_norm):
    e = pl.program_id(1)

    @pl.when(e == 0)
    def _():
        out_ref[...] = xm_ref[...]

    gu = jnp.dot(h2_ref[...], wgu_ref[0], preferred_element_type=F32)
    lane = lax.broadcasted_iota(jnp.int32, comb_ref.shape, 1)
    cw = jnp.sum(jnp.where(lane == e, comb_ref[...], 0.0), axis=-1, keepdims=True)
    gate = gu[:, :EXPERT_FF]
    hid = gate * _sigmoid(gate) * gu[:, EXPERT_FF:] * cw
    out_ref[...] += jnp.dot(hid.astype(BF16), wd_ref[0], preferred_element_type=F32)

    if final_norm:
        @pl.when(e == N_EXPERTS - 1)
        def _():
            out_ref[...] = _rmsnorm(out_ref[...], gf_ref[...])


def _moe(xm, h2, comb, wgu, wd, g_final, final_norm):
    n = xm.shape[0]
    tm = MOE_ROW_TILE
    return pl.pallas_call(
        functools.partial(_moe_kernel, final_norm=final_norm),
        grid=(n // tm, N_EXPERTS),
        in_specs=[pl.BlockSpec((tm, D_MODEL), lambda i, e: (i, 0)),
                  pl.BlockSpec((tm, D_MODEL), lambda i, e: (i, 0)),
                  pl.BlockSpec((tm, N_EXPERTS), lambda i, e: (i, 0)),
                  pl.BlockSpec((1, D_MODEL, 2 * EXPERT_FF), lambda i, e: (e, 0, 0)),
                  pl.BlockSpec((1, EXPERT_FF, D_MODEL), lambda i, e: (e, 0, 0)),
                  _full((1, D_MODEL))],
        out_specs=pl.BlockSpec((tm, D_MODEL), lambda i, e: (i, 0)),
        out_shape=jax.ShapeDtypeStruct((n, D_MODEL), F32),
        compiler_params=_params("parallel", "arbitrary"),
        name="moe_ffn",
    )(xm, h2, comb, wgu, wd, g_final)


def _pack_lora(mu_wag, w1, a1, g1, mu_v, v1):
    d = D_MODEL
    if v1 is None:
        mu_v = jnp.zeros((d,), F32)
        v1 = jnp.zeros((d, V_LORA), F32)
    pad = jnp.zeros((d, LORA_HALF - (W_LORA + A_LORA + G_LORA + V_LORA)), F32)
    mus = (mu_wag[0], mu_wag[1], mu_wag[2], mu_v)
    ws = (w1, a1, g1, v1)
    cur = [w * (1.0 - m)[:, None] for w, m in zip(ws, mus)]
    prev = [w * m[:, None] for w, m in zip(ws, mus)]
    return jnp.concatenate(cur + [pad] + prev + [pad], axis=1)


def _pad_rows(w, lo, total=LANES):
    return jnp.zeros((total, w.shape[1]), F32).at[lo:lo + w.shape[0]].set(w)


def kernel(x, positions, norm_mix_g, w_in, gmlp_ln_g, gmlp_ln_b, gmlp_ws, gmlp_bs, rwkv_mu_rkv, rwkv_mu_wag, rwkv_w0, rwkv_w1, rwkv_w2, rwkv_a0, rwkv_a1, rwkv_a2, rwkv_g1, rwkv_g2, rwkv_k_k, rwkv_k_a, rwkv_r_k, rwkv_ln_w, rwkv_ln_b, rwkv_mu_v, rwkv_v0, rwkv_v1, rwkv_v2, w_branch_a, w_branch_b, w_branch_c, w_gate, b_gate, w_out, norm_ffn_g, router_group_w, router_group_b, router_expert_w, router_expert_b, expert_w_gate, expert_w_up, expert_w_down, final_norm_g):
    batch, seq, d = x.shape
    depth = w_in.shape[0]
    n = batch * seq
    x2 = x.reshape(n, d)
    rope = _rope_tables(positions)
    v_first = None
    for l in range(depth):
        vres = l > 0
        lora_w = _pack_lora(rwkv_mu_wag[l], rwkv_w1[l], rwkv_a1[l], rwkv_g1[l],
                            rwkv_mu_v[l - 1] if vres else None, rwkv_v1[l - 1] if vres else None)
        w_all = jnp.concatenate([w_in[l], lora_w], axis=1).astype(BF16)
        ya, qkv, rkv, lora = _inproj(
            x2, norm_mix_g[l][None], w_all, gmlp_ln_g[l][None], gmlp_ln_b[l][None], gmlp_ws[l],
            gmlp_bs[l].T, rope)

        attn = [_dilated_attention(qkv, batch, seq, g, dil)
                for g, (_, dil) in enumerate(DILATED_PATTERNS)]

        zero = jnp.zeros((C_WIDTH,), F32)
        vecs = jnp.stack([rwkv_w0[l], rwkv_a0[l], rwkv_v0[l - 1] if vres else zero, rwkv_k_k[l],
                          rwkv_k_a[l], rwkv_r_k[l].reshape(C_WIDTH), rwkv_ln_w[l], rwkv_ln_b[l]])
        yc, v_c = _rwkv(rkv, lora, v_first, rwkv_mu_rkv[l], vecs,
                        _pad_rows(rwkv_w2[l], 0), _pad_rows(rwkv_a2[l], W_LORA), rwkv_g2[l],
                        _pad_rows(rwkv_v2[l - 1], 0) if vres else None, batch, seq)
        if l == 0:
            v_first = v_c

        wr = jnp.concatenate([router_group_w[l], router_expert_w[l]], axis=1).T
        wr = jnp.zeros((32, d), F32).at[:wr.shape[0]].set(wr)
        br = jnp.concatenate([router_group_b[l], router_expert_b[l]])
        br = jnp.zeros((32, LANES), F32).at[:br.shape[0], 0].set(br)
        xm, h2, comb_t = _merge(
            x2, norm_mix_g[l][None], ya, attn, yc, w_gate[l].astype(BF16), b_gate[l][None],
            w_branch_a[l].astype(BF16), w_branch_b[l].astype(BF16), w_branch_c[l].astype(BF16),
            w_out[l].astype(BF16), norm_ffn_g[l][None], wr, br)

        wgu = jnp.concatenate([expert_w_gate[l], expert_w_up[l]], axis=2).astype(BF16)
        x2 = _moe(xm, h2, comb_t.T, wgu, expert_w_down[l].astype(BF16), final_norm_g[None],
                  final_norm=(l == depth - 1))
    return x2.reshape(batch, seq, d)
```

```python
import functools

import numpy as np
import jax
import jax.numpy as jnp
from jax import lax
from jax.experimental import pallas as pl
from jax.experimental.pallas import tpu as pltpu

F32 = jnp.float32
BF16 = jnp.bfloat16
HIGHEST = lax.Precision.HIGHEST

D_MODEL = 1024
HEAD_DIM = 64
A_GROUPS = 4
A_WIDTH = 512
CHUNK = 128
DILATED_PATTERNS = ((128, 1), (512, 4), (2048, 16))
B_HEADS_PER_GROUP = 4
B_WIDTH = 768
B_OUT = 256
BLOCK = 128
ROPE_THETA = 500000.0
ROPE_DIMS = 16
C_HEADS = 8
C_WIDTH = 512
W_LORA, A_LORA, V_LORA, G_LORA = 64, 64, 32, 128
RWKV_GN_EPS = HEAD_DIM * 1e-5
IN_COLS = 2 * A_WIDTH + 3 * B_WIDTH + 3 * C_WIDTH
LORA_HALF = 384
LORA_COLS = 2 * LORA_HALF
N_GROUPS = 4
EXPERTS_PER_GROUP = 4
N_EXPERTS = 16
EXPERT_FF = 512
RMS_EPS = 1e-6
LN_EPS = 1e-5

LANES = 128
ROW_TILE = 256
MERGE_ROW_TILE = 512
RWKV_CHUNK = 64
MOE_ROW_TILE = 1024
VMEM_LIMIT = 56 * 1024 * 1024
NEG_BIG = -1e30


def _params(*sem):
    return pltpu.CompilerParams(dimension_semantics=sem, vmem_limit_bytes=VMEM_LIMIT)


def _full(shape):
    nd = len(shape)
    return pl.BlockSpec(shape, lambda *_: (0,) * nd)


def _gelu_tanh(x):
    return 0.5 * x * (1.0 + jnp.tanh(0.7978845608028654 * (x + 0.044715 * (x * x * x))))


def _sigmoid(x):
    return 1.0 / (1.0 + jnp.exp(-x))


def _rmsnorm(x, g):
    return x * lax.rsqrt(jnp.mean(x * x, axis=-1, keepdims=True) + RMS_EPS) * g


def _rope_table_kernel(pos_ref, consts_ref, c_ref, sa_ref, sb_ref):
    ang = pos_ref[...].astype(F32) * consts_ref[0:1, :]
    c_ref[...] = jnp.cos(ang)
    s = jnp.sin(ang)
    sa_ref[...] = -s * consts_ref[1:2, :]
    sb_ref[...] = s * consts_ref[2:3, :]


def _rope_tables(positions):
    n = positions.size
    half = ROPE_DIMS // 2
    inv_freq = ROPE_THETA ** (-jnp.arange(half, dtype=F32) / half)
    lane = np.arange(LANES) % HEAD_DIM
    in_rope = lane < ROPE_DIMS
    invf_row = jnp.where(in_rope, inv_freq[lane % half], 0.0)
    consts = jnp.zeros((8, LANES), F32)
    consts = consts.at[0].set(invf_row)
    consts = consts.at[1].set(jnp.asarray(lane < half, F32))
    consts = consts.at[2].set(jnp.asarray((lane >= half) & in_rope, F32))
    tm = 1024
    out = jax.ShapeDtypeStruct((n, LANES), F32)
    return pl.pallas_call(
        _rope_table_kernel,
        grid=(n // tm,),
        in_specs=[pl.BlockSpec((tm, 1), lambda i: (i, 0)), _full((8, LANES))],
        out_specs=[pl.BlockSpec((tm, LANES), lambda i: (i, 0))] * 3,
        out_shape=[out, out, out],
        compiler_params=_params("parallel"),
        name="rope_tables",
    )(positions.reshape(n, 1), consts)


def _inproj_kernel(x_ref, g_ref, w_ref, lng_ref, lnb_ref, ws_ref, bst_ref, c_ref, sa_ref, sb_ref,
                   ya_ref, qkv0_ref, qkv1_ref, qkv2_ref, rkv_ref, lora_ref, nat_ref):
    tm = x_ref.shape[0]
    hb = _rmsnorm(x_ref[...], g_ref[...]).astype(BF16)

    def proj(lo, width):
        return jnp.dot(hb, w_ref[:, lo:lo + width], preferred_element_type=F32)

    uv = proj(0, 2 * A_WIDTH)
    u = _gelu_tanh(uv[:, :A_WIDTH])
    v = _gelu_tanh(uv[:, A_WIDTH:])
    mu = jnp.mean(v, axis=-1, keepdims=True)
    d = v - mu
    var = jnp.mean(d * d, axis=-1, keepdims=True)
    vn = (d * lax.rsqrt(var + LN_EPS) * lng_ref[...] + lnb_ref[...]).astype(BF16)
    row = lax.broadcasted_iota(jnp.int32, (CHUNK, CHUNK), 0)
    col = lax.broadcasted_iota(jnp.int32, (CHUNK, CHUNK), 1)
    for g in range(A_GROUPS):
        wg = jnp.where(row >= col, ws_ref[g], 0.0).astype(BF16)
        bias = bst_ref[:, g:g + 1]
        cs = slice(g * LANES, (g + 1) * LANES)
        for c in range(tm // CHUNK):
            rs = slice(c * CHUNK, (c + 1) * CHUNK)
            mixed = jnp.dot(wg, vn[rs, cs], preferred_element_type=F32) + bias
            ya_ref[rs, cs] = (u[rs, cs] * mixed).astype(BF16)

    cos, sa, sb = c_ref[...], sa_ref[...], sb_ref[...]
    q_off = 2 * A_WIDTH
    for grp, (qkv_ref, (_, dil)) in enumerate(zip((qkv0_ref, qkv1_ref, qkv2_ref), DILATED_PATTERNS)):
        acc = proj(q_off + grp * B_WIDTH, B_WIDTH)
        for c in range(B_WIDTH // LANES):
            t = acc[:, c * LANES:(c + 1) * LANES]
            if c < 2 * B_OUT // LANES:
                t = t * cos + pltpu.roll(t, LANES - 8, 1) * sa + pltpu.roll(t, 8, 1) * sb
            if c < B_OUT // LANES:
                t = t * (HEAD_DIM ** -0.5)
            cs = slice(c * LANES, (c + 1) * LANES)
            if dil == 1:
                qkv_ref[0, 0, :, cs] = t.astype(BF16)
            else:
                nat_ref[c] = t
                for r in range(dil):
                    qkv_ref[0, r, :, cs] = nat_ref[c, pl.ds(r, tm // dil, stride=dil), :].astype(BF16)

    rkv_ref[...] = proj(q_off + 3 * B_WIDTH, 3 * C_WIDTH)
    lora_ref[...] = proj(IN_COLS, LORA_COLS)


def _residue_spec(tm, dil, width, tiles_per_seq):
    return pl.BlockSpec((1, dil, tm // dil, width),
                        lambda i: (i // tiles_per_seq, 0, i % tiles_per_seq, 0))


def _inproj(x2, g, w_all, ln_g, ln_b, ws, bs_t, rope, batch, seq):
    n = x2.shape[0]
    tm = ROW_TILE
    wcols = w_all.shape[1]
    row = lambda width: pl.BlockSpec((tm, width), lambda i: (i, 0))
    dils = [d for _, d in DILATED_PATTERNS]
    return pl.pallas_call(
        _inproj_kernel,
        grid=(n // tm,),
        in_specs=[row(D_MODEL), _full((1, D_MODEL)), _full((D_MODEL, wcols)),
                  _full((1, A_WIDTH)), _full((1, A_WIDTH)), _full((A_GROUPS, CHUNK, CHUNK)),
                  _full((CHUNK, A_GROUPS)), row(LANES), row(LANES), row(LANES)],
        out_specs=[row(A_WIDTH)] + [_residue_spec(tm, d, B_WIDTH, seq // tm) for d in dils]
        + [row(3 * C_WIDTH), row(LORA_COLS)],
        out_shape=[jax.ShapeDtypeStruct((n, A_WIDTH), BF16)]
        + [jax.ShapeDtypeStruct((batch, d, seq // d, B_WIDTH), BF16) for d in dils]
        + [jax.ShapeDtypeStruct((n, 3 * C_WIDTH), F32),
           jax.ShapeDtypeStruct((n, LORA_COLS), F32)],
        scratch_shapes=[pltpu.VMEM((B_WIDTH // LANES, tm, LANES), F32)],
        compiler_params=_params("parallel"),
        name="inproj",
    )(x2, g, w_all, ln_g, ln_b, ws, bs_t, *rope)


def _attn_block(q, kw, vw, valid):
    lo = lax.broadcasted_iota(jnp.int32, (BLOCK, LANES), 1) < HEAD_DIM
    pairs = range(B_OUT // LANES)
    scores = []
    for p in pairs:
        ps = slice(p * LANES, (p + 1) * LANES)
        qp = q[:, ps]
        zero = jnp.zeros_like(qp)
        for part in (jnp.where(lo, qp, zero), jnp.where(lo, zero, qp)):
            s = lax.dot_general(part, kw[:, ps], NT, preferred_element_type=F32)
            scores.append(jnp.where(valid, s, NEG_BIG))
    m = [jnp.max(s, axis=-1, keepdims=True) for s in scores]
    e = [jnp.exp(s - mx) for s, mx in zip(scores, m)]
    l = [jnp.sum(t, axis=-1, keepdims=True) for t in e]
    pv = [jnp.dot(t.astype(BF16), vw[:, (h // 2) * LANES:(h // 2 + 1) * LANES],
                  preferred_element_type=F32) for h, t in enumerate(e)]
    out = [jnp.where(lo, pv[2 * p] / l[2 * p], pv[2 * p + 1] / l[2 * p + 1]) for p in pairs]
    lse = [mx + jnp.log(t) for mx, t in zip(m, l)]
    lse = [jnp.where(lo, lse[2 * p], lse[2 * p + 1]) for p in pairs]
    return jnp.concatenate(out, axis=-1), jnp.concatenate(lse, axis=-1)


def _attn_kernel(qkv_ref, o_ref, l_ref):
    dil, sub = qkv_ref.shape[1], qkv_ref.shape[2]
    nb = sub // BLOCK
    qs, ks, vs = (slice(i * B_OUT, (i + 1) * B_OUT) for i in range(3))
    qi = lax.broadcasted_iota(jnp.int32, (BLOCK, BLOCK), 0)
    kj = lax.broadcasted_iota(jnp.int32, (BLOCK, BLOCK), 1)
    qi2 = lax.broadcasted_iota(jnp.int32, (BLOCK, 2 * BLOCK), 0)
    kj2 = lax.broadcasted_iota(jnp.int32, (BLOCK, 2 * BLOCK), 1)
    valid2 = (kj2 >= qi2) & (kj2 <= qi2 + BLOCK)

    def residue(r):
        first = pl.ds(0, BLOCK)
        o, l = _attn_block(qkv_ref[0, r, first, qs], qkv_ref[0, r, first, ks],
                           qkv_ref[0, r, first, vs], kj <= qi)
        o_ref[0, r, first, :] = o
        l_ref[0, r, first, :] = l

        def body(n, carry):
            cur = pl.ds(pl.multiple_of(n * BLOCK, BLOCK), BLOCK)
            win = pl.ds(pl.multiple_of((n - 1) * BLOCK, BLOCK), 2 * BLOCK)
            o, l = _attn_block(qkv_ref[0, r, cur, qs], qkv_ref[0, r, win, ks],
                               qkv_ref[0, r, win, vs], valid2)
            o_ref[0, r, cur, :] = o
            l_ref[0, r, cur, :] = l
            return carry

        if nb > 1:
            lax.fori_loop(1, nb, body, 0)

    if dil <= 4:
        for r in range(dil):
            residue(r)
    else:
        def rbody(r, carry):
            residue(r)
            return carry
        lax.fori_loop(0, dil, rbody, 0)


def _dilated_attention(qkv, dilation):
    batch, _, sub, _ = qkv.shape
    ospec = pl.BlockSpec((1, dilation, sub, B_OUT), lambda b: (b, 0, 0, 0))
    oshape = jax.ShapeDtypeStruct((batch, dilation, sub, B_OUT), F32)
    return pl.pallas_call(
        _attn_kernel,
        grid=(batch,),
        in_specs=[pl.BlockSpec((1, dilation, sub, B_WIDTH), lambda b: (b, 0, 0, 0))],
        out_specs=[ospec, ospec],
        out_shape=[oshape, oshape],
        compiler_params=_params("parallel"),
        name=f"dilated_attn_d{dilation}",
    )(qkv)


def _shift_rows(x, prev_row):
    rolled = pltpu.roll(x, 1, 0)
    first = lax.broadcasted_iota(jnp.int32, x.shape, 0) == 0
    return jnp.where(first, prev_row, rolled)


NN = (((1,), (0,)), ((), ()))
NT = (((1,), (1,)), ((), ()))
TN = (((0,), (0,)), ((), ()))


def _dot(a, b, dims=NN):
    return lax.dot_general(a.astype(BF16), b.astype(BF16), dims, preferred_element_type=F32)


def _split(a, pieces):
    out = []
    for _ in range(pieces):
        p = a.astype(BF16)
        out.append(p)
        a = a - p.astype(F32)
    return out


def _dot3(a, b, dims=NN):
    (ah, al), (bh, bl) = _split(a, 2), _split(b, 2)
    return _dot(ah, bh, dims) + _dot(ah, bl, dims) + _dot(al, bh, dims)


def _head_sum(x):
    lo = lax.broadcasted_iota(jnp.int32, (x.shape[0], LANES), 1) < HEAD_DIM
    outs = []
    for c in range(x.shape[1] // LANES):
        xc = x[:, c * LANES:(c + 1) * LANES]
        s_lo = jnp.sum(jnp.where(lo, xc, 0.0), axis=-1, keepdims=True)
        s_hi = jnp.sum(jnp.where(lo, 0.0, xc), axis=-1, keepdims=True)
        outs.append(jnp.where(lo, s_lo, s_hi))
    return jnp.concatenate(outs, axis=-1)


def _rwkv_kernel(*refs, has_vres):
    if has_vres:
        (rkv_ref, lora_ref, vf_ref, mu_ref, vec_ref, w2_ref, a2_ref, g2_ref, v2_ref,
         yc_ref, vout_ref, state_ref, prkv_ref, plora_ref, y_ref) = refs
    else:
        (rkv_ref, lora_ref, mu_ref, vec_ref, w2_ref, a2_ref, g2_ref,
         yc_ref, vout_ref, state_ref, prkv_ref, plora_ref, y_ref) = refs
    ch = rkv_ref.shape[0]
    hd = HEAD_DIM

    @pl.when(pl.program_id(1) == 0)
    def _():
        state_ref[...] = jnp.zeros_like(state_ref)
        prkv_ref[...] = jnp.zeros_like(prkv_ref)
        plora_ref[...] = jnp.zeros_like(plora_ref)

    rkv = rkv_ref[...]
    rkv_prev = _shift_rows(rkv, prkv_ref[0:1, :])
    prkv_ref[0:1, :] = rkv[ch - 1:ch, :]
    lora2 = lora_ref[:, LORA_HALF:]
    lora = lora_ref[:, :LORA_HALF] + _shift_rows(lora2, plora_ref[0:1, :])
    plora_ref[0:1, :] = lora2[ch - 1:ch, :]

    w0, a0, v0 = vec_ref[0:1, :], vec_ref[1:2, :], vec_ref[2:3, :]
    k_k, k_a, r_k = vec_ref[3:4, :], vec_ref[4:5, :], vec_ref[5:6, :]
    ln_w, ln_b = vec_ref[6:7, :], vec_ref[7:8, :]

    def mix(i):
        cur = rkv[:, i * C_WIDTH:(i + 1) * C_WIDTH]
        prev = rkv_prev[:, i * C_WIDTH:(i + 1) * C_WIDTH]
        return cur + (prev - cur) * mu_ref[i:i + 1, :]

    r, k, v = mix(0), mix(1), mix(2)
    wa = lora[:, 0:LANES]
    z = w0 + _dot3(jnp.tanh(wa), w2_ref[...])
    w_log = -(jnp.maximum(-z, 0.0) + jnp.log(1.0 + jnp.exp(-jnp.abs(z)))) - 0.5
    log_decay = -jnp.exp(w_log)
    a = _sigmoid(a0 + _dot(wa, a2_ref[...]))
    gate = _dot(_sigmoid(lora[:, LANES:2 * LANES]), g2_ref[...])
    if has_vres:
        vmix = _sigmoid(v0 + _dot(lora[:, 2 * LANES:3 * LANES], v2_ref[...]))
        v = v + (vf_ref[...] - v) * vmix
    vout_ref[...] = v

    kk = k * k_k
    kk = kk / jnp.maximum(jnp.sqrt(_head_sum(kk * kk)), 1e-12)
    k = k * (1.0 + (a - 1.0) * k_a)
    bonus = _head_sum(r * k * r_k) * v
    b = kk * a

    ti = lax.broadcasted_iota(jnp.int32, (ch, ch), 0)
    si = lax.broadcasted_iota(jnp.int32, (ch, ch), 1)
    tri = (si <= ti).astype(BF16)
    cum = functools.reduce(lambda s, t: s + t, [_dot(tri, p) for p in _split(log_decay, 3)])
    cum_end = cum[ch - 1:ch, :]
    e_neg = jnp.exp(-cum)
    e_end = jnp.exp(cum_end - cum)
    kt = kk * jnp.exp(cum - log_decay)
    rt = r * jnp.exp(cum)
    bt = b * e_neg
    kq = k * e_neg
    kh = k * e_end
    bh = b * e_end
    unit = (lax.broadcasted_iota(jnp.int32, (8, LANES), 0) == 0).astype(F32)
    w_end_t = lax.dot_general(jnp.broadcast_to(jnp.exp(cum_end), (8, C_WIDTH)), unit, TN,
                              precision=HIGHEST, preferred_element_type=F32)

    lane = lax.broadcasted_iota(jnp.int32, (ch, LANES), 1)
    lo = lane < hd
    lo2 = lax.broadcasted_iota(jnp.int32, (2 * ch, LANES), 1) < hd
    row4 = lax.broadcasted_iota(jnp.int32, (4 * ch, LANES), 0)
    col4 = lax.broadcasted_iota(jnp.int32, (4 * ch, LANES), 1) % ch
    t4 = row4 % ch
    tri_mask = (col4 < t4) | ((col4 == t4) & ((row4 // ch) % 2 == 1))
    eye_right = (lane - hd == lax.broadcasted_iota(jnp.int32, (ch, LANES), 0)).astype(F32)
    brow = lax.broadcasted_iota(jnp.int32, (LANES, LANES), 0) // hd
    bcol = lax.broadcasted_iota(jnp.int32, (LANES, LANES), 1) // hd
    block_diag = brow == bcol
    zeros = jnp.zeros((ch, LANES), F32)

    pairs = range(C_HEADS // 2)
    lanes_of = [slice(p * LANES, (p + 1) * LANES) for p in pairs]
    s_bd = [state_ref[p] for p in pairs]
    l2 = [jnp.concatenate([kt[:, ps], rt[:, ps]], axis=0) for ps in lanes_of]
    ss = [_dot(l2[p], s_bd[p]) for p in pairs]
    g = [jnp.where(tri_mask,
                   _dot(jnp.concatenate([jnp.where(lo2, l2[p], 0.0), jnp.where(lo2, 0.0, l2[p])],
                                        axis=0),
                        jnp.concatenate([bt[:, ps], kq[:, ps]], axis=0), NT), 0.0)
         for p, ps in zip(pairs, lanes_of)]
    xv = [_dot(jnp.concatenate([g[p][0:ch], g[p][2 * ch:3 * ch]], axis=0),
               jnp.concatenate([zeros, v[:, ps]], axis=0))
          for p, ps in zip(pairs, lanes_of)]
    heads = [(p, i) for p in pairs for i in range(2)]
    n0 = [-g[p][2 * i * ch:(2 * i + 1) * ch] for p, i in heads]
    w = [_dot(n[:, :ch], jnp.where(lo, n, eye_right)) + jnp.where(lo, 0.0, eye_right) for n in n0]
    for _ in range(int(np.log2(ch)) - 1):
        w = [_dot(t[:, :ch], t) + jnp.where(lo, 0.0, t) for t in w]
    us = [_dot(w[2 * p + i],
               jnp.concatenate([zeros, ss[p][0:ch] + xv[p][i * ch:(i + 1) * ch]], axis=0))
          for p, i in heads]
    u = [jnp.where(lo, us[2 * p], us[2 * p + 1]) for p in pairs]
    yv = [_dot(jnp.concatenate([g[p][ch:2 * ch], g[p][3 * ch:4 * ch]], axis=0),
               jnp.concatenate([-u[p], v[:, ps]], axis=0))
          for p, ps in zip(pairs, lanes_of)]
    upd = [_dot(jnp.concatenate([kh[:, ps], bh[:, ps]], axis=0),
                jnp.concatenate([v[:, ps], -u[p]], axis=0), TN)
           for p, ps in zip(pairs, lanes_of)]
    for p, ps in zip(pairs, lanes_of):
        y_ref[:, ps] = ss[p][ch:2 * ch] + jnp.where(lo, yv[p][0:ch], yv[p][ch:2 * ch])
        state_ref[p] = w_end_t[ps, :] * s_bd[p] + jnp.where(block_diag, upd[p], 0.0)

    y = y_ref[...]
    d = y - _head_sum(y) * (1.0 / hd)
    var = _head_sum(d * d) * (1.0 / hd)
    yn = d * lax.rsqrt(var + RWKV_GN_EPS) * ln_w + ln_b
    yc_ref[...] = ((yn + bonus) * gate).astype(BF16)


def _rwkv(rkv, lora, v_first, mu_rkv, vecs, w2, a2, g2, v2, batch, seq):
    n = rkv.shape[0]
    ch = RWKV_CHUNK
    nc = seq // ch
    has_vres = v_first is not None
    row = lambda width: pl.BlockSpec((ch, width), lambda b, c: (b * nc + c, 0))
    ins = [rkv, lora] + ([v_first] if has_vres else []) + [mu_rkv, vecs, w2, a2, g2] \
        + ([v2] if has_vres else [])
    specs = [row(3 * C_WIDTH), row(LORA_COLS)] + ([row(C_WIDTH)] if has_vres else []) \
        + [_full((3, C_WIDTH)), _full((8, C_WIDTH)), _full((LANES, C_WIDTH)),
           _full((LANES, C_WIDTH)), _full((LANES, C_WIDTH))] \
        + ([_full((LANES, C_WIDTH))] if has_vres else [])
    return pl.pallas_call(
        functools.partial(_rwkv_kernel, has_vres=has_vres),
        grid=(batch, nc),
        in_specs=specs,
        out_specs=[row(C_WIDTH), row(C_WIDTH)],
        out_shape=[jax.ShapeDtypeStruct((n, C_WIDTH), BF16),
                   jax.ShapeDtypeStruct((n, C_WIDTH), F32)],
        scratch_shapes=[pltpu.VMEM((C_HEADS // 2, LANES, LANES), F32),
                        pltpu.VMEM((8, 3 * C_WIDTH), F32),
                        pltpu.VMEM((8, LORA_HALF), F32),
                        pltpu.VMEM((ch, C_WIDTH), F32)],
        compiler_params=_params("parallel", "arbitrary"),
        name="rwkv7_chunked",
    )(*ins)


def _merge_kernel(x_ref, g_ref, ya_ref, o0_ref, l0_ref, o1_ref, l1_ref, o2_ref, l2_ref, yc_ref,
                  wg_ref, bg_ref, wa_ref, wb_ref, wc_ref, wo_ref, g2_ref, wr_ref, br_ref,
                  xm_ref, h2_ref, comb_ref, nat_ref):
    x = x_ref[...]
    tm = x.shape[0]
    hb = _rmsnorm(x, g_ref[...]).astype(BF16)

    def token_order(ref, slot):
        dil = ref.shape[1]
        if dil == 1:
            return ref[0, 0]
        chunks = []
        for c in range(B_OUT // LANES):
            for r in range(dil):
                nat_ref[slot, c, pl.ds(r, tm // dil, stride=dil), :] = \
                    ref[0, r, :, c * LANES:(c + 1) * LANES]
            chunks.append(nat_ref[slot, c])
        return jnp.concatenate(chunks, axis=-1)

    l0, l1, l2 = token_order(l0_ref, 0), token_order(l1_ref, 0), token_order(l2_ref, 1)
    lm = jnp.maximum(jnp.maximum(l0, l1), l2)
    e0, e1, e2 = jnp.exp(l0 - lm), jnp.exp(l1 - lm), jnp.exp(l2 - lm)
    o0, o1, o2 = token_order(o0_ref, 0), token_order(o1_ref, 2), token_order(o2_ref, 3)
    yb = ((e0 * o0 + e1 * o1 + e2 * o2) / (e0 + e1 + e2)).astype(BF16)

    merged = None
    for j, (y, w_ref) in enumerate(((ya_ref[...], wa_ref), (yb, wb_ref), (yc_ref[...], wc_ref))):
        cs = slice(j * D_MODEL, (j + 1) * D_MODEL)
        gate = _sigmoid(jnp.dot(hb, wg_ref[:, cs], preferred_element_type=F32) + bg_ref[:, cs])
        term = gate * jnp.dot(y, w_ref[...], preferred_element_type=F32)
        merged = term if merged is None else merged + term
    xm = x + jnp.dot(merged.astype(BF16), wo_ref[...], preferred_element_type=F32)
    xm_ref[...] = xm

    h2 = _rmsnorm(xm, g2_ref[...])
    h2_ref[...] = h2.astype(BF16)

    logits = _dot3(wr_ref[...], h2, NT) + br_ref[:, 0:1]
    gl = [logits[g:g + 1, :] for g in range(N_GROUPS)]
    gmax = functools.reduce(jnp.maximum, gl)
    gsum = functools.reduce(lambda s, t: s + t, [jnp.exp(t - gmax) for t in gl])
    gp = 1.0 / gsum
    taken = jnp.zeros_like(gmax, dtype=jnp.bool_)
    sel = []
    for g in range(N_GROUPS):
        pick = (gl[g] == gmax) & jnp.logical_not(taken)
        sel.append(pick)
        taken = taken | pick
    el = []
    for e in range(EXPERTS_PER_GROUP):
        acc = jnp.zeros_like(gmax)
        for g in range(N_GROUPS):
            r0 = N_GROUPS + g * EXPERTS_PER_GROUP + e
            acc = jnp.where(sel[g], logits[r0:r0 + 1, :], acc)
        el.append(acc)
    emax = functools.reduce(jnp.maximum, el)
    ex = [jnp.exp(t - emax) for t in el]
    esum = functools.reduce(lambda s, t: s + t, ex)
    p = [t / esum for t in ex]
    p1 = functools.reduce(jnp.maximum, p)
    taken = jnp.zeros_like(gmax, dtype=jnp.bool_)
    first = []
    for e in range(EXPERTS_PER_GROUP):
        pick = (p[e] == p1) & jnp.logical_not(taken)
        first.append(pick)
        taken = taken | pick
    rest = [jnp.where(first[e], -1.0, p[e]) for e in range(EXPERTS_PER_GROUP)]
    p2 = functools.reduce(jnp.maximum, rest)
    taken = jnp.zeros_like(gmax, dtype=jnp.bool_)
    second = []
    for e in range(EXPERTS_PER_GROUP):
        pick = (rest[e] == p2) & jnp.logical_not(taken)
        second.append(pick)
        taken = taken | pick
    denom = p1 + p2
    for e in range(EXPERTS_PER_GROUP):
        in_group = jnp.where(first[e], p1 / denom, jnp.where(second[e], p2 / denom, 0.0))
        for g in range(N_GROUPS):
            r0 = g * EXPERTS_PER_GROUP + e
            comb_ref[r0:r0 + 1, :] = jnp.where(sel[g], gp * in_group, 0.0)


def _merge(x2, g, ya, attn, yc, wg, bg, wa, wb, wc, wo, g2, wr, br, seq):
    n = x2.shape[0]
    tm = MERGE_ROW_TILE
    row = lambda width: pl.BlockSpec((tm, width), lambda i: (i, 0))
    attn_flat = [t for pair in attn for t in pair]
    attn_specs = [_residue_spec(tm, t.shape[1], B_OUT, seq // tm) for t in attn_flat]
    return pl.pallas_call(
        _merge_kernel,
        grid=(n // tm,),
        in_specs=[row(D_MODEL), _full((1, D_MODEL)), row(A_WIDTH)] + attn_specs
        + [row(C_WIDTH), _full((D_MODEL, 3 * D_MODEL)), _full((1, 3 * D_MODEL)),
           _full((A_WIDTH, D_MODEL)), _full((B_OUT, D_MODEL)), _full((C_WIDTH, D_MODEL)),
           _full((D_MODEL, D_MODEL)), _full((1, D_MODEL)), _full((32, D_MODEL)), _full((32, LANES))],
        out_specs=[row(D_MODEL), row(D_MODEL), pl.BlockSpec((N_EXPERTS, tm), lambda i: (0, i))],
        out_shape=[jax.ShapeDtypeStruct((n, D_MODEL), F32),
                   jax.ShapeDtypeStruct((n, D_MODEL), BF16),
                   jax.ShapeDtypeStruct((N_EXPERTS, n), F32)],
        scratch_shapes=[pltpu.VMEM((4, B_OUT // LANES, tm, LANES), F32)],
        compiler_params=_params("parallel"),
        name="merge_router",
    )(x2, g, ya, *attn_flat, yc, wg, bg, wa, wb, wc, wo, g2, wr, br)


def _moe_kernel(xm_ref, h2_ref, comb_ref, wg_ref, wu_ref, wd_ref, gf_ref, out_ref, *, final_norm):
    e = pl.program_id(1)

    @pl.when(e == 0)
    def _():
        out_ref[...] = xm_ref[...]

    h2 = h2_ref[...]
    gate = jnp.dot(h2, wg_ref[0], preferred_element_type=F32)
    up = jnp.dot(h2, wu_ref[0], preferred_element_type=F32)
    lane = lax.broadcasted_iota(jnp.int32, comb_ref.shape, 1)
    cw = jnp.sum(jnp.where(lane == e, comb_ref[...], 0.0), axis=-1, keepdims=True)
    hid = gate * _sigmoid(gate) * up * cw
    out_ref[...] += jnp.dot(hid.astype(BF16), wd_ref[0], preferred_element_type=F32)

    if final_norm:
        @pl.when(e == N_EXPERTS - 1)
        def _():
            out_ref[...] = _rmsnorm(out_ref[...], gf_ref[...])


def _moe(xm, h2, comb, wg, wu, wd, g_final, final_norm):
    n = xm.shape[0]
    tm = MOE_ROW_TILE
    return pl.pallas_call(
        functools.partial(_moe_kernel, final_norm=final_norm),
        grid=(n // tm, N_EXPERTS),
        in_specs=[pl.BlockSpec((tm, D_MODEL), lambda i, e: (i, 0)),
                  pl.BlockSpec((tm, D_MODEL), lambda i, e: (i, 0)),
                  pl.BlockSpec((tm, N_EXPERTS), lambda i, e: (i, 0)),
                  pl.BlockSpec((1, D_MODEL, EXPERT_FF), lambda i, e: (e, 0, 0)),
                  pl.BlockSpec((1, D_MODEL, EXPERT_FF), lambda i, e: (e, 0, 0)),
                  pl.BlockSpec((1, EXPERT_FF, D_MODEL), lambda i, e: (e, 0, 0)),
                  _full((1, D_MODEL))],
        out_specs=pl.BlockSpec((tm, D_MODEL), lambda i, e: (i, 0)),
        out_shape=jax.ShapeDtypeStruct((n, D_MODEL), F32),
        compiler_params=_params("parallel", "arbitrary"),
        name="moe_ffn",
    )(xm, h2, comb, wg, wu, wd, g_final)


def _pack_lora(mu_wag, w1, a1, g1, mu_v, v1):
    d = D_MODEL
    if v1 is None:
        mu_v = jnp.zeros((d,), F32)
        v1 = jnp.zeros((d, V_LORA), F32)
    pad = jnp.zeros((d, LORA_HALF - (W_LORA + A_LORA + G_LORA + V_LORA)), F32)
    mus = (mu_wag[0], mu_wag[1], mu_wag[2], mu_v)
    ws = (w1, a1, g1, v1)
    cur = [w * (1.0 - m)[:, None] for w, m in zip(ws, mus)]
    prev = [w * m[:, None] for w, m in zip(ws, mus)]
    return jnp.concatenate(cur + [pad] + prev + [pad], axis=1)


def _in_col_order():
    b0 = 2 * A_WIDTH
    cols = list(range(b0))
    for g in range(len(DILATED_PATTERNS)):
        for part in range(3):
            lo = b0 + part * B_WIDTH + g * B_OUT
            cols += list(range(lo, lo + B_OUT))
    return np.asarray(cols + list(range(b0 + 3 * B_WIDTH, IN_COLS)), np.int32)


_IN_COL_ORDER = _in_col_order()


def _pad_rows(w, lo, total=LANES):
    return jnp.zeros((total, w.shape[1]), F32).at[lo:lo + w.shape[0]].set(w)


def kernel(x, positions, norm_mix_g, w_in, gmlp_ln_g, gmlp_ln_b, gmlp_ws, gmlp_bs, rwkv_mu_rkv, rwkv_mu_wag, rwkv_w0, rwkv_w1, rwkv_w2, rwkv_a0, rwkv_a1, rwkv_a2, rwkv_g1, rwkv_g2, rwkv_k_k, rwkv_k_a, rwkv_r_k, rwkv_ln_w, rwkv_ln_b, rwkv_mu_v, rwkv_v0, rwkv_v1, rwkv_v2, w_branch_a, w_branch_b, w_branch_c, w_gate, b_gate, w_out, norm_ffn_g, router_group_w, router_group_b, router_expert_w, router_expert_b, expert_w_gate, expert_w_up, expert_w_down, final_norm_g):
    batch, seq, d = x.shape
    depth = w_in.shape[0]
    n = batch * seq
    x2 = x.reshape(n, d)
    rope = _rope_tables(positions)
    v_first = None
    for l in range(depth):
        vres = l > 0
        lora_w = _pack_lora(rwkv_mu_wag[l], rwkv_w1[l], rwkv_a1[l], rwkv_g1[l],
                            rwkv_mu_v[l - 1] if vres else None, rwkv_v1[l - 1] if vres else None)
        w_all = jnp.concatenate([w_in[l][:, _IN_COL_ORDER], lora_w], axis=1).astype(BF16)
        ya, qkv0, qkv1, qkv2, rkv, lora = _inproj(
            x2, norm_mix_g[l][None], w_all, gmlp_ln_g[l][None], gmlp_ln_b[l][None], gmlp_ws[l],
            gmlp_bs[l].T, rope, batch, seq)

        attn = [_dilated_attention(qkv, dil)
                for qkv, (_, dil) in zip((qkv0, qkv1, qkv2), DILATED_PATTERNS)]

        zero = jnp.zeros((C_WIDTH,), F32)
        vecs = jnp.stack([rwkv_w0[l], rwkv_a0[l], rwkv_v0[l - 1] if vres else zero, rwkv_k_k[l],
                          rwkv_k_a[l], rwkv_r_k[l].reshape(C_WIDTH), rwkv_ln_w[l], rwkv_ln_b[l]])
        yc, v_c = _rwkv(rkv, lora, v_first, rwkv_mu_rkv[l], vecs,
                        _pad_rows(rwkv_w2[l], 0), _pad_rows(rwkv_a2[l], W_LORA), rwkv_g2[l],
                        _pad_rows(rwkv_v2[l - 1], 0) if vres else None, batch, seq)
        if l == 0:
            v_first = v_c

        wr = jnp.concatenate([router_group_w[l], router_expert_w[l]], axis=1).T
        wr = jnp.zeros((32, d), F32).at[:wr.shape[0]].set(wr)
        br = jnp.concatenate([router_group_b[l], router_expert_b[l]])
        br = jnp.zeros((32, LANES), F32).at[:br.shape[0], 0].set(br)
        xm, h2, comb_t = _merge(
            x2, norm_mix_g[l][None], ya, attn, yc, w_gate[l].astype(BF16), b_gate[l][None],
            w_branch_a[l].astype(BF16), w_branch_b[l].astype(BF16), w_branch_c[l].astype(BF16),
            w_out[l].astype(BF16), norm_ffn_g[l][None], wr, br, seq)

        x2 = _moe(xm, h2, comb_t.T, expert_w_gate[l].astype(BF16), expert_w_up[l].astype(BF16),
                  expert_w_down[l].astype(BF16), final_norm_g[None], final_norm=(l == depth - 1))
    return x2.reshape(batch, seq, d)
```

```python
import functools

import numpy as np
import jax
import jax.numpy as jnp
from jax import lax
from jax.experimental import pallas as pl
from jax.experimental.pallas import tpu as pltpu

F32 = jnp.float32
BF16 = jnp.bfloat16
HIGHEST = lax.Precision.HIGHEST

D_MODEL = 1024
HEAD_DIM = 64
A_GROUPS = 4
A_WIDTH = 512
CHUNK = 128
DILATED_PATTERNS = ((128, 1), (512, 4), (2048, 16))
B_HEADS_PER_GROUP = 4
B_WIDTH = 768
B_OUT = 256
BLOCK = 128
ROPE_THETA = 500000.0
ROPE_DIMS = 16
C_HEADS = 8
C_WIDTH = 512
W_LORA, A_LORA, V_LORA, G_LORA = 64, 64, 32, 128
RWKV_GN_EPS = HEAD_DIM * 1e-5
IN_COLS = 2 * A_WIDTH + 3 * B_WIDTH + 3 * C_WIDTH
LORA_HALF = 384
LORA_COLS = 2 * LORA_HALF
N_GROUPS = 4
EXPERTS_PER_GROUP = 4
N_EXPERTS = 16
EXPERT_FF = 512
RMS_EPS = 1e-6
LN_EPS = 1e-5

LANES = 128
ROW_TILE = 512
MERGE_ROW_TILE = 512
RWKV_CHUNK = 64
RWKV_BATCH_ROWS = 2
MOE_ROW_TILE = 1024
VMEM_LIMIT = 56 * 1024 * 1024
NEG_BIG = -1e30


def _params(*sem):
    return pltpu.CompilerParams(dimension_semantics=sem, vmem_limit_bytes=VMEM_LIMIT)


def _full(shape):
    nd = len(shape)
    return pl.BlockSpec(shape, lambda *_: (0,) * nd)


def _gelu_tanh(x):
    return 0.5 * x * (1.0 + jnp.tanh(0.7978845608028654 * (x + 0.044715 * (x * x * x))))


def _sigmoid(x):
    return 1.0 / (1.0 + jnp.exp(-x))


def _rmsnorm(x, g):
    return x * lax.rsqrt(jnp.mean(x * x, axis=-1, keepdims=True) + RMS_EPS) * g


def _rope_table_kernel(pos_ref, consts_ref, c_ref, sa_ref, sb_ref):
    ang = pos_ref[...].astype(F32) * consts_ref[0:1, :]
    c_ref[...] = jnp.cos(ang)
    s = jnp.sin(ang)
    sa_ref[...] = -s * consts_ref[1:2, :]
    sb_ref[...] = s * consts_ref[2:3, :]


def _rope_tables(positions):
    n = positions.size
    half = ROPE_DIMS // 2
    inv_freq = ROPE_THETA ** (-jnp.arange(half, dtype=F32) / half)
    lane = np.arange(LANES) % HEAD_DIM
    in_rope = lane < ROPE_DIMS
    invf_row = jnp.where(in_rope, inv_freq[lane % half], 0.0)
    consts = jnp.zeros((8, LANES), F32)
    consts = consts.at[0].set(invf_row)
    consts = consts.at[1].set(jnp.asarray(lane < half, F32))
    consts = consts.at[2].set(jnp.asarray((lane >= half) & in_rope, F32))
    tm = 1024
    out = jax.ShapeDtypeStruct((n, LANES), F32)
    return pl.pallas_call(
        _rope_table_kernel,
        grid=(n // tm,),
        in_specs=[pl.BlockSpec((tm, 1), lambda i: (i, 0)), _full((8, LANES))],
        out_specs=[pl.BlockSpec((tm, LANES), lambda i: (i, 0))] * 3,
        out_shape=[out, out, out],
        compiler_params=_params("parallel"),
        name="rope_tables",
    )(positions.reshape(n, 1), consts)


def _inproj_kernel(x_ref, g_ref, w_ref, lng_ref, lnb_ref, ws_ref, bst_ref, c_ref, sa_ref, sb_ref,
                   ya_ref, qkv0_ref, qkv1_ref, qkv2_ref, rkv_ref, lora_ref, nat_ref):
    tm = x_ref.shape[0]
    hb = _rmsnorm(x_ref[...], g_ref[...]).astype(BF16)

    def proj(lo, width):
        return jnp.dot(hb, w_ref[:, lo:lo + width], preferred_element_type=F32)

    uv = proj(0, 2 * A_WIDTH)
    u = _gelu_tanh(uv[:, :A_WIDTH])
    v = _gelu_tanh(uv[:, A_WIDTH:])
    mu = jnp.mean(v, axis=-1, keepdims=True)
    d = v - mu
    var = jnp.mean(d * d, axis=-1, keepdims=True)
    vn = (d * lax.rsqrt(var + LN_EPS) * lng_ref[...] + lnb_ref[...]).astype(BF16)
    row = lax.broadcasted_iota(jnp.int32, (CHUNK, CHUNK), 0)
    col = lax.broadcasted_iota(jnp.int32, (CHUNK, CHUNK), 1)
    for g in range(A_GROUPS):
        wg = jnp.where(row >= col, ws_ref[g], 0.0).astype(BF16)
        bias = bst_ref[:, g:g + 1]
        cs = slice(g * LANES, (g + 1) * LANES)
        for c in range(tm // CHUNK):
            rs = slice(c * CHUNK, (c + 1) * CHUNK)
            mixed = jnp.dot(wg, vn[rs, cs], preferred_element_type=F32) + bias
            ya_ref[rs, cs] = (u[rs, cs] * mixed).astype(BF16)

    cos, sa, sb = c_ref[...], sa_ref[...], sb_ref[...]
    q_off = 2 * A_WIDTH
    for grp, (qkv_ref, (_, dil)) in enumerate(zip((qkv0_ref, qkv1_ref, qkv2_ref), DILATED_PATTERNS)):
        acc = proj(q_off + grp * B_WIDTH, B_WIDTH)
        for c in range(B_WIDTH // LANES):
            t = acc[:, c * LANES:(c + 1) * LANES]
            if c < 2 * B_OUT // LANES:
                t = t * cos + pltpu.roll(t, LANES - 8, 1) * sa + pltpu.roll(t, 8, 1) * sb
            if c < B_OUT // LANES:
                t = t * (HEAD_DIM ** -0.5)
            cs = slice(c * LANES, (c + 1) * LANES)
            if dil == 1:
                qkv_ref[0, 0, :, cs] = t.astype(BF16)
            else:
                nat_ref[c] = t
                for r in range(dil):
                    qkv_ref[0, r, :, cs] = nat_ref[c, pl.ds(r, tm // dil, stride=dil), :].astype(BF16)

    rkv_ref[...] = proj(q_off + 3 * B_WIDTH, 3 * C_WIDTH)
    lora_ref[...] = proj(IN_COLS, LORA_COLS)


def _residue_spec(tm, dil, width, tiles_per_seq):
    return pl.BlockSpec((1, dil, tm // dil, width),
                        lambda i: (i // tiles_per_seq, 0, i % tiles_per_seq, 0))


def _inproj(x2, g, w_all, ln_g, ln_b, ws, bs_t, rope, batch, seq):
    n = x2.shape[0]
    tm = ROW_TILE
    wcols = w_all.shape[1]
    row = lambda width: pl.BlockSpec((tm, width), lambda i: (i, 0))
    dils = [d for _, d in DILATED_PATTERNS]
    return pl.pallas_call(
        _inproj_kernel,
        grid=(n // tm,),
        in_specs=[row(D_MODEL), _full((1, D_MODEL)), _full((D_MODEL, wcols)),
                  _full((1, A_WIDTH)), _full((1, A_WIDTH)), _full((A_GROUPS, CHUNK, CHUNK)),
                  _full((CHUNK, A_GROUPS)), row(LANES), row(LANES), row(LANES)],
        out_specs=[row(A_WIDTH)] + [_residue_spec(tm, d, B_WIDTH, seq // tm) for d in dils]
        + [row(3 * C_WIDTH), row(LORA_COLS)],
        out_shape=[jax.ShapeDtypeStruct((n, A_WIDTH), BF16)]
        + [jax.ShapeDtypeStruct((batch, d, seq // d, B_WIDTH), BF16) for d in dils]
        + [jax.ShapeDtypeStruct((n, 3 * C_WIDTH), F32),
           jax.ShapeDtypeStruct((n, LORA_COLS), F32)],
        scratch_shapes=[pltpu.VMEM((B_WIDTH // LANES, tm, LANES), F32)],
        compiler_params=_params("parallel"),
        name="inproj",
    )(x2, g, w_all, ln_g, ln_b, ws, bs_t, *rope)


def _attn_block(q, kw, vw, valid):
    lo = lax.broadcasted_iota(jnp.int32, (BLOCK, LANES), 1) < HEAD_DIM
    pairs = range(B_OUT // LANES)
    scores = []
    for p in pairs:
        ps = slice(p * LANES, (p + 1) * LANES)
        qp = q[:, ps]
        zero = jnp.zeros_like(qp)
        for part in (jnp.where(lo, qp, zero), jnp.where(lo, zero, qp)):
            s = lax.dot_general(part, kw[:, ps], NT, preferred_element_type=F32)
            scores.append(jnp.where(valid, s, NEG_BIG))
    m = [jnp.max(s, axis=-1, keepdims=True) for s in scores]
    e = [jnp.exp(s - mx) for s, mx in zip(scores, m)]
    l = [jnp.sum(t, axis=-1, keepdims=True) for t in e]
    pv = [jnp.dot(t.astype(BF16), vw[:, (h // 2) * LANES:(h // 2 + 1) * LANES],
                  preferred_element_type=F32) for h, t in enumerate(e)]
    out = [jnp.where(lo, pv[2 * p] / l[2 * p], pv[2 * p + 1] / l[2 * p + 1]) for p in pairs]
    lse = [mx + jnp.log(t) for mx, t in zip(m, l)]
    lse = [jnp.where(lo, lse[2 * p], lse[2 * p + 1]) for p in pairs]
    return jnp.concatenate(out, axis=-1), jnp.concatenate(lse, axis=-1)


def _attn_kernel(qkv_ref, o_ref, l_ref):
    dil, sub = qkv_ref.shape[1], qkv_ref.shape[2]
    nb = sub // BLOCK
    qs, ks, vs = (slice(i * B_OUT, (i + 1) * B_OUT) for i in range(3))
    qi = lax.broadcasted_iota(jnp.int32, (BLOCK, BLOCK), 0)
    kj = lax.broadcasted_iota(jnp.int32, (BLOCK, BLOCK), 1)
    qi2 = lax.broadcasted_iota(jnp.int32, (BLOCK, 2 * BLOCK), 0)
    kj2 = lax.broadcasted_iota(jnp.int32, (BLOCK, 2 * BLOCK), 1)
    valid2 = (kj2 >= qi2) & (kj2 <= qi2 + BLOCK)

    def residue(r):
        first = pl.ds(0, BLOCK)
        o, l = _attn_block(qkv_ref[0, r, first, qs], qkv_ref[0, r, first, ks],
                           qkv_ref[0, r, first, vs], kj <= qi)
        o_ref[0, r, first, :] = o
        l_ref[0, r, first, :] = l

        def body(n, carry):
            cur = pl.ds(pl.multiple_of(n * BLOCK, BLOCK), BLOCK)
            win = pl.ds(pl.multiple_of((n - 1) * BLOCK, BLOCK), 2 * BLOCK)
            o, l = _attn_block(qkv_ref[0, r, cur, qs], qkv_ref[0, r, win, ks],
                               qkv_ref[0, r, win, vs], valid2)
            o_ref[0, r, cur, :] = o
            l_ref[0, r, cur, :] = l
            return carry

        if nb > 1:
            lax.fori_loop(1, nb, body, 0)

    if dil <= 4:
        for r in range(dil):
            residue(r)
    else:
        def rbody(r, carry):
            residue(r)
            return carry
        lax.fori_loop(0, dil, rbody, 0)


def _dilated_attention(qkv, dilation):
    batch, _, sub, _ = qkv.shape
    ospec = pl.BlockSpec((1, dilation, sub, B_OUT), lambda b: (b, 0, 0, 0))
    oshape = jax.ShapeDtypeStruct((batch, dilation, sub, B_OUT), F32)
    return pl.pallas_call(
        _attn_kernel,
        grid=(batch,),
        in_specs=[pl.BlockSpec((1, dilation, sub, B_WIDTH), lambda b: (b, 0, 0, 0))],
        out_specs=[ospec, ospec],
        out_shape=[oshape, oshape],
        compiler_params=_params("parallel"),
        name=f"dilated_attn_d{dilation}",
    )(qkv)


def _shift_rows(x, prev_row):
    rolled = pltpu.roll(x, 1, 0)
    first = lax.broadcasted_iota(jnp.int32, x.shape, 0) == 0
    return jnp.where(first, prev_row, rolled)


NN = (((1,), (0,)), ((), ()))
NT = (((1,), (1,)), ((), ()))
TN = (((0,), (0,)), ((), ()))


def _dot(a, b, dims=NN):
    return lax.dot_general(a.astype(BF16), b.astype(BF16), dims, preferred_element_type=F32)


def _split(a, pieces):
    out = []
    for _ in range(pieces):
        p = a.astype(BF16)
        out.append(p)
        a = a - p.astype(F32)
    return out


def _dot3(a, b, dims=NN):
    (ah, al), (bh, bl) = _split(a, 2), _split(b, 2)
    return _dot(ah, bh, dims) + _dot(ah, bl, dims) + _dot(al, bh, dims)


def _head_sum(x):
    lo = lax.broadcasted_iota(jnp.int32, (x.shape[0], LANES), 1) < HEAD_DIM
    outs = []
    for c in range(x.shape[1] // LANES):
        xc = x[:, c * LANES:(c + 1) * LANES]
        s_lo = jnp.sum(jnp.where(lo, xc, 0.0), axis=-1, keepdims=True)
        s_hi = jnp.sum(jnp.where(lo, 0.0, xc), axis=-1, keepdims=True)
        outs.append(jnp.where(lo, s_lo, s_hi))
    return jnp.concatenate(outs, axis=-1)


def _rwkv_kernel(*refs, has_vres):
    if has_vres:
        (rkv_ref, lora_ref, vf_ref, mu_ref, vec_ref, w2_ref, a2_ref, g2_ref, v2_ref,
         yc_ref, vout_ref, state_ref, prkv_ref, plora_ref, y_ref) = refs
    else:
        (rkv_ref, lora_ref, mu_ref, vec_ref, w2_ref, a2_ref, g2_ref,
         yc_ref, vout_ref, state_ref, prkv_ref, plora_ref, y_ref) = refs
    nbat, ch = rkv_ref.shape[0], rkv_ref.shape[1]
    rows = nbat * ch
    hd = HEAD_DIM
    row_of = [slice(j * ch, (j + 1) * ch) for j in range(nbat)]

    @pl.when(pl.program_id(1) == 0)
    def _():
        state_ref[...] = jnp.zeros_like(state_ref)
        prkv_ref[...] = jnp.zeros_like(prkv_ref)
        plora_ref[...] = jnp.zeros_like(plora_ref)

    def shifted(x, prev_ref):
        parts = []
        for j, rs in enumerate(row_of):
            parts.append(_shift_rows(x[rs], prev_ref[8 * j:8 * j + 1, :]))
            prev_ref[8 * j:8 * j + 1, :] = x[(j + 1) * ch - 1:(j + 1) * ch, :]
        return jnp.concatenate(parts, axis=0)

    rkv = rkv_ref[...].reshape(rows, 3 * C_WIDTH)
    rkv_prev = shifted(rkv, prkv_ref)
    lora_all = lora_ref[...].reshape(rows, LORA_COLS)
    lora = lora_all[:, :LORA_HALF] + shifted(lora_all[:, LORA_HALF:], plora_ref)

    w0, a0, v0 = vec_ref[0:1, :], vec_ref[1:2, :], vec_ref[2:3, :]
    k_k, k_a, r_k = vec_ref[3:4, :], vec_ref[4:5, :], vec_ref[5:6, :]
    ln_w, ln_b = vec_ref[6:7, :], vec_ref[7:8, :]

    def mix(i):
        cur = rkv[:, i * C_WIDTH:(i + 1) * C_WIDTH]
        prev = rkv_prev[:, i * C_WIDTH:(i + 1) * C_WIDTH]
        return cur + (prev - cur) * mu_ref[i:i + 1, :]

    r, k, v = mix(0), mix(1), mix(2)
    wa = lora[:, 0:LANES]
    z = w0 + _dot3(jnp.tanh(wa), w2_ref[...])
    log_decay = -float(np.exp(-0.5)) * _sigmoid(z)
    a = _sigmoid(a0 + _dot(wa, a2_ref[...]))
    gate = _dot(_sigmoid(lora[:, LANES:2 * LANES]), g2_ref[...])
    if has_vres:
        vmix = _sigmoid(v0 + _dot(lora[:, 2 * LANES:3 * LANES], v2_ref[...]))
        v = v + (vf_ref[...].reshape(rows, C_WIDTH) - v) * vmix
    vout_ref[...] = v.reshape(nbat, ch, C_WIDTH)

    kk = k * k_k
    kk = kk * lax.rsqrt(jnp.maximum(_head_sum(kk * kk), 1e-24))
    k = k * (1.0 + (a - 1.0) * k_a)
    bonus = _head_sum(r * k * r_k) * v
    b = kk * a

    ti = lax.broadcasted_iota(jnp.int32, (rows, rows), 0)
    si = lax.broadcasted_iota(jnp.int32, (rows, rows), 1)
    tri = ((si <= ti) & (si // ch == ti // ch)).astype(BF16)
    cum = functools.reduce(lambda s, t: s + t, [_dot(tri, p) for p in _split(log_decay, 3)])
    ends = [cum[(j + 1) * ch - 1:(j + 1) * ch, :] for j in range(nbat)]
    cum_end = jnp.concatenate([jnp.broadcast_to(e, (ch, C_WIDTH)) for e in ends], axis=0)
    e_neg = jnp.exp(-cum)
    e_end = jnp.exp(cum_end - cum)
    kt = kk * jnp.exp(cum - log_decay)
    rt = r * jnp.exp(cum)
    bt = b * e_neg
    kq = k * e_neg
    kh = k * e_end
    bh = b * e_end
    unit = (lax.broadcasted_iota(jnp.int32, (8, LANES), 0) == 0).astype(BF16)
    w_end_t = [functools.reduce(lambda s, t: s + t,
                                [_dot(jnp.broadcast_to(p, (8, C_WIDTH)), unit, TN)
                                 for p in _split(jnp.exp(e), 2)])
               for e in ends]

    lane = lax.broadcasted_iota(jnp.int32, (ch, LANES), 1)
    lo = lane < hd
    lo2 = lax.broadcasted_iota(jnp.int32, (2 * ch, LANES), 1) < hd
    row4 = lax.broadcasted_iota(jnp.int32, (4 * ch, LANES), 0)
    col4 = lax.broadcasted_iota(jnp.int32, (4 * ch, LANES), 1) % ch
    t4 = row4 % ch
    tri_mask = (col4 < t4) | ((col4 == t4) & ((row4 // ch) % 2 == 1))
    eye_right = (lane - hd == lax.broadcasted_iota(jnp.int32, (ch, LANES), 0)).astype(F32)
    brow = lax.broadcasted_iota(jnp.int32, (LANES, LANES), 0) // hd
    bcol = lax.broadcasted_iota(jnp.int32, (LANES, LANES), 1) // hd
    block_diag = brow == bcol
    zeros = jnp.zeros((ch, LANES), F32)

    npair = C_HEADS // 2
    chains = [(j, p) for j in range(nbat) for p in range(npair)]
    at = lambda t, c: t[row_of[c[0]], c[1] * LANES:(c[1] + 1) * LANES]
    idx = range(len(chains))
    s_bd = [state_ref[j * npair + p] for j, p in chains]
    l2 = [jnp.concatenate([at(kt, c), at(rt, c)], axis=0) for c in chains]
    ss = [_dot(l2[i], s_bd[i]) for i in idx]
    g = [jnp.where(tri_mask,
                   _dot(jnp.concatenate([jnp.where(lo2, l2[i], 0.0), jnp.where(lo2, 0.0, l2[i])],
                                        axis=0),
                        jnp.concatenate([at(bt, c), at(kq, c)], axis=0), NT), 0.0)
         for i, c in zip(idx, chains)]
    xv = [_dot(jnp.concatenate([g[i][0:ch], g[i][2 * ch:3 * ch]], axis=0),
               jnp.concatenate([zeros, at(v, c)], axis=0))
          for i, c in zip(idx, chains)]
    heads = [(i, h) for i in idx for h in range(2)]
    n0 = [-g[i][2 * h * ch:(2 * h + 1) * ch] for i, h in heads]
    w = [_dot(n[:, :ch], jnp.where(lo, n, eye_right)) + jnp.where(lo, 0.0, eye_right) for n in n0]
    for _ in range(int(np.log2(ch)) - 1):
        w = [_dot(t[:, :ch], t) + jnp.where(lo, 0.0, t) for t in w]
    us = [_dot(w[2 * i + h],
               jnp.concatenate([zeros, ss[i][0:ch] + xv[i][h * ch:(h + 1) * ch]], axis=0))
          for i, h in heads]
    u = [jnp.where(lo, us[2 * i], us[2 * i + 1]) for i in idx]
    yv = [_dot(jnp.concatenate([g[i][ch:2 * ch], g[i][3 * ch:4 * ch]], axis=0),
               jnp.concatenate([-u[i], at(v, c)], axis=0))
          for i, c in zip(idx, chains)]
    upd = [_dot(jnp.concatenate([at(kh, c), at(bh, c)], axis=0),
                jnp.concatenate([at(v, c), -u[i]], axis=0), TN)
           for i, c in zip(idx, chains)]
    for i, (j, p) in zip(idx, chains):
        ps = slice(p * LANES, (p + 1) * LANES)
        y_ref[row_of[j], ps] = ss[i][ch:2 * ch] + jnp.where(lo, yv[i][0:ch], yv[i][ch:2 * ch])
        state_ref[j * npair + p] = w_end_t[j][ps, :] * s_bd[i] + jnp.where(block_diag, upd[i], 0.0)

    y = y_ref[...]
    d = y - _head_sum(y) * (1.0 / hd)
    var = _head_sum(d * d) * (1.0 / hd)
    yn = d * lax.rsqrt(var + RWKV_GN_EPS) * ln_w + ln_b
    yc_ref[...] = ((yn + bonus) * gate).astype(BF16).reshape(nbat, ch, C_WIDTH)


def _rwkv(rkv, lora, v_first, mu_rkv, vecs, w2, a2, g2, v2, batch, seq):
    ch = RWKV_CHUNK
    nbat = RWKV_BATCH_ROWS
    has_vres = v_first is not None
    per_seq = lambda t: t.reshape(batch, seq, t.shape[-1])
    blk = lambda width: pl.BlockSpec((nbat, ch, width), lambda i, c: (i, c, 0))
    ins = [per_seq(rkv), per_seq(lora)] + ([per_seq(v_first)] if has_vres else []) \
        + [mu_rkv, vecs, w2, a2, g2] + ([v2] if has_vres else [])
    specs = [blk(3 * C_WIDTH), blk(LORA_COLS)] + ([blk(C_WIDTH)] if has_vres else []) \
        + [_full((3, C_WIDTH)), _full((8, C_WIDTH)), _full((LANES, C_WIDTH)),
           _full((LANES, C_WIDTH)), _full((LANES, C_WIDTH))] \
        + ([_full((LANES, C_WIDTH))] if has_vres else [])
    yc, v_out = pl.pallas_call(
        functools.partial(_rwkv_kernel, has_vres=has_vres),
        grid=(batch // nbat, seq // ch),
        in_specs=specs,
        out_specs=[blk(C_WIDTH), blk(C_WIDTH)],
        out_shape=[jax.ShapeDtypeStruct((batch, seq, C_WIDTH), BF16),
                   jax.ShapeDtypeStruct((batch, seq, C_WIDTH), F32)],
        scratch_shapes=[pltpu.VMEM((nbat * C_HEADS // 2, LANES, LANES), F32),
                        pltpu.VMEM((8 * nbat, 3 * C_WIDTH), F32),
                        pltpu.VMEM((8 * nbat, LORA_HALF), F32),
                        pltpu.VMEM((nbat * ch, C_WIDTH), F32)],
        compiler_params=_params("parallel", "arbitrary"),
        name="rwkv7_chunked",
    )(*ins)
    return yc.reshape(batch * seq, C_WIDTH), v_out.reshape(batch * seq, C_WIDTH)


def _merge_kernel(x_ref, g_ref, ya_ref, o0_ref, l0_ref, o1_ref, l1_ref, o2_ref, l2_ref, yc_ref,
                  wg_ref, bg_ref, wa_ref, wb_ref, wc_ref, wo_ref, g2_ref, wr_ref, br_ref,
                  xm_ref, h2_ref, comb_ref, nat_ref):
    x = x_ref[...]
    tm = x.shape[0]
    hb = _rmsnorm(x, g_ref[...]).astype(BF16)

    def token_order(ref, slot):
        dil = ref.shape[1]
        if dil == 1:
            return ref[0, 0]
        chunks = []
        for c in range(B_OUT // LANES):
            for r in range(dil):
                nat_ref[slot, c, pl.ds(r, tm // dil, stride=dil), :] = \
                    ref[0, r, :, c * LANES:(c + 1) * LANES]
            chunks.append(nat_ref[slot, c])
        return jnp.concatenate(chunks, axis=-1)

    l0, l1, l2 = token_order(l0_ref, 0), token_order(l1_ref, 0), token_order(l2_ref, 1)
    lm = jnp.maximum(jnp.maximum(l0, l1), l2)
    e0, e1, e2 = jnp.exp(l0 - lm), jnp.exp(l1 - lm), jnp.exp(l2 - lm)
    o0, o1, o2 = token_order(o0_ref, 0), token_order(o1_ref, 2), token_order(o2_ref, 3)
    yb = ((e0 * o0 + e1 * o1 + e2 * o2) / (e0 + e1 + e2)).astype(BF16)

    merged = None
    for j, (y, w_ref) in enumerate(((ya_ref[...], wa_ref), (yb, wb_ref), (yc_ref[...], wc_ref))):
        cs = slice(j * D_MODEL, (j + 1) * D_MODEL)
        gate = _sigmoid(jnp.dot(hb, wg_ref[:, cs], preferred_element_type=F32) + bg_ref[:, cs])
        term = gate * jnp.dot(y, w_ref[...], preferred_element_type=F32)
        merged = term if merged is None else merged + term
    xm = x + jnp.dot(merged.astype(BF16), wo_ref[...], preferred_element_type=F32)
    xm_ref[...] = xm

    h2 = _rmsnorm(xm, g2_ref[...])
    h2_ref[...] = h2.astype(BF16)

    logits = _dot3(wr_ref[...], h2, NT) + br_ref[:, 0:1]
    gl = [logits[g:g + 1, :] for g in range(N_GROUPS)]
    gmax = functools.reduce(jnp.maximum, gl)
    gsum = functools.reduce(lambda s, t: s + t, [jnp.exp(t - gmax) for t in gl])
    gp = 1.0 / gsum
    taken = jnp.zeros_like(gmax, dtype=jnp.bool_)
    sel = []
    for g in range(N_GROUPS):
        pick = (gl[g] == gmax) & jnp.logical_not(taken)
        sel.append(pick)
        taken = taken | pick
    el = []
    for e in range(EXPERTS_PER_GROUP):
        acc = jnp.zeros_like(gmax)
        for g in range(N_GROUPS):
            r0 = N_GROUPS + g * EXPERTS_PER_GROUP + e
            acc = jnp.where(sel[g], logits[r0:r0 + 1, :], acc)
        el.append(acc)
    emax = functools.reduce(jnp.maximum, el)
    ex = [jnp.exp(t - emax) for t in el]
    esum = functools.reduce(lambda s, t: s + t, ex)
    p = [t / esum for t in ex]
    p1 = functools.reduce(jnp.maximum, p)
    taken = jnp.zeros_like(gmax, dtype=jnp.bool_)
    first = []
    for e in range(EXPERTS_PER_GROUP):
        pick = (p[e] == p1) & jnp.logical_not(taken)
        first.append(pick)
        taken = taken | pick
    rest = [jnp.where(first[e], -1.0, p[e]) for e in range(EXPERTS_PER_GROUP)]
    p2 = functools.reduce(jnp.maximum, rest)
    taken = jnp.zeros_like(gmax, dtype=jnp.bool_)
    second = []
    for e in range(EXPERTS_PER_GROUP):
        pick = (rest[e] == p2) & jnp.logical_not(taken)
        second.append(pick)
        taken = taken | pick
    denom = p1 + p2
    for e in range(EXPERTS_PER_GROUP):
        in_group = jnp.where(first[e], p1 / denom, jnp.where(second[e], p2 / denom, 0.0))
        for g in range(N_GROUPS):
            r0 = g * EXPERTS_PER_GROUP + e
            comb_ref[r0:r0 + 1, :] = jnp.where(sel[g], gp * in_group, 0.0)


def _merge(x2, g, ya, attn, yc, wg, bg, wa, wb, wc, wo, g2, wr, br, seq):
    n = x2.shape[0]
    tm = MERGE_ROW_TILE
    row = lambda width: pl.BlockSpec((tm, width), lambda i: (i, 0))
    attn_flat = [t for pair in attn for t in pair]
    attn_specs = [_residue_spec(tm, t.shape[1], B_OUT, seq // tm) for t in attn_flat]
    return pl.pallas_call(
        _merge_kernel,
        grid=(n // tm,),
        in_specs=[row(D_MODEL), _full((1, D_MODEL)), row(A_WIDTH)] + attn_specs
        + [row(C_WIDTH), _full((D_MODEL, 3 * D_MODEL)), _full((1, 3 * D_MODEL)),
           _full((A_WIDTH, D_MODEL)), _full((B_OUT, D_MODEL)), _full((C_WIDTH, D_MODEL)),
           _full((D_MODEL, D_MODEL)), _full((1, D_MODEL)), _full((32, D_MODEL)), _full((32, LANES))],
        out_specs=[row(D_MODEL), row(D_MODEL), pl.BlockSpec((N_EXPERTS, tm), lambda i: (0, i))],
        out_shape=[jax.ShapeDtypeStruct((n, D_MODEL), F32),
                   jax.ShapeDtypeStruct((n, D_MODEL), BF16),
                   jax.ShapeDtypeStruct((N_EXPERTS, n), F32)],
        scratch_shapes=[pltpu.VMEM((4, B_OUT // LANES, tm, LANES), F32)],
        compiler_params=_params("parallel"),
        name="merge_router",
    )(x2, g, ya, *attn_flat, yc, wg, bg, wa, wb, wc, wo, g2, wr, br)


def _moe_kernel(xm_ref, h2_ref, comb_ref, wg_ref, wu_ref, wd_ref, gf_ref, out_ref, *, final_norm):
    e = pl.program_id(1)

    @pl.when(e == 0)
    def _():
        out_ref[...] = xm_ref[...]

    h2 = h2_ref[...]
    gate = jnp.dot(h2, wg_ref[0], preferred_element_type=F32)
    up = jnp.dot(h2, wu_ref[0], preferred_element_type=F32)
    lane = lax.broadcasted_iota(jnp.int32, comb_ref.shape, 1)
    cw = jnp.sum(jnp.where(lane == e, comb_ref[...], 0.0), axis=-1, keepdims=True)
    hid = gate * _sigmoid(gate) * up * cw
    out_ref[...] += jnp.dot(hid.astype(BF16), wd_ref[0], preferred_element_type=F32)

    if final_norm:
        @pl.when(e == N_EXPERTS - 1)
        def _():
            out_ref[...] = _rmsnorm(out_ref[...], gf_ref[...])


def _moe(xm, h2, comb, wg, wu, wd, g_final, final_norm):
    n = xm.shape[0]
    tm = MOE_ROW_TILE
    return pl.pallas_call(
        functools.partial(_moe_kernel, final_norm=final_norm),
        grid=(n // tm, N_EXPERTS),
        in_specs=[pl.BlockSpec((tm, D_MODEL), lambda i, e: (i, 0)),
                  pl.BlockSpec((tm, D_MODEL), lambda i, e: (i, 0)),
                  pl.BlockSpec((tm, N_EXPERTS), lambda i, e: (i, 0)),
                  pl.BlockSpec((1, D_MODEL, EXPERT_FF), lambda i, e: (e, 0, 0)),
                  pl.BlockSpec((1, D_MODEL, EXPERT_FF), lambda i, e: (e, 0, 0)),
                  pl.BlockSpec((1, EXPERT_FF, D_MODEL), lambda i, e: (e, 0, 0)),
                  _full((1, D_MODEL))],
        out_specs=pl.BlockSpec((tm, D_MODEL), lambda i, e: (i, 0)),
        out_shape=jax.ShapeDtypeStruct((n, D_MODEL), F32),
        compiler_params=_params("parallel", "arbitrary"),
        name="moe_ffn",
    )(xm, h2, comb, wg, wu, wd, g_final)


def _pack_lora(mu_wag, w1, a1, g1, mu_v, v1):
    d = D_MODEL
    if v1 is None:
        mu_v = jnp.zeros((d,), F32)
        v1 = jnp.zeros((d, V_LORA), F32)
    pad = jnp.zeros((d, LORA_HALF - (W_LORA + A_LORA + G_LORA + V_LORA)), F32)
    mus = (mu_wag[0], mu_wag[1], mu_wag[2], mu_v)
    ws = (w1, a1, g1, v1)
    cur = [w * (1.0 - m)[:, None] for w, m in zip(ws, mus)]
    prev = [w * m[:, None] for w, m in zip(ws, mus)]
    return jnp.concatenate(cur + [pad] + prev + [pad], axis=1)


def _in_col_order():
    b0 = 2 * A_WIDTH
    cols = list(range(b0))
    for g in range(len(DILATED_PATTERNS)):
        for part in range(3):
            lo = b0 + part * B_WIDTH + g * B_OUT
            cols += list(range(lo, lo + B_OUT))
    return np.asarray(cols + list(range(b0 + 3 * B_WIDTH, IN_COLS)), np.int32)


_IN_COL_ORDER = _in_col_order()


def _pad_rows(w, lo, total=LANES):
    return jnp.zeros((total, w.shape[1]), F32).at[lo:lo + w.shape[0]].set(w)


def kernel(x, positions, norm_mix_g, w_in, gmlp_ln_g, gmlp_ln_b, gmlp_ws, gmlp_bs, rwkv_mu_rkv, rwkv_mu_wag, rwkv_w0, rwkv_w1, rwkv_w2, rwkv_a0, rwkv_a1, rwkv_a2, rwkv_g1, rwkv_g2, rwkv_k_k, rwkv_k_a, rwkv_r_k, rwkv_ln_w, rwkv_ln_b, rwkv_mu_v, rwkv_v0, rwkv_v1, rwkv_v2, w_branch_a, w_branch_b, w_branch_c, w_gate, b_gate, w_out, norm_ffn_g, router_group_w, router_group_b, router_expert_w, router_expert_b, expert_w_gate, expert_w_up, expert_w_down, final_norm_g):
    batch, seq, d = x.shape
    depth = w_in.shape[0]
    n = batch * seq
    x2 = x.reshape(n, d)
    rope = _rope_tables(positions)
    v_first = None
    for l in range(depth):
        vres = l > 0
        lora_w = _pack_lora(rwkv_mu_wag[l], rwkv_w1[l], rwkv_a1[l], rwkv_g1[l],
                            rwkv_mu_v[l - 1] if vres else None, rwkv_v1[l - 1] if vres else None)
        w_all = jnp.concatenate([w_in[l][:, _IN_COL_ORDER], lora_w], axis=1).astype(BF16)
        ya, qkv0, qkv1, qkv2, rkv, lora = _inproj(
            x2, norm_mix_g[l][None], w_all, gmlp_ln_g[l][None], gmlp_ln_b[l][None], gmlp_ws[l],
            gmlp_bs[l].T, rope, batch, seq)

        attn = [_dilated_attention(qkv, dil)
                for qkv, (_, dil) in zip((qkv0, qkv1, qkv2), DILATED_PATTERNS)]

        zero = jnp.zeros((C_WIDTH,), F32)
        vecs = jnp.stack([rwkv_w0[l], rwkv_a0[l], rwkv_v0[l - 1] if vres else zero, rwkv_k_k[l],
                          rwkv_k_a[l], rwkv_r_k[l].reshape(C_WIDTH), rwkv_ln_w[l], rwkv_ln_b[l]])
        yc, v_c = _rwkv(rkv, lora, v_first, rwkv_mu_rkv[l], vecs,
                        _pad_rows(rwkv_w2[l], 0), _pad_rows(rwkv_a2[l], W_LORA), rwkv_g2[l],
                        _pad_rows(rwkv_v2[l - 1], 0) if vres else None, batch, seq)
        if l == 0:
            v_first = v_c

        wr = jnp.concatenate([router_group_w[l], router_expert_w[l]], axis=1).T
        wr = jnp.zeros((32, d), F32).at[:wr.shape[0]].set(wr)
        br = jnp.concatenate([router_group_b[l], router_expert_b[l]])
        br = jnp.zeros((32, LANES), F32).at[:br.shape[0], 0].set(br)
        xm, h2, comb_t = _merge(
            x2, norm_mix_g[l][None], ya, attn, yc, w_gate[l].astype(BF16), b_gate[l][None],
            w_branch_a[l].astype(BF16), w_branch_b[l].astype(BF16), w_branch_c[l].astype(BF16),
            w_out[l].astype(BF16), norm_ffn_g[l][None], wr, br, seq)

        x2 = _moe(xm, h2, comb_t.T, expert_w_gate[l].astype(BF16), expert_w_up[l].astype(BF16),
                  expert_w_down[l].astype(BF16), final_norm_g[None], final_norm=(l == depth - 1))
    return x2.reshape(batch, seq, d)
```

```python
import functools

import numpy as np
import jax
import jax.numpy as jnp
from jax import lax
from jax.experimental import pallas as pl
from jax.experimental.pallas import tpu as pltpu

F32 = jnp.float32
BF16 = jnp.bfloat16
HIGHEST = lax.Precision.HIGHEST

D_MODEL = 1024
HEAD_DIM = 64
A_GROUPS = 4
A_WIDTH = 512
CHUNK = 128
DILATED_PATTERNS = ((128, 1), (512, 4), (2048, 16))
B_HEADS_PER_GROUP = 4
B_WIDTH = 768
B_OUT = 256
BLOCK = 128
ROPE_THETA = 500000.0
ROPE_DIMS = 16
C_HEADS = 8
C_WIDTH = 512
W_LORA, A_LORA, V_LORA, G_LORA = 64, 64, 32, 128
RWKV_GN_EPS = HEAD_DIM * 1e-5
IN_COLS = 2 * A_WIDTH + 3 * B_WIDTH + 3 * C_WIDTH
LORA_HALF = 384
LORA_COLS = 2 * LORA_HALF
N_GROUPS = 4
EXPERTS_PER_GROUP = 4
N_EXPERTS = 16
EXPERT_FF = 512
RMS_EPS = 1e-6
LN_EPS = 1e-5

LANES = 128
ROW_TILE = 512
MERGE_ROW_TILE = 512
RWKV_CHUNK = 64
RWKV_BATCH_ROWS = 2
MOE_ROW_TILE = 1024
VMEM_LIMIT = 56 * 1024 * 1024
NEG_BIG = -1e30


def _params(*sem):
    return pltpu.CompilerParams(dimension_semantics=sem, vmem_limit_bytes=VMEM_LIMIT)


def _full(shape):
    nd = len(shape)
    return pl.BlockSpec(shape, lambda *_: (0,) * nd)


def _gelu_tanh(x):
    return 0.5 * x * (1.0 + jnp.tanh(0.7978845608028654 * (x + 0.044715 * (x * x * x))))


def _sigmoid(x):
    return 1.0 / (1.0 + jnp.exp(-x))


def _rmsnorm(x, g):
    return x * lax.rsqrt(jnp.mean(x * x, axis=-1, keepdims=True) + RMS_EPS) * g


def _rope_table_kernel(pos_ref, consts_ref, c_ref, sa_ref, sb_ref):
    ang = pos_ref[...].astype(F32) * consts_ref[0:1, :]
    c_ref[...] = jnp.cos(ang)
    s = jnp.sin(ang)
    sa_ref[...] = -s * consts_ref[1:2, :]
    sb_ref[...] = s * consts_ref[2:3, :]


def _rope_tables(positions):
    n = positions.size
    half = ROPE_DIMS // 2
    inv_freq = ROPE_THETA ** (-jnp.arange(half, dtype=F32) / half)
    lane = np.arange(LANES) % HEAD_DIM
    in_rope = lane < ROPE_DIMS
    invf_row = jnp.where(in_rope, inv_freq[lane % half], 0.0)
    consts = jnp.zeros((8, LANES), F32)
    consts = consts.at[0].set(invf_row)
    consts = consts.at[1].set(jnp.asarray(lane < half, F32))
    consts = consts.at[2].set(jnp.asarray((lane >= half) & in_rope, F32))
    tm = 1024
    out = jax.ShapeDtypeStruct((n, LANES), F32)
    return pl.pallas_call(
        _rope_table_kernel,
        grid=(n // tm,),
        in_specs=[pl.BlockSpec((tm, 1), lambda i: (i, 0)), _full((8, LANES))],
        out_specs=[pl.BlockSpec((tm, LANES), lambda i: (i, 0))] * 3,
        out_shape=[out, out, out],
        compiler_params=_params("parallel"),
        name="rope_tables",
    )(positions.reshape(n, 1), consts)


def _inproj_kernel(x_ref, g_ref, w_ref, wl_ref, lng_ref, lnb_ref, ws_ref, bst_ref, c_ref, sa_ref,
                   sb_ref, ya_ref, qkv0_ref, qkv1_ref, qkv2_ref, rkv_ref, lora_ref, nat_ref):
    tm = x_ref.shape[0]
    hb = _rmsnorm(x_ref[...], g_ref[...]).astype(BF16)

    def proj(lo, width):
        return jnp.dot(hb, w_ref[0, :, lo:lo + width], preferred_element_type=F32)

    uv = proj(0, 2 * A_WIDTH)
    u = _gelu_tanh(uv[:, :A_WIDTH])
    v = _gelu_tanh(uv[:, A_WIDTH:])
    mu = jnp.mean(v, axis=-1, keepdims=True)
    d = v - mu
    var = jnp.mean(d * d, axis=-1, keepdims=True)
    vn = (d * lax.rsqrt(var + LN_EPS) * lng_ref[...] + lnb_ref[...]).astype(BF16)
    row = lax.broadcasted_iota(jnp.int32, (CHUNK, CHUNK), 0)
    col = lax.broadcasted_iota(jnp.int32, (CHUNK, CHUNK), 1)
    for g in range(A_GROUPS):
        wg = jnp.where(row >= col, ws_ref[g], 0.0).astype(BF16)
        bias = bst_ref[:, g:g + 1]
        cs = slice(g * LANES, (g + 1) * LANES)
        for c in range(tm // CHUNK):
            rs = slice(c * CHUNK, (c + 1) * CHUNK)
            mixed = jnp.dot(wg, vn[rs, cs], preferred_element_type=F32) + bias
            ya_ref[rs, cs] = (u[rs, cs] * mixed).astype(BF16)

    cos, sa, sb = c_ref[...], sa_ref[...], sb_ref[...]
    q_off = 2 * A_WIDTH
    for grp, (qkv_ref, (_, dil)) in enumerate(zip((qkv0_ref, qkv1_ref, qkv2_ref), DILATED_PATTERNS)):
        acc = [proj(q_off + part * B_WIDTH + grp * B_OUT, B_OUT) for part in range(3)]
        for c in range(B_WIDTH // LANES):
            t = acc[c * LANES // B_OUT][:, c * LANES % B_OUT:c * LANES % B_OUT + LANES]
            if c < 2 * B_OUT // LANES:
                t = t * cos + pltpu.roll(t, LANES - 8, 1) * sa + pltpu.roll(t, 8, 1) * sb
            if c < B_OUT // LANES:
                t = t * (HEAD_DIM ** -0.5)
            cs = slice(c * LANES, (c + 1) * LANES)
            if dil == 1:
                qkv_ref[0, 0, :, cs] = t.astype(BF16)
            else:
                nat_ref[c] = t
                for r in range(dil):
                    qkv_ref[0, r, :, cs] = nat_ref[c, pl.ds(r, tm // dil, stride=dil), :].astype(BF16)

    rkv_ref[...] = proj(q_off + 3 * B_WIDTH, 3 * C_WIDTH)
    lora_ref[...] = jnp.dot(hb, wl_ref[...], preferred_element_type=F32)


def _residue_spec(tm, dil, width, tiles_per_seq):
    return pl.BlockSpec((1, dil, tm // dil, width),
                        lambda i: (i // tiles_per_seq, 0, i % tiles_per_seq, 0))


def _inproj(x2, g, w_in, layer, w_lora, ln_g, ln_b, ws, bs_t, rope, batch, seq):
    n = x2.shape[0]
    tm = ROW_TILE
    row = lambda width: pl.BlockSpec((tm, width), lambda i: (i, 0))
    dils = [d for _, d in DILATED_PATTERNS]
    return pl.pallas_call(
        _inproj_kernel,
        grid=(n // tm,),
        in_specs=[row(D_MODEL), _full((1, D_MODEL)),
                  pl.BlockSpec((1, D_MODEL, IN_COLS), lambda i: (layer, 0, 0)),
                  _full((D_MODEL, LORA_COLS)),
                  _full((1, A_WIDTH)), _full((1, A_WIDTH)), _full((A_GROUPS, CHUNK, CHUNK)),
                  _full((CHUNK, A_GROUPS)), row(LANES), row(LANES), row(LANES)],
        out_specs=[row(A_WIDTH)] + [_residue_spec(tm, d, B_WIDTH, seq // tm) for d in dils]
        + [row(3 * C_WIDTH), row(LORA_COLS)],
        out_shape=[jax.ShapeDtypeStruct((n, A_WIDTH), BF16)]
        + [jax.ShapeDtypeStruct((batch, d, seq // d, B_WIDTH), BF16) for d in dils]
        + [jax.ShapeDtypeStruct((n, 3 * C_WIDTH), F32),
           jax.ShapeDtypeStruct((n, LORA_COLS), F32)],
        scratch_shapes=[pltpu.VMEM((B_WIDTH // LANES, tm, LANES), F32)],
        compiler_params=_params("parallel"),
        name="inproj",
    )(x2, g, w_in, w_lora, ln_g, ln_b, ws, bs_t, *rope)


def _attn_block(q, kw, vw, valid):
    lo = lax.broadcasted_iota(jnp.int32, (BLOCK, LANES), 1) < HEAD_DIM
    pairs = range(B_OUT // LANES)
    scores = []
    for p in pairs:
        ps = slice(p * LANES, (p + 1) * LANES)
        qp = q[:, ps]
        zero = jnp.zeros_like(qp)
        for part in (jnp.where(lo, qp, zero), jnp.where(lo, zero, qp)):
            s = lax.dot_general(part, kw[:, ps], NT, preferred_element_type=F32)
            scores.append(jnp.where(valid, s, NEG_BIG))
    m = [jnp.max(s, axis=-1, keepdims=True) for s in scores]
    e = [jnp.exp(s - mx) for s, mx in zip(scores, m)]
    l = [jnp.sum(t, axis=-1, keepdims=True) for t in e]
    pv = [jnp.dot(t.astype(BF16), vw[:, (h // 2) * LANES:(h // 2 + 1) * LANES],
                  preferred_element_type=F32) for h, t in enumerate(e)]
    out = [jnp.where(lo, pv[2 * p] / l[2 * p], pv[2 * p + 1] / l[2 * p + 1]) for p in pairs]
    lse = [mx + jnp.log(t) for mx, t in zip(m, l)]
    lse = [jnp.where(lo, lse[2 * p], lse[2 * p + 1]) for p in pairs]
    return jnp.concatenate(out, axis=-1), jnp.concatenate(lse, axis=-1)


def _attn_kernel(qkv_ref, o_ref, l_ref):
    dil, sub = qkv_ref.shape[1], qkv_ref.shape[2]
    nb = sub // BLOCK
    qs, ks, vs = (slice(i * B_OUT, (i + 1) * B_OUT) for i in range(3))
    qi = lax.broadcasted_iota(jnp.int32, (BLOCK, BLOCK), 0)
    kj = lax.broadcasted_iota(jnp.int32, (BLOCK, BLOCK), 1)
    qi2 = lax.broadcasted_iota(jnp.int32, (BLOCK, 2 * BLOCK), 0)
    kj2 = lax.broadcasted_iota(jnp.int32, (BLOCK, 2 * BLOCK), 1)
    valid2 = (kj2 >= qi2) & (kj2 <= qi2 + BLOCK)

    def residue(r):
        first = pl.ds(0, BLOCK)
        o, l = _attn_block(qkv_ref[0, r, first, qs], qkv_ref[0, r, first, ks],
                           qkv_ref[0, r, first, vs], kj <= qi)
        o_ref[0, r, first, :] = o
        l_ref[0, r, first, :] = l

        def body(n, carry):
            cur = pl.ds(pl.multiple_of(n * BLOCK, BLOCK), BLOCK)
            win = pl.ds(pl.multiple_of((n - 1) * BLOCK, BLOCK), 2 * BLOCK)
            o, l = _attn_block(qkv_ref[0, r, cur, qs], qkv_ref[0, r, win, ks],
                               qkv_ref[0, r, win, vs], valid2)
            o_ref[0, r, cur, :] = o
            l_ref[0, r, cur, :] = l
            return carry

        if nb > 1:
            lax.fori_loop(1, nb, body, 0)

    if dil <= 4:
        for r in range(dil):
            residue(r)
    else:
        def rbody(r, carry):
            residue(r)
            return carry
        lax.fori_loop(0, dil, rbody, 0)


def _dilated_attention(qkv, dilation):
    batch, _, sub, _ = qkv.shape
    ospec = pl.BlockSpec((1, dilation, sub, B_OUT), lambda b: (b, 0, 0, 0))
    oshape = jax.ShapeDtypeStruct((batch, dilation, sub, B_OUT), F32)
    return pl.pallas_call(
        _attn_kernel,
        grid=(batch,),
        in_specs=[pl.BlockSpec((1, dilation, sub, B_WIDTH), lambda b: (b, 0, 0, 0))],
        out_specs=[ospec, ospec],
        out_shape=[oshape, oshape],
        compiler_params=_params("parallel"),
        name=f"dilated_attn_d{dilation}",
    )(qkv)


def _shift_rows(x, prev_row):
    rolled = pltpu.roll(x, 1, 0)
    first = lax.broadcasted_iota(jnp.int32, x.shape, 0) == 0
    return jnp.where(first, prev_row, rolled)


NN = (((1,), (0,)), ((), ()))
NT = (((1,), (1,)), ((), ()))
TN = (((0,), (0,)), ((), ()))


def _dot(a, b, dims=NN):
    return lax.dot_general(a.astype(BF16), b.astype(BF16), dims, preferred_element_type=F32)


def _split(a, pieces):
    out = []
    for _ in range(pieces):
        p = a.astype(BF16)
        out.append(p)
        a = a - p.astype(F32)
    return out


def _dot3(a, b, dims=NN):
    (ah, al), (bh, bl) = _split(a, 2), _split(b, 2)
    return _dot(ah, bh, dims) + _dot(ah, bl, dims) + _dot(al, bh, dims)


def _head_sum(x):
    lo = lax.broadcasted_iota(jnp.int32, (x.shape[0], LANES), 1) < HEAD_DIM
    outs = []
    for c in range(x.shape[1] // LANES):
        xc = x[:, c * LANES:(c + 1) * LANES]
        s_lo = jnp.sum(jnp.where(lo, xc, 0.0), axis=-1, keepdims=True)
        s_hi = jnp.sum(jnp.where(lo, 0.0, xc), axis=-1, keepdims=True)
        outs.append(jnp.where(lo, s_lo, s_hi))
    return jnp.concatenate(outs, axis=-1)


def _rwkv_kernel(*refs, has_vres):
    if has_vres:
        (rkv_ref, lora_ref, vf_ref, mu_ref, vec_ref, w2_ref, a2_ref, g2_ref, v2_ref,
         yc_ref, vout_ref, state_ref, prkv_ref, plora_ref, y_ref) = refs
    else:
        (rkv_ref, lora_ref, mu_ref, vec_ref, w2_ref, a2_ref, g2_ref,
         yc_ref, vout_ref, state_ref, prkv_ref, plora_ref, y_ref) = refs
    nbat, ch = rkv_ref.shape[0], rkv_ref.shape[1]
    rows = nbat * ch
    hd = HEAD_DIM
    row_of = [slice(j * ch, (j + 1) * ch) for j in range(nbat)]

    @pl.when(pl.program_id(1) == 0)
    def _():
        state_ref[...] = jnp.zeros_like(state_ref)
        prkv_ref[...] = jnp.zeros_like(prkv_ref)
        plora_ref[...] = jnp.zeros_like(plora_ref)

    def shifted(x, prev_ref):
        parts = []
        for j, rs in enumerate(row_of):
            parts.append(_shift_rows(x[rs], prev_ref[8 * j:8 * j + 1, :]))
            prev_ref[8 * j:8 * j + 1, :] = x[(j + 1) * ch - 1:(j + 1) * ch, :]
        return jnp.concatenate(parts, axis=0)

    rkv = rkv_ref[...].reshape(rows, 3 * C_WIDTH)
    rkv_prev = shifted(rkv, prkv_ref)
    lora_all = lora_ref[...].reshape(rows, LORA_COLS)
    lora = lora_all[:, :LORA_HALF] + shifted(lora_all[:, LORA_HALF:], plora_ref)

    w0, a0, v0 = vec_ref[0:1, :], vec_ref[1:2, :], vec_ref[2:3, :]
    k_k, k_a, r_k = vec_ref[3:4, :], vec_ref[4:5, :], vec_ref[5:6, :]
    ln_w, ln_b = vec_ref[6:7, :], vec_ref[7:8, :]

    def mix(i):
        cur = rkv[:, i * C_WIDTH:(i + 1) * C_WIDTH]
        prev = rkv_prev[:, i * C_WIDTH:(i + 1) * C_WIDTH]
        return cur + (prev - cur) * mu_ref[i:i + 1, :]

    r, k, v = mix(0), mix(1), mix(2)
    wa = lora[:, 0:LANES]
    z = w0 + _dot3(jnp.tanh(wa), w2_ref[...])
    log_decay = -float(np.exp(-0.5)) * _sigmoid(z)
    a = _sigmoid(a0 + _dot(wa, a2_ref[...]))
    gate = _dot(_sigmoid(lora[:, LANES:2 * LANES]), g2_ref[...])
    if has_vres:
        vmix = _sigmoid(v0 + _dot(lora[:, 2 * LANES:3 * LANES], v2_ref[...]))
        v = v + (vf_ref[...].reshape(rows, C_WIDTH) - v) * vmix
    vout_ref[...] = v.reshape(nbat, ch, C_WIDTH)

    kk = k * k_k
    kk = kk * lax.rsqrt(jnp.maximum(_head_sum(kk * kk), 1e-24))
    k = k * (1.0 + (a - 1.0) * k_a)
    bonus = _head_sum(r * k * r_k) * v
    b = kk * a

    ti = lax.broadcasted_iota(jnp.int32, (rows, rows), 0)
    si = lax.broadcasted_iota(jnp.int32, (rows, rows), 1)
    tri = ((si <= ti) & (si // ch == ti // ch)).astype(BF16)
    cum = functools.reduce(lambda s, t: s + t, [_dot(tri, p) for p in _split(log_decay, 3)])
    ends = [cum[(j + 1) * ch - 1:(j + 1) * ch, :] for j in range(nbat)]
    cum_end = jnp.concatenate([jnp.broadcast_to(e, (ch, C_WIDTH)) for e in ends], axis=0)
    e_neg = jnp.exp(-cum)
    e_end = jnp.exp(cum_end - cum)
    kt = kk * jnp.exp(cum - log_decay)
    rt = r * jnp.exp(cum)
    bt = b * e_neg
    kq = k * e_neg
    kh = k * e_end
    bh = b * e_end
    unit = (lax.broadcasted_iota(jnp.int32, (8, LANES), 0) == 0).astype(BF16)
    w_end_t = [functools.reduce(lambda s, t: s + t,
                                [_dot(jnp.broadcast_to(p, (8, C_WIDTH)), unit, TN)
                                 for p in _split(jnp.exp(e), 2)])
               for e in ends]

    lane = lax.broadcasted_iota(jnp.int32, (ch, LANES), 1)
    lo = lane < hd
    lo2 = lax.broadcasted_iota(jnp.int32, (2 * ch, LANES), 1) < hd
    row4 = lax.broadcasted_iota(jnp.int32, (4 * ch, LANES), 0)
    col4 = lax.broadcasted_iota(jnp.int32, (4 * ch, LANES), 1) % ch
    t4 = row4 % ch
    tri_mask = (col4 < t4) | ((col4 == t4) & ((row4 // ch) % 2 == 1))
    eye_right = (lane - hd == lax.broadcasted_iota(jnp.int32, (ch, LANES), 0)).astype(F32)
    brow = lax.broadcasted_iota(jnp.int32, (LANES, LANES), 0) // hd
    bcol = lax.broadcasted_iota(jnp.int32, (LANES, LANES), 1) // hd
    block_diag = brow == bcol
    zeros = jnp.zeros((ch, LANES), F32)

    npair = C_HEADS // 2
    chains = [(j, p) for j in range(nbat) for p in range(npair)]
    at = lambda t, c: t[row_of[c[0]], c[1] * LANES:(c[1] + 1) * LANES]
    idx = range(len(chains))
    s_bd = [state_ref[j * npair + p] for j, p in chains]
    l2 = [jnp.concatenate([at(kt, c), at(rt, c)], axis=0) for c in chains]
    ss = [_dot(l2[i], s_bd[i]) for i in idx]
    g = [jnp.where(tri_mask,
                   _dot(jnp.concatenate([jnp.where(lo2, l2[i], 0.0), jnp.where(lo2, 0.0, l2[i])],
                                        axis=0),
                        jnp.concatenate([at(bt, c), at(kq, c)], axis=0), NT), 0.0)
         for i, c in zip(idx, chains)]
    xv = [_dot(jnp.concatenate([g[i][0:ch], g[i][2 * ch:3 * ch]], axis=0),
               jnp.concatenate([zeros, at(v, c)], axis=0))
          for i, c in zip(idx, chains)]
    heads = [(i, h) for i in idx for h in range(2)]
    n0 = [-g[i][2 * h * ch:(2 * h + 1) * ch] for i, h in heads]
    w = [_dot(n[:, :ch], jnp.where(lo, n, eye_right)) + jnp.where(lo, 0.0, eye_right) for n in n0]
    for _ in range(int(np.log2(ch)) - 1):
        w = [_dot(t[:, :ch], t) + jnp.where(lo, 0.0, t) for t in w]
    us = [_dot(w[2 * i + h],
               jnp.concatenate([zeros, ss[i][0:ch] + xv[i][h * ch:(h + 1) * ch]], axis=0))
          for i, h in heads]
    u = [jnp.where(lo, us[2 * i], us[2 * i + 1]) for i in idx]
    yv = [_dot(jnp.concatenate([g[i][ch:2 * ch], g[i][3 * ch:4 * ch]], axis=0),
               jnp.concatenate([-u[i], at(v, c)], axis=0))
          for i, c in zip(idx, chains)]
    upd = [_dot(jnp.concatenate([at(kh, c), at(bh, c)], axis=0),
                jnp.concatenate([at(v, c), -u[i]], axis=0), TN)
           for i, c in zip(idx, chains)]
    for i, (j, p) in zip(idx, chains):
        ps = slice(p * LANES, (p + 1) * LANES)
        y_ref[row_of[j], ps] = ss[i][ch:2 * ch] + jnp.where(lo, yv[i][0:ch], yv[i][ch:2 * ch])
        state_ref[j * npair + p] = w_end_t[j][ps, :] * s_bd[i] + jnp.where(block_diag, upd[i], 0.0)

    y = y_ref[...]
    d = y - _head_sum(y) * (1.0 / hd)
    var = _head_sum(d * d) * (1.0 / hd)
    yn = d * lax.rsqrt(var + RWKV_GN_EPS) * ln_w + ln_b
    yc_ref[...] = ((yn + bonus) * gate).astype(BF16).reshape(nbat, ch, C_WIDTH)


def _rwkv(rkv, lora, v_first, mu_rkv, vecs, w2, a2, g2, v2, batch, seq):
    ch = RWKV_CHUNK
    nbat = RWKV_BATCH_ROWS
    has_vres = v_first is not None
    per_seq = lambda t: t.reshape(batch, seq, t.shape[-1])
    blk = lambda width: pl.BlockSpec((nbat, ch, width), lambda i, c: (i, c, 0))
    ins = [per_seq(rkv), per_seq(lora)] + ([per_seq(v_first)] if has_vres else []) \
        + [mu_rkv, vecs, w2, a2, g2] + ([v2] if has_vres else [])
    specs = [blk(3 * C_WIDTH), blk(LORA_COLS)] + ([blk(C_WIDTH)] if has_vres else []) \
        + [_full((3, C_WIDTH)), _full((8, C_WIDTH)), _full((LANES, C_WIDTH)),
           _full((LANES, C_WIDTH)), _full((LANES, C_WIDTH))] \
        + ([_full((LANES, C_WIDTH))] if has_vres else [])
    yc, v_out = pl.pallas_call(
        functools.partial(_rwkv_kernel, has_vres=has_vres),
        grid=(batch // nbat, seq // ch),
        in_specs=specs,
        out_specs=[blk(C_WIDTH), blk(C_WIDTH)],
        out_shape=[jax.ShapeDtypeStruct((batch, seq, C_WIDTH), BF16),
                   jax.ShapeDtypeStruct((batch, seq, C_WIDTH), F32)],
        scratch_shapes=[pltpu.VMEM((nbat * C_HEADS // 2, LANES, LANES), F32),
                        pltpu.VMEM((8 * nbat, 3 * C_WIDTH), F32),
                        pltpu.VMEM((8 * nbat, LORA_HALF), F32),
                        pltpu.VMEM((nbat * ch, C_WIDTH), F32)],
        compiler_params=_params("parallel", "arbitrary"),
        name="rwkv7_chunked",
    )(*ins)
    return yc.reshape(batch * seq, C_WIDTH), v_out.reshape(batch * seq, C_WIDTH)


def _merge_kernel(x_ref, g_ref, ya_ref, o0_ref, l0_ref, o1_ref, l1_ref, o2_ref, l2_ref, yc_ref,
                  wg_ref, bg_ref, wa_ref, wb_ref, wc_ref, wo_ref, g2_ref, wr_ref, br_ref,
                  xm_ref, h2_ref, comb_ref, nat_ref):
    x = x_ref[...]
    tm = x.shape[0]
    hb = _rmsnorm(x, g_ref[...]).astype(BF16)

    def token_order(ref, slot):
        dil = ref.shape[1]
        if dil == 1:
            return ref[0, 0]
        chunks = []
        for c in range(B_OUT // LANES):
            for r in range(dil):
                nat_ref[slot, c, pl.ds(r, tm // dil, stride=dil), :] = \
                    ref[0, r, :, c * LANES:(c + 1) * LANES]
            chunks.append(nat_ref[slot, c])
        return jnp.concatenate(chunks, axis=-1)

    l0, l1, l2 = token_order(l0_ref, 0), token_order(l1_ref, 0), token_order(l2_ref, 1)
    lm = jnp.maximum(jnp.maximum(l0, l1), l2)
    e0, e1, e2 = jnp.exp(l0 - lm), jnp.exp(l1 - lm), jnp.exp(l2 - lm)
    o0, o1, o2 = token_order(o0_ref, 0), token_order(o1_ref, 2), token_order(o2_ref, 3)
    yb = ((e0 * o0 + e1 * o1 + e2 * o2) / (e0 + e1 + e2)).astype(BF16)

    merged = None
    for j, (y, w_ref) in enumerate(((ya_ref[...], wa_ref), (yb, wb_ref), (yc_ref[...], wc_ref))):
        cs = slice(j * D_MODEL, (j + 1) * D_MODEL)
        gate = _sigmoid(jnp.dot(hb, wg_ref[0, :, cs], preferred_element_type=F32) + bg_ref[:, cs])
        term = gate * jnp.dot(y, w_ref[0], preferred_element_type=F32)
        merged = term if merged is None else merged + term
    xm = x + jnp.dot(merged.astype(BF16), wo_ref[0], preferred_element_type=F32)
    xm_ref[...] = xm

    h2 = _rmsnorm(xm, g2_ref[...])
    h2_ref[...] = h2.astype(BF16)

    logits = _dot3(wr_ref[...], h2, NT) + br_ref[:, 0:1]
    gl = [logits[g:g + 1, :] for g in range(N_GROUPS)]
    gmax = functools.reduce(jnp.maximum, gl)
    gsum = functools.reduce(lambda s, t: s + t, [jnp.exp(t - gmax) for t in gl])
    gp = 1.0 / gsum
    taken = jnp.zeros_like(gmax, dtype=jnp.bool_)
    sel = []
    for g in range(N_GROUPS):
        pick = (gl[g] == gmax) & jnp.logical_not(taken)
        sel.append(pick)
        taken = taken | pick
    el = []
    for e in range(EXPERTS_PER_GROUP):
        acc = jnp.zeros_like(gmax)
        for g in range(N_GROUPS):
            r0 = N_GROUPS + g * EXPERTS_PER_GROUP + e
            acc = jnp.where(sel[g], logits[r0:r0 + 1, :], acc)
        el.append(acc)
    emax = functools.reduce(jnp.maximum, el)
    ex = [jnp.exp(t - emax) for t in el]
    esum = functools.reduce(lambda s, t: s + t, ex)
    p = [t / esum for t in ex]
    p1 = functools.reduce(jnp.maximum, p)
    taken = jnp.zeros_like(gmax, dtype=jnp.bool_)
    first = []
    for e in range(EXPERTS_PER_GROUP):
        pick = (p[e] == p1) & jnp.logical_not(taken)
        first.append(pick)
        taken = taken | pick
    rest = [jnp.where(first[e], -1.0, p[e]) for e in range(EXPERTS_PER_GROUP)]
    p2 = functools.reduce(jnp.maximum, rest)
    taken = jnp.zeros_like(gmax, dtype=jnp.bool_)
    second = []
    for e in range(EXPERTS_PER_GROUP):
        pick = (rest[e] == p2) & jnp.logical_not(taken)
        second.append(pick)
        taken = taken | pick
    denom = p1 + p2
    for e in range(EXPERTS_PER_GROUP):
        in_group = jnp.where(first[e], p1 / denom, jnp.where(second[e], p2 / denom, 0.0))
        for g in range(N_GROUPS):
            r0 = g * EXPERTS_PER_GROUP + e
            comb_ref[r0:r0 + 1, :] = jnp.where(sel[g], gp * in_group, 0.0)


def _merge(x2, g, ya, attn, yc, layer, wg, bg, wa, wb, wc, wo, g2, wr, br, seq):
    n = x2.shape[0]
    tm = MERGE_ROW_TILE
    row = lambda width: pl.BlockSpec((tm, width), lambda i: (i, 0))
    of_layer = lambda rows, cols: pl.BlockSpec((1, rows, cols), lambda i: (layer, 0, 0))
    attn_flat = [t for pair in attn for t in pair]
    attn_specs = [_residue_spec(tm, t.shape[1], B_OUT, seq // tm) for t in attn_flat]
    return pl.pallas_call(
        _merge_kernel,
        grid=(n // tm,),
        in_specs=[row(D_MODEL), _full((1, D_MODEL)), row(A_WIDTH)] + attn_specs
        + [row(C_WIDTH), of_layer(D_MODEL, 3 * D_MODEL), _full((1, 3 * D_MODEL)),
           of_layer(A_WIDTH, D_MODEL), of_layer(B_OUT, D_MODEL), of_layer(C_WIDTH, D_MODEL),
           of_layer(D_MODEL, D_MODEL), _full((1, D_MODEL)), _full((32, D_MODEL)), _full((32, LANES))],
        out_specs=[row(D_MODEL), row(D_MODEL), pl.BlockSpec((N_EXPERTS, tm), lambda i: (0, i))],
        out_shape=[jax.ShapeDtypeStruct((n, D_MODEL), F32),
                   jax.ShapeDtypeStruct((n, D_MODEL), BF16),
                   jax.ShapeDtypeStruct((N_EXPERTS, n), F32)],
        scratch_shapes=[pltpu.VMEM((4, B_OUT // LANES, tm, LANES), F32)],
        compiler_params=_params("parallel"),
        name="merge_router",
    )(x2, g, ya, *attn_flat, yc, wg, bg, wa, wb, wc, wo, g2, wr, br)


def _moe_kernel(xm_ref, h2_ref, comb_ref, wg_ref, wu_ref, wd_ref, gf_ref, out_ref, *, final_norm):
    e = pl.program_id(1)

    @pl.when(e == 0)
    def _():
        out_ref[...] = xm_ref[...]

    h2 = h2_ref[...]
    gate = jnp.dot(h2, wg_ref[0, 0].astype(BF16), preferred_element_type=F32)
    up = jnp.dot(h2, wu_ref[0, 0].astype(BF16), preferred_element_type=F32)
    lane = lax.broadcasted_iota(jnp.int32, comb_ref.shape, 1)
    cw = jnp.sum(jnp.where(lane == e, comb_ref[...], 0.0), axis=-1, keepdims=True)
    hid = gate * _sigmoid(gate) * up * cw
    out_ref[...] += jnp.dot(hid.astype(BF16), wd_ref[0, 0].astype(BF16),
                            preferred_element_type=F32)

    if final_norm:
        @pl.when(e == N_EXPERTS - 1)
        def _():
            out_ref[...] = _rmsnorm(out_ref[...], gf_ref[...])


def _moe(xm, h2, comb, layer, wg, wu, wd, g_final, final_norm):
    n = xm.shape[0]
    tm = MOE_ROW_TILE
    return pl.pallas_call(
        functools.partial(_moe_kernel, final_norm=final_norm),
        grid=(n // tm, N_EXPERTS),
        in_specs=[pl.BlockSpec((tm, D_MODEL), lambda i, e: (i, 0)),
                  pl.BlockSpec((tm, D_MODEL), lambda i, e: (i, 0)),
                  pl.BlockSpec((tm, N_EXPERTS), lambda i, e: (i, 0)),
                  pl.BlockSpec((1, 1, D_MODEL, EXPERT_FF), lambda i, e: (layer, e, 0, 0)),
                  pl.BlockSpec((1, 1, D_MODEL, EXPERT_FF), lambda i, e: (layer, e, 0, 0)),
                  pl.BlockSpec((1, 1, EXPERT_FF, D_MODEL), lambda i, e: (layer, e, 0, 0)),
                  _full((1, D_MODEL))],
        out_specs=pl.BlockSpec((tm, D_MODEL), lambda i, e: (i, 0)),
        out_shape=jax.ShapeDtypeStruct((n, D_MODEL), F32),
        compiler_params=_params("parallel", "arbitrary"),
        name="moe_ffn",
    )(xm, h2, comb, wg, wu, wd, g_final)


def _pack_lora(mu_wag, w1, a1, g1, mu_v, v1):
    d = D_MODEL
    if v1 is None:
        mu_v = jnp.zeros((d,), F32)
        v1 = jnp.zeros((d, V_LORA), F32)
    pad = jnp.zeros((d, LORA_HALF - (W_LORA + A_LORA + G_LORA + V_LORA)), F32)
    mus = (mu_wag[0], mu_wag[1], mu_wag[2], mu_v)
    ws = (w1, a1, g1, v1)
    cur = [w * (1.0 - m)[:, None] for w, m in zip(ws, mus)]
    prev = [w * m[:, None] for w, m in zip(ws, mus)]
    return jnp.concatenate(cur + [pad] + prev + [pad], axis=1)


def _pad_rows(w, lo, total=LANES):
    return jnp.zeros((total, w.shape[1]), F32).at[lo:lo + w.shape[0]].set(w)


def kernel(x, positions, norm_mix_g, w_in, gmlp_ln_g, gmlp_ln_b, gmlp_ws, gmlp_bs, rwkv_mu_rkv, rwkv_mu_wag, rwkv_w0, rwkv_w1, rwkv_w2, rwkv_a0, rwkv_a1, rwkv_a2, rwkv_g1, rwkv_g2, rwkv_k_k, rwkv_k_a, rwkv_r_k, rwkv_ln_w, rwkv_ln_b, rwkv_mu_v, rwkv_v0, rwkv_v1, rwkv_v2, w_branch_a, w_branch_b, w_branch_c, w_gate, b_gate, w_out, norm_ffn_g, router_group_w, router_group_b, router_expert_w, router_expert_b, expert_w_gate, expert_w_up, expert_w_down, final_norm_g):
    batch, seq, d = x.shape
    depth = w_in.shape[0]
    n = batch * seq
    x2 = x.reshape(n, d)
    rope = _rope_tables(positions)
    w_in_b, w_gate_b, w_out_b = w_in.astype(BF16), w_gate.astype(BF16), w_out.astype(BF16)
    wba, wbb, wbc = w_branch_a.astype(BF16), w_branch_b.astype(BF16), w_branch_c.astype(BF16)
    v_first = None
    for l in range(depth):
        vres = l > 0
        lora_w = _pack_lora(rwkv_mu_wag[l], rwkv_w1[l], rwkv_a1[l], rwkv_g1[l],
                            rwkv_mu_v[l - 1] if vres else None, rwkv_v1[l - 1] if vres else None)
        ya, qkv0, qkv1, qkv2, rkv, lora = _inproj(
            x2, norm_mix_g[l][None], w_in_b, l, lora_w.astype(BF16), gmlp_ln_g[l][None],
            gmlp_ln_b[l][None], gmlp_ws[l], gmlp_bs[l].T, rope, batch, seq)

        attn = [_dilated_attention(qkv, dil)
                for qkv, (_, dil) in zip((qkv0, qkv1, qkv2), DILATED_PATTERNS)]

        zero = jnp.zeros((C_WIDTH,), F32)
        vecs = jnp.stack([rwkv_w0[l], rwkv_a0[l], rwkv_v0[l - 1] if vres else zero, rwkv_k_k[l],
                          rwkv_k_a[l], rwkv_r_k[l].reshape(C_WIDTH), rwkv_ln_w[l], rwkv_ln_b[l]])
        yc, v_c = _rwkv(rkv, lora, v_first, rwkv_mu_rkv[l], vecs,
                        _pad_rows(rwkv_w2[l], 0), _pad_rows(rwkv_a2[l], W_LORA), rwkv_g2[l],
                        _pad_rows(rwkv_v2[l - 1], 0) if vres else None, batch, seq)
        if l == 0:
            v_first = v_c

        wr = jnp.concatenate([router_group_w[l], router_expert_w[l]], axis=1).T
        wr = jnp.zeros((32, d), F32).at[:wr.shape[0]].set(wr)
        br = jnp.concatenate([router_group_b[l], router_expert_b[l]])
        br = jnp.zeros((32, LANES), F32).at[:br.shape[0], 0].set(br)
        xm, h2, comb_t = _merge(
            x2, norm_mix_g[l][None], ya, attn, yc, l, w_gate_b, b_gate[l][None], wba, wbb, wbc,
            w_out_b, norm_ffn_g[l][None], wr, br, seq)

        x2 = _moe(xm, h2, comb_t.T, l, expert_w_gate, expert_w_up, expert_w_down,
                  final_norm_g[None], final_norm=(l == depth - 1))
    return x2.reshape(batch, seq, d)
```

```python
import functools

import numpy as np
import jax
import jax.numpy as jnp
from jax import lax
from jax.experimental import pallas as pl
from jax.experimental.pallas import tpu as pltpu

F32 = jnp.float32
BF16 = jnp.bfloat16
HIGHEST = lax.Precision.HIGHEST

D_MODEL = 1024
HEAD_DIM = 64
A_GROUPS = 4
A_WIDTH = 512
CHUNK = 128
DILATED_PATTERNS = ((128, 1), (512, 4), (2048, 16))
B_HEADS_PER_GROUP = 4
B_WIDTH = 768
B_OUT = 256
BLOCK = 128
ROPE_THETA = 500000.0
ROPE_DIMS = 16
C_HEADS = 8
C_WIDTH = 512
W_LORA, A_LORA, V_LORA, G_LORA = 64, 64, 32, 128
RWKV_GN_EPS = HEAD_DIM * 1e-5
IN_COLS = 2 * A_WIDTH + 3 * B_WIDTH + 3 * C_WIDTH
LORA_HALF = 384
LORA_COLS = 2 * LORA_HALF
N_GROUPS = 4
EXPERTS_PER_GROUP = 4
N_EXPERTS = 16
EXPERT_FF = 512
RMS_EPS = 1e-6
LN_EPS = 1e-5

LANES = 128
ROW_TILE = 512
MERGE_ROW_TILE = 512
RWKV_CHUNK = 64
RWKV_BATCH_ROWS = 2
MOE_ROW_TILE = 1024
MOE_CAP = 320
ROUTE_ROWS = 24
VMEM_LIMIT = 56 * 1024 * 1024
NEG_BIG = -1e30


def _params(*sem):
    return pltpu.CompilerParams(dimension_semantics=sem, vmem_limit_bytes=VMEM_LIMIT)


def _full(shape):
    nd = len(shape)
    return pl.BlockSpec(shape, lambda *_: (0,) * nd)


def _gelu_tanh(x):
    return 0.5 * x * (1.0 + jnp.tanh(0.7978845608028654 * (x + 0.044715 * (x * x * x))))


def _sigmoid(x):
    return 1.0 / (1.0 + jnp.exp(-x))


def _rmsnorm(x, g):
    return x * lax.rsqrt(jnp.mean(x * x, axis=-1, keepdims=True) + RMS_EPS) * g


def _rope_table_kernel(pos_ref, consts_ref, c_ref, sa_ref, sb_ref):
    ang = pos_ref[...].astype(F32) * consts_ref[0:1, :]
    c_ref[...] = jnp.cos(ang)
    s = jnp.sin(ang)
    sa_ref[...] = -s * consts_ref[1:2, :]
    sb_ref[...] = s * consts_ref[2:3, :]


def _rope_tables(positions):
    n = positions.size
    half = ROPE_DIMS // 2
    inv_freq = ROPE_THETA ** (-jnp.arange(half, dtype=F32) / half)
    lane = np.arange(LANES) % HEAD_DIM
    in_rope = lane < ROPE_DIMS
    invf_row = jnp.where(in_rope, inv_freq[lane % half], 0.0)
    consts = jnp.zeros((8, LANES), F32)
    consts = consts.at[0].set(invf_row)
    consts = consts.at[1].set(jnp.asarray(lane < half, F32))
    consts = consts.at[2].set(jnp.asarray((lane >= half) & in_rope, F32))
    tm = 1024
    out = jax.ShapeDtypeStruct((n, LANES), F32)
    return pl.pallas_call(
        _rope_table_kernel,
        grid=(n // tm,),
        in_specs=[pl.BlockSpec((tm, 1), lambda i: (i, 0)), _full((8, LANES))],
        out_specs=[pl.BlockSpec((tm, LANES), lambda i: (i, 0))] * 3,
        out_shape=[out, out, out],
        compiler_params=_params("parallel"),
        name="rope_tables",
    )(positions.reshape(n, 1), consts)


def _inproj_kernel(x_ref, g_ref, w_ref, wl_ref, lng_ref, lnb_ref, ws_ref, bst_ref, c_ref, sa_ref,
                   sb_ref, ya_ref, qkv0_ref, qkv1_ref, qkv2_ref, rkv_ref, lora_ref, nat_ref):
    tm = x_ref.shape[0]
    hb = _rmsnorm(x_ref[...], g_ref[...]).astype(BF16)

    def proj(lo, width):
        return jnp.dot(hb, w_ref[0, :, lo:lo + width], preferred_element_type=F32)

    uv = proj(0, 2 * A_WIDTH)
    u = _gelu_tanh(uv[:, :A_WIDTH])
    v = _gelu_tanh(uv[:, A_WIDTH:])
    mu = jnp.mean(v, axis=-1, keepdims=True)
    d = v - mu
    var = jnp.mean(d * d, axis=-1, keepdims=True)
    vn = (d * lax.rsqrt(var + LN_EPS) * lng_ref[...] + lnb_ref[...]).astype(BF16)
    row = lax.broadcasted_iota(jnp.int32, (CHUNK, CHUNK), 0)
    col = lax.broadcasted_iota(jnp.int32, (CHUNK, CHUNK), 1)
    for g in range(A_GROUPS):
        wg = jnp.where(row >= col, ws_ref[g], 0.0).astype(BF16)
        bias = bst_ref[:, g:g + 1]
        cs = slice(g * LANES, (g + 1) * LANES)
        for c in range(tm // CHUNK):
            rs = slice(c * CHUNK, (c + 1) * CHUNK)
            mixed = jnp.dot(wg, vn[rs, cs], preferred_element_type=F32) + bias
            ya_ref[rs, cs] = (u[rs, cs] * mixed).astype(BF16)

    cos, sa, sb = c_ref[...], sa_ref[...], sb_ref[...]
    q_off = 2 * A_WIDTH
    for grp, (qkv_ref, (_, dil)) in enumerate(zip((qkv0_ref, qkv1_ref, qkv2_ref), DILATED_PATTERNS)):
        acc = [proj(q_off + part * B_WIDTH + grp * B_OUT, B_OUT) for part in range(3)]
        for c in range(B_WIDTH // LANES):
            t = acc[c * LANES // B_OUT][:, c * LANES % B_OUT:c * LANES % B_OUT + LANES]
            if c < 2 * B_OUT // LANES:
                t = t * cos + pltpu.roll(t, LANES - 8, 1) * sa + pltpu.roll(t, 8, 1) * sb
            if c < B_OUT // LANES:
                t = t * (HEAD_DIM ** -0.5)
            cs = slice(c * LANES, (c + 1) * LANES)
            if dil == 1:
                qkv_ref[0, 0, :, cs] = t.astype(BF16)
            else:
                nat_ref[c] = t
                for r in range(dil):
                    qkv_ref[0, r, :, cs] = nat_ref[c, pl.ds(r, tm // dil, stride=dil), :].astype(BF16)

    rkv_ref[...] = proj(q_off + 3 * B_WIDTH, 3 * C_WIDTH)
    lora_ref[...] = jnp.dot(hb, wl_ref[...], preferred_element_type=F32)


def _residue_spec(tm, dil, width, tiles_per_seq):
    return pl.BlockSpec((1, dil, tm // dil, width),
                        lambda i: (i // tiles_per_seq, 0, i % tiles_per_seq, 0))


def _inproj(x2, g, w_in, layer, w_lora, ln_g, ln_b, ws, bs_t, rope, batch, seq):
    n = x2.shape[0]
    tm = ROW_TILE
    row = lambda width: pl.BlockSpec((tm, width), lambda i: (i, 0))
    dils = [d for _, d in DILATED_PATTERNS]
    return pl.pallas_call(
        _inproj_kernel,
        grid=(n // tm,),
        in_specs=[row(D_MODEL), _full((1, D_MODEL)),
                  pl.BlockSpec((1, D_MODEL, IN_COLS), lambda i: (layer, 0, 0)),
                  _full((D_MODEL, LORA_COLS)),
                  _full((1, A_WIDTH)), _full((1, A_WIDTH)), _full((A_GROUPS, CHUNK, CHUNK)),
                  _full((CHUNK, A_GROUPS)), row(LANES), row(LANES), row(LANES)],
        out_specs=[row(A_WIDTH)] + [_residue_spec(tm, d, B_WIDTH, seq // tm) for d in dils]
        + [row(3 * C_WIDTH), row(LORA_COLS)],
        out_shape=[jax.ShapeDtypeStruct((n, A_WIDTH), BF16)]
        + [jax.ShapeDtypeStruct((batch, d, seq // d, B_WIDTH), BF16) for d in dils]
        + [jax.ShapeDtypeStruct((n, 3 * C_WIDTH), F32),
           jax.ShapeDtypeStruct((n, LORA_COLS), F32)],
        scratch_shapes=[pltpu.VMEM((B_WIDTH // LANES, tm, LANES), F32)],
        compiler_params=_params("parallel"),
        name="inproj",
    )(x2, g, w_in, w_lora, ln_g, ln_b, ws, bs_t, *rope)


def _attn_block(q, kw, vw, valid):
    lo = lax.broadcasted_iota(jnp.int32, (BLOCK, LANES), 1) < HEAD_DIM
    pairs = range(B_OUT // LANES)
    scores = []
    for p in pairs:
        ps = slice(p * LANES, (p + 1) * LANES)
        qp = q[:, ps]
        zero = jnp.zeros_like(qp)
        for part in (jnp.where(lo, qp, zero), jnp.where(lo, zero, qp)):
            s = lax.dot_general(part, kw[:, ps], NT, preferred_element_type=F32)
            scores.append(jnp.where(valid, s, NEG_BIG))
    m = [jnp.max(s, axis=-1, keepdims=True) for s in scores]
    e = [jnp.exp(s - mx) for s, mx in zip(scores, m)]
    l = [jnp.sum(t, axis=-1, keepdims=True) for t in e]
    pv = [jnp.dot(t.astype(BF16), vw[:, (h // 2) * LANES:(h // 2 + 1) * LANES],
                  preferred_element_type=F32) for h, t in enumerate(e)]
    out = [jnp.where(lo, pv[2 * p] / l[2 * p], pv[2 * p + 1] / l[2 * p + 1]) for p in pairs]
    lse = [mx + jnp.log(t) for mx, t in zip(m, l)]
    lse = [jnp.where(lo, lse[2 * p], lse[2 * p + 1]) for p in pairs]
    return jnp.concatenate(out, axis=-1), jnp.concatenate(lse, axis=-1)


def _attn_kernel(qkv_ref, o_ref, l_ref):
    dil, sub = qkv_ref.shape[1], qkv_ref.shape[2]
    nb = sub // BLOCK
    qs, ks, vs = (slice(i * B_OUT, (i + 1) * B_OUT) for i in range(3))
    qi = lax.broadcasted_iota(jnp.int32, (BLOCK, BLOCK), 0)
    kj = lax.broadcasted_iota(jnp.int32, (BLOCK, BLOCK), 1)
    qi2 = lax.broadcasted_iota(jnp.int32, (BLOCK, 2 * BLOCK), 0)
    kj2 = lax.broadcasted_iota(jnp.int32, (BLOCK, 2 * BLOCK), 1)
    valid2 = (kj2 >= qi2) & (kj2 <= qi2 + BLOCK)

    def residue(r):
        first = pl.ds(0, BLOCK)
        o, l = _attn_block(qkv_ref[0, r, first, qs], qkv_ref[0, r, first, ks],
                           qkv_ref[0, r, first, vs], kj <= qi)
        o_ref[0, r, first, :] = o
        l_ref[0, r, first, :] = l

        def body(n, carry):
            cur = pl.ds(pl.multiple_of(n * BLOCK, BLOCK), BLOCK)
            win = pl.ds(pl.multiple_of((n - 1) * BLOCK, BLOCK), 2 * BLOCK)
            o, l = _attn_block(qkv_ref[0, r, cur, qs], qkv_ref[0, r, win, ks],
                               qkv_ref[0, r, win, vs], valid2)
            o_ref[0, r, cur, :] = o
            l_ref[0, r, cur, :] = l
            return carry

        if nb > 1:
            lax.fori_loop(1, nb, body, 0)

    if dil <= 4:
        for r in range(dil):
            residue(r)
    else:
        def rbody(r, carry):
            residue(r)
            return carry
        lax.fori_loop(0, dil, rbody, 0)


def _dilated_attention(qkv, dilation):
    batch, _, sub, _ = qkv.shape
    ospec = pl.BlockSpec((1, dilation, sub, B_OUT), lambda b: (b, 0, 0, 0))
    oshape = jax.ShapeDtypeStruct((batch, dilation, sub, B_OUT), F32)
    return pl.pallas_call(
        _attn_kernel,
        grid=(batch,),
        in_specs=[pl.BlockSpec((1, dilation, sub, B_WIDTH), lambda b: (b, 0, 0, 0))],
        out_specs=[ospec, ospec],
        out_shape=[oshape, oshape],
        compiler_params=_params("parallel"),
        name=f"dilated_attn_d{dilation}",
    )(qkv)


def _shift_rows(x, prev_row):
    rolled = pltpu.roll(x, 1, 0)
    first = lax.broadcasted_iota(jnp.int32, x.shape, 0) == 0
    return jnp.where(first, prev_row, rolled)


NN = (((1,), (0,)), ((), ()))
NT = (((1,), (1,)), ((), ()))
TN = (((0,), (0,)), ((), ()))


def _dot(a, b, dims=NN):
    return lax.dot_general(a.astype(BF16), b.astype(BF16), dims, preferred_element_type=F32)


def _split(a, pieces):
    out = []
    for _ in range(pieces):
        p = a.astype(BF16)
        out.append(p)
        a = a - p.astype(F32)
    return out


def _dot3(a, b, dims=NN):
    (ah, al), (bh, bl) = _split(a, 2), _split(b, 2)
    return _dot(ah, bh, dims) + _dot(ah, bl, dims) + _dot(al, bh, dims)


def _head_sum(x):
    lo = lax.broadcasted_iota(jnp.int32, (x.shape[0], LANES), 1) < HEAD_DIM
    outs = []
    for c in range(x.shape[1] // LANES):
        xc = x[:, c * LANES:(c + 1) * LANES]
        s_lo = jnp.sum(jnp.where(lo, xc, 0.0), axis=-1, keepdims=True)
        s_hi = jnp.sum(jnp.where(lo, 0.0, xc), axis=-1, keepdims=True)
        outs.append(jnp.where(lo, s_lo, s_hi))
    return jnp.concatenate(outs, axis=-1)


def _rwkv_kernel(*refs, has_vres):
    if has_vres:
        (rkv_ref, lora_ref, vf_ref, mu_ref, vec_ref, w2_ref, a2_ref, g2_ref, v2_ref,
         yc_ref, vout_ref, state_ref, prkv_ref, plora_ref, y_ref) = refs
    else:
        (rkv_ref, lora_ref, mu_ref, vec_ref, w2_ref, a2_ref, g2_ref,
         yc_ref, vout_ref, state_ref, prkv_ref, plora_ref, y_ref) = refs
    nbat, ch = rkv_ref.shape[0], rkv_ref.shape[1]
    rows = nbat * ch
    hd = HEAD_DIM
    row_of = [slice(j * ch, (j + 1) * ch) for j in range(nbat)]

    @pl.when(pl.program_id(1) == 0)
    def _():
        state_ref[...] = jnp.zeros_like(state_ref)
        prkv_ref[...] = jnp.zeros_like(prkv_ref)
        plora_ref[...] = jnp.zeros_like(plora_ref)

    def shifted(x, prev_ref):
        parts = []
        for j, rs in enumerate(row_of):
            parts.append(_shift_rows(x[rs], prev_ref[8 * j:8 * j + 1, :]))
            prev_ref[8 * j:8 * j + 1, :] = x[(j + 1) * ch - 1:(j + 1) * ch, :]
        return jnp.concatenate(parts, axis=0)

    rkv = rkv_ref[...].reshape(rows, 3 * C_WIDTH)
    rkv_prev = shifted(rkv, prkv_ref)
    lora_all = lora_ref[...].reshape(rows, LORA_COLS)
    lora = lora_all[:, :LORA_HALF] + shifted(lora_all[:, LORA_HALF:], plora_ref)

    w0, a0, v0 = vec_ref[0:1, :], vec_ref[1:2, :], vec_ref[2:3, :]
    k_k, k_a, r_k = vec_ref[3:4, :], vec_ref[4:5, :], vec_ref[5:6, :]
    ln_w, ln_b = vec_ref[6:7, :], vec_ref[7:8, :]

    def mix(i):
        cur = rkv[:, i * C_WIDTH:(i + 1) * C_WIDTH]
        prev = rkv_prev[:, i * C_WIDTH:(i + 1) * C_WIDTH]
        return cur + (prev - cur) * mu_ref[i:i + 1, :]

    r, k, v = mix(0), mix(1), mix(2)
    wa = lora[:, 0:LANES]
    z = w0 + _dot3(jnp.tanh(wa), w2_ref[...])
    log_decay = -float(np.exp(-0.5)) * _sigmoid(z)
    a = _sigmoid(a0 + _dot(wa, a2_ref[...]))
    gate = _dot(_sigmoid(lora[:, LANES:2 * LANES]), g2_ref[...])
    if has_vres:
        vmix = _sigmoid(v0 + _dot(lora[:, 2 * LANES:3 * LANES], v2_ref[...]))
        v = v + (vf_ref[...].reshape(rows, C_WIDTH) - v) * vmix
    vout_ref[...] = v.reshape(nbat, ch, C_WIDTH)

    kk = k * k_k
    kk = kk * lax.rsqrt(jnp.maximum(_head_sum(kk * kk), 1e-24))
    k = k * (1.0 + (a - 1.0) * k_a)
    bonus = _head_sum(r * k * r_k) * v
    b = kk * a

    ti = lax.broadcasted_iota(jnp.int32, (rows, rows), 0)
    si = lax.broadcasted_iota(jnp.int32, (rows, rows), 1)
    tri = ((si <= ti) & (si // ch == ti // ch)).astype(BF16)
    cum = functools.reduce(lambda s, t: s + t, [_dot(tri, p) for p in _split(log_decay, 3)])
    ends = [cum[(j + 1) * ch - 1:(j + 1) * ch, :] for j in range(nbat)]
    cum_end = jnp.concatenate([jnp.broadcast_to(e, (ch, C_WIDTH)) for e in ends], axis=0)
    e_neg = jnp.exp(-cum)
    e_end = jnp.exp(cum_end - cum)
    kt = kk * jnp.exp(cum - log_decay)
    rt = r * jnp.exp(cum)
    bt = b * e_neg
    kq = k * e_neg
    kh = k * e_end
    bh = b * e_end
    unit = (lax.broadcasted_iota(jnp.int32, (8, LANES), 0) == 0).astype(BF16)
    w_end_t = [functools.reduce(lambda s, t: s + t,
                                [_dot(jnp.broadcast_to(p, (8, C_WIDTH)), unit, TN)
                                 for p in _split(jnp.exp(e), 2)])
               for e in ends]

    lane = lax.broadcasted_iota(jnp.int32, (ch, LANES), 1)
    lo = lane < hd
    lo2 = lax.broadcasted_iota(jnp.int32, (2 * ch, LANES), 1) < hd
    row4 = lax.broadcasted_iota(jnp.int32, (4 * ch, LANES), 0)
    col4 = lax.broadcasted_iota(jnp.int32, (4 * ch, LANES), 1) % ch
    t4 = row4 % ch
    tri_mask = (col4 < t4) | ((col4 == t4) & ((row4 // ch) % 2 == 1))
    eye_right = (lane - hd == lax.broadcasted_iota(jnp.int32, (ch, LANES), 0)).astype(F32)
    brow = lax.broadcasted_iota(jnp.int32, (LANES, LANES), 0) // hd
    bcol = lax.broadcasted_iota(jnp.int32, (LANES, LANES), 1) // hd
    block_diag = brow == bcol
    zeros = jnp.zeros((ch, LANES), F32)

    npair = C_HEADS // 2
    chains = [(j, p) for j in range(nbat) for p in range(npair)]
    at = lambda t, c: t[row_of[c[0]], c[1] * LANES:(c[1] + 1) * LANES]
    idx = range(len(chains))
    s_bd = [state_ref[j * npair + p] for j, p in chains]
    l2 = [jnp.concatenate([at(kt, c), at(rt, c)], axis=0) for c in chains]
    ss = [_dot(l2[i], s_bd[i]) for i in idx]
    g = [jnp.where(tri_mask,
                   _dot(jnp.concatenate([jnp.where(lo2, l2[i], 0.0), jnp.where(lo2, 0.0, l2[i])],
                                        axis=0),
                        jnp.concatenate([at(bt, c), at(kq, c)], axis=0), NT), 0.0)
         for i, c in zip(idx, chains)]
    xv = [_dot(jnp.concatenate([g[i][0:ch], g[i][2 * ch:3 * ch]], axis=0),
               jnp.concatenate([zeros, at(v, c)], axis=0))
          for i, c in zip(idx, chains)]
    heads = [(i, h) for i in idx for h in range(2)]
    n0 = [-g[i][2 * h * ch:(2 * h + 1) * ch] for i, h in heads]
    w = [_dot(n[:, :ch], jnp.where(lo, n, eye_right)) + jnp.where(lo, 0.0, eye_right) for n in n0]
    for _ in range(int(np.log2(ch)) - 1):
        w = [_dot(t[:, :ch], t) + jnp.where(lo, 0.0, t) for t in w]
    us = [_dot(w[2 * i + h],
               jnp.concatenate([zeros, ss[i][0:ch] + xv[i][h * ch:(h + 1) * ch]], axis=0))
          for i, h in heads]
    u = [jnp.where(lo, us[2 * i], us[2 * i + 1]) for i in idx]
    yv = [_dot(jnp.concatenate([g[i][ch:2 * ch], g[i][3 * ch:4 * ch]], axis=0),
               jnp.concatenate([-u[i], at(v, c)], axis=0))
          for i, c in zip(idx, chains)]
    upd = [_dot(jnp.concatenate([at(kh, c), at(bh, c)], axis=0),
                jnp.concatenate([at(v, c), -u[i]], axis=0), TN)
           for i, c in zip(idx, chains)]
    for i, (j, p) in zip(idx, chains):
        ps = slice(p * LANES, (p + 1) * LANES)
        y_ref[row_of[j], ps] = ss[i][ch:2 * ch] + jnp.where(lo, yv[i][0:ch], yv[i][ch:2 * ch])
        state_ref[j * npair + p] = w_end_t[j][ps, :] * s_bd[i] + jnp.where(block_diag, upd[i], 0.0)

    y = y_ref[...]
    d = y - _head_sum(y) * (1.0 / hd)
    var = _head_sum(d * d) * (1.0 / hd)
    yn = d * lax.rsqrt(var + RWKV_GN_EPS) * ln_w + ln_b
    yc_ref[...] = ((yn + bonus) * gate).astype(BF16).reshape(nbat, ch, C_WIDTH)


def _rwkv(rkv, lora, v_first, mu_rkv, vecs, w2, a2, g2, v2, batch, seq):
    ch = RWKV_CHUNK
    nbat = RWKV_BATCH_ROWS
    has_vres = v_first is not None
    per_seq = lambda t: t.reshape(batch, seq, t.shape[-1])
    blk = lambda width: pl.BlockSpec((nbat, ch, width), lambda i, c: (i, c, 0))
    ins = [per_seq(rkv), per_seq(lora)] + ([per_seq(v_first)] if has_vres else []) \
        + [mu_rkv, vecs, w2, a2, g2] + ([v2] if has_vres else [])
    specs = [blk(3 * C_WIDTH), blk(LORA_COLS)] + ([blk(C_WIDTH)] if has_vres else []) \
        + [_full((3, C_WIDTH)), _full((8, C_WIDTH)), _full((LANES, C_WIDTH)),
           _full((LANES, C_WIDTH)), _full((LANES, C_WIDTH))] \
        + ([_full((LANES, C_WIDTH))] if has_vres else [])
    yc, v_out = pl.pallas_call(
        functools.partial(_rwkv_kernel, has_vres=has_vres),
        grid=(batch // nbat, seq // ch),
        in_specs=specs,
        out_specs=[blk(C_WIDTH), blk(C_WIDTH)],
        out_shape=[jax.ShapeDtypeStruct((batch, seq, C_WIDTH), BF16),
                   jax.ShapeDtypeStruct((batch, seq, C_WIDTH), F32)],
        scratch_shapes=[pltpu.VMEM((nbat * C_HEADS // 2, LANES, LANES), F32),
                        pltpu.VMEM((8 * nbat, 3 * C_WIDTH), F32),
                        pltpu.VMEM((8 * nbat, LORA_HALF), F32),
                        pltpu.VMEM((nbat * ch, C_WIDTH), F32)],
        compiler_params=_params("parallel", "arbitrary"),
        name="rwkv7_chunked",
    )(*ins)
    return yc.reshape(batch * seq, C_WIDTH), v_out.reshape(batch * seq, C_WIDTH)


def _merge_kernel(x_ref, g_ref, ya_ref, o0_ref, l0_ref, o1_ref, l1_ref, o2_ref, l2_ref, yc_ref,
                  wg_ref, bg_ref, wa_ref, wb_ref, wc_ref, wo_ref, g2_ref, wr_ref, br_ref,
                  xm_ref, h2_ref, comb_ref, nat_ref):
    x = x_ref[...]
    tm = x.shape[0]
    hb = _rmsnorm(x, g_ref[...]).astype(BF16)

    def token_order(ref, slot):
        dil = ref.shape[1]
        if dil == 1:
            return ref[0, 0]
        chunks = []
        for c in range(B_OUT // LANES):
            for r in range(dil):
                nat_ref[slot, c, pl.ds(r, tm // dil, stride=dil), :] = \
                    ref[0, r, :, c * LANES:(c + 1) * LANES]
            chunks.append(nat_ref[slot, c])
        return jnp.concatenate(chunks, axis=-1)

    l0, l1, l2 = token_order(l0_ref, 0), token_order(l1_ref, 0), token_order(l2_ref, 1)
    lm = jnp.maximum(jnp.maximum(l0, l1), l2)
    e0, e1, e2 = jnp.exp(l0 - lm), jnp.exp(l1 - lm), jnp.exp(l2 - lm)
    o0, o1, o2 = token_order(o0_ref, 0), token_order(o1_ref, 2), token_order(o2_ref, 3)
    yb = ((e0 * o0 + e1 * o1 + e2 * o2) / (e0 + e1 + e2)).astype(BF16)

    merged = None
    for j, (y, w_ref) in enumerate(((ya_ref[...], wa_ref), (yb, wb_ref), (yc_ref[...], wc_ref))):
        cs = slice(j * D_MODEL, (j + 1) * D_MODEL)
        gate = _sigmoid(jnp.dot(hb, wg_ref[0, :, cs], preferred_element_type=F32) + bg_ref[:, cs])
        term = gate * jnp.dot(y, w_ref[0], preferred_element_type=F32)
        merged = term if merged is None else merged + term
    xm = x + jnp.dot(merged.astype(BF16), wo_ref[0], preferred_element_type=F32)
    xm_ref[...] = xm

    h2 = _rmsnorm(xm, g2_ref[...])
    h2_ref[...] = h2.astype(BF16)

    logits = _dot3(wr_ref[...], h2, NT) + br_ref[:, 0:1]
    gl = [logits[g:g + 1, :] for g in range(N_GROUPS)]
    gmax = functools.reduce(jnp.maximum, gl)
    gsum = functools.reduce(lambda s, t: s + t, [jnp.exp(t - gmax) for t in gl])
    gp = 1.0 / gsum
    taken = jnp.zeros_like(gmax, dtype=jnp.bool_)
    sel = []
    for g in range(N_GROUPS):
        pick = (gl[g] == gmax) & jnp.logical_not(taken)
        sel.append(pick)
        taken = taken | pick
    el = []
    for e in range(EXPERTS_PER_GROUP):
        acc = jnp.zeros_like(gmax)
        for g in range(N_GROUPS):
            r0 = N_GROUPS + g * EXPERTS_PER_GROUP + e
            acc = jnp.where(sel[g], logits[r0:r0 + 1, :], acc)
        el.append(acc)
    emax = functools.reduce(jnp.maximum, el)
    ex = [jnp.exp(t - emax) for t in el]
    esum = functools.reduce(lambda s, t: s + t, ex)
    p = [t / esum for t in ex]
    p1 = functools.reduce(jnp.maximum, p)
    taken = jnp.zeros_like(gmax, dtype=jnp.bool_)
    first = []
    for e in range(EXPERTS_PER_GROUP):
        pick = (p[e] == p1) & jnp.logical_not(taken)
        first.append(pick)
        taken = taken | pick
    rest = [jnp.where(first[e], -1.0, p[e]) for e in range(EXPERTS_PER_GROUP)]
    p2 = functools.reduce(jnp.maximum, rest)
    taken = jnp.zeros_like(gmax, dtype=jnp.bool_)
    second = []
    for e in range(EXPERTS_PER_GROUP):
        pick = (rest[e] == p2) & jnp.logical_not(taken)
        second.append(pick)
        taken = taken | pick
    denom = p1 + p2
    for e in range(EXPERTS_PER_GROUP):
        in_group = jnp.where(first[e], p1 / denom, jnp.where(second[e], p2 / denom, 0.0))
        for g in range(N_GROUPS):
            r0 = g * EXPERTS_PER_GROUP + e
            comb_ref[r0:r0 + 1, :] = jnp.where(sel[g], gp * in_group, 0.0)
    gid = functools.reduce(lambda s, t: s + t,
                           [jnp.where(sel[g], float(g), 0.0) for g in range(N_GROUPS)])
    comb_ref[N_EXPERTS:N_EXPERTS + 1, :] = gid
    comb_ref[N_EXPERTS + 1:, :] = jnp.zeros((ROUTE_ROWS - N_EXPERTS - 1, gid.shape[1]), F32)


def _merge(x2, g, ya, attn, yc, layer, wg, bg, wa, wb, wc, wo, g2, wr, br, seq):
    n = x2.shape[0]
    tm = MERGE_ROW_TILE
    row = lambda width: pl.BlockSpec((tm, width), lambda i: (i, 0))
    of_layer = lambda rows, cols: pl.BlockSpec((1, rows, cols), lambda i: (layer, 0, 0))
    attn_flat = [t for pair in attn for t in pair]
    attn_specs = [_residue_spec(tm, t.shape[1], B_OUT, seq // tm) for t in attn_flat]
    return pl.pallas_call(
        _merge_kernel,
        grid=(n // tm,),
        in_specs=[row(D_MODEL), _full((1, D_MODEL)), row(A_WIDTH)] + attn_specs
        + [row(C_WIDTH), of_layer(D_MODEL, 3 * D_MODEL), _full((1, 3 * D_MODEL)),
           of_layer(A_WIDTH, D_MODEL), of_layer(B_OUT, D_MODEL), of_layer(C_WIDTH, D_MODEL),
           of_layer(D_MODEL, D_MODEL), _full((1, D_MODEL)), _full((32, D_MODEL)), _full((32, LANES))],
        out_specs=[row(D_MODEL), row(D_MODEL), pl.BlockSpec((ROUTE_ROWS, tm), lambda i: (0, i))],
        out_shape=[jax.ShapeDtypeStruct((n, D_MODEL), F32),
                   jax.ShapeDtypeStruct((n, D_MODEL), BF16),
                   jax.ShapeDtypeStruct((ROUTE_ROWS, n), F32)],
        scratch_shapes=[pltpu.VMEM((4, B_OUT // LANES, tm, LANES), F32)],
        compiler_params=_params("parallel"),
        name="merge_router",
    )(x2, g, ya, *attn_flat, yc, wg, bg, wa, wb, wc, wo, g2, wr, br)


def _moe_kernel(dense_ref, xm_ref, h2_ref, route_ref, comb3_ref, comb_ref, wg_ref, wu_ref, wd_ref,
                gf_ref, out_ref, p_ref, xc_ref, yc_ref, cw_ref, *, final_norm):
    i, e = pl.program_id(0), pl.program_id(1)
    t = xm_ref.shape[0]
    cap = MOE_CAP
    wg, wu, wd = wg_ref[0, 0], wu_ref[0, 0], wd_ref[0, 0]
    compact = dense_ref[i] == 0
    last = e == N_EXPERTS - 1

    def ffn(h, cw):
        gate = jnp.dot(h, wg, preferred_element_type=F32)
        up = jnp.dot(h, wu, preferred_element_type=F32)
        hid = gate * _sigmoid(gate) * up * cw
        return jnp.dot(hid.astype(BF16), wd, preferred_element_type=F32)

    def finish(res):
        out_ref[...] = _rmsnorm(res, gf_ref[...]) if final_norm else res

    @pl.when(compact & (e == 0))
    def _():
        gid = route_ref[N_EXPERTS:N_EXPERTS + 1, :]
        grow = lax.broadcasted_iota(jnp.int32, (8, t), 0).astype(F32)
        member = gid == grow
        before = (lax.broadcasted_iota(jnp.int32, (t, t), 0)
                  < lax.broadcasted_iota(jnp.int32, (t, t), 1)).astype(BF16)
        ranks = jnp.dot(member.astype(BF16), before, preferred_element_type=F32)
        rank = jnp.sum(jnp.where(member, ranks, 0.0), axis=0, keepdims=True)
        slot = lax.broadcasted_iota(jnp.int32, (cap, t), 0).astype(F32)
        for g in range(N_GROUPS):
            rs = slice(g * cap, (g + 1) * cap)
            rank_g = jnp.where(gid == float(g), rank, -1.0)
            p = jnp.where(rank_g == slot, 1.0, 0.0).astype(BF16)
            p_ref[rs, :] = p
            xc_ref[rs, :] = jnp.dot(p, h2_ref[...], preferred_element_type=F32).astype(BF16)
            cw_ref[rs, :] = jnp.dot(p, comb3_ref[...], preferred_element_type=F32)
        yc_ref[...] = jnp.zeros_like(yc_ref)

    @pl.when(compact)
    def _():
        rs = pl.ds(pl.multiple_of((e // EXPERTS_PER_GROUP) * cap, 16), cap)
        lane = lax.broadcasted_iota(jnp.int32, (cap, LANES), 1)
        mine = (lane % N_EXPERTS == e) & (lane < 3 * N_EXPERTS)
        cw = jnp.sum(jnp.where(mine, cw_ref[rs, :], 0.0), axis=-1, keepdims=True)
        yc_ref[rs, :] += ffn(xc_ref[rs, :], cw)

        @pl.when(last)
        def _():
            moe = lax.dot_general(p_ref[...], yc_ref[...].astype(BF16), TN,
                                  preferred_element_type=F32)
            finish(xm_ref[...] + moe)

    @pl.when(jnp.logical_not(compact))
    def _():
        @pl.when(e == 0)
        def _():
            out_ref[...] = xm_ref[...]

        lane = lax.broadcasted_iota(jnp.int32, comb_ref.shape, 1)
        cw = jnp.sum(jnp.where(lane == e, comb_ref[...], 0.0), axis=-1, keepdims=True)
        out_ref[...] += ffn(h2_ref[...], cw)

        if final_norm:
            @pl.when(last)
            def _():
                finish(out_ref[...])


def _moe(xm, h2, route, layer, wg, wu, wd, g_final, final_norm):
    n = xm.shape[0]
    tm = MOE_ROW_TILE
    cap = MOE_CAP
    comb = route[:N_EXPERTS].T
    pieces = []
    rest = comb
    for _ in range(3):
        pieces.append(rest.astype(BF16))
        rest = rest - pieces[-1].astype(F32)
    comb3 = jnp.concatenate(pieces + [jnp.zeros((n, LANES - 3 * N_EXPERTS), BF16)], axis=1)
    gid = route[N_EXPERTS].astype(jnp.int32).reshape(n // tm, tm)
    counts = jnp.sum(gid[:, :, None] == jnp.arange(N_GROUPS)[None, None, :], axis=1)
    dense = (jnp.max(counts, axis=1) > cap).astype(jnp.int32)
    tile = lambda width: pl.BlockSpec((tm, width), lambda i, e, d: (i, 0))
    expert = lambda rows, cols: pl.BlockSpec((1, 1, rows, cols), lambda i, e, d: (layer, e, 0, 0))
    grid_spec = pltpu.PrefetchScalarGridSpec(
        num_scalar_prefetch=1,
        grid=(n // tm, N_EXPERTS),
        in_specs=[tile(D_MODEL), tile(D_MODEL),
                  pl.BlockSpec((ROUTE_ROWS, tm), lambda i, e, d: (0, i)),
                  tile(LANES), tile(N_EXPERTS),
                  expert(D_MODEL, EXPERT_FF), expert(D_MODEL, EXPERT_FF), expert(EXPERT_FF, D_MODEL),
                  pl.BlockSpec((1, D_MODEL), lambda i, e, d: (0, 0))],
        out_specs=tile(D_MODEL),
        scratch_shapes=[pltpu.VMEM((N_GROUPS * cap, tm), BF16),
                        pltpu.VMEM((N_GROUPS * cap, D_MODEL), BF16),
                        pltpu.VMEM((N_GROUPS * cap, D_MODEL), F32),
                        pltpu.VMEM((N_GROUPS * cap, LANES), F32)])
    return pl.pallas_call(
        functools.partial(_moe_kernel, final_norm=final_norm),
        grid_spec=grid_spec,
        out_shape=jax.ShapeDtypeStruct((n, D_MODEL), F32),
        compiler_params=_params("parallel", "arbitrary"),
        name="moe_ffn",
    )(dense, xm, h2, route, comb3, comb, wg, wu, wd, g_final)


def _pack_lora(mu_wag, w1, a1, g1, mu_v, v1):
    d = D_MODEL
    if v1 is None:
        mu_v = jnp.zeros((d,), F32)
        v1 = jnp.zeros((d, V_LORA), F32)
    pad = jnp.zeros((d, LORA_HALF - (W_LORA + A_LORA + G_LORA + V_LORA)), F32)
    mus = (mu_wag[0], mu_wag[1], mu_wag[2], mu_v)
    ws = (w1, a1, g1, v1)
    cur = [w * (1.0 - m)[:, None] for w, m in zip(ws, mus)]
    prev = [w * m[:, None] for w, m in zip(ws, mus)]
    return jnp.concatenate(cur + [pad] + prev + [pad], axis=1)


def _pad_rows(w, lo, total=LANES):
    return jnp.zeros((total, w.shape[1]), F32).at[lo:lo + w.shape[0]].set(w)


def kernel(x, positions, norm_mix_g, w_in, gmlp_ln_g, gmlp_ln_b, gmlp_ws, gmlp_bs, rwkv_mu_rkv, rwkv_mu_wag, rwkv_w0, rwkv_w1, rwkv_w2, rwkv_a0, rwkv_a1, rwkv_a2, rwkv_g1, rwkv_g2, rwkv_k_k, rwkv_k_a, rwkv_r_k, rwkv_ln_w, rwkv_ln_b, rwkv_mu_v, rwkv_v0, rwkv_v1, rwkv_v2, w_branch_a, w_branch_b, w_branch_c, w_gate, b_gate, w_out, norm_ffn_g, router_group_w, router_group_b, router_expert_w, router_expert_b, expert_w_gate, expert_w_up, expert_w_down, final_norm_g):
    batch, seq, d = x.shape
    depth = w_in.shape[0]
    n = batch * seq
    x2 = x.reshape(n, d)
    rope = _rope_tables(positions)
    w_in_b, w_gate_b, w_out_b = w_in.astype(BF16), w_gate.astype(BF16), w_out.astype(BF16)
    wba, wbb, wbc = w_branch_a.astype(BF16), w_branch_b.astype(BF16), w_branch_c.astype(BF16)
    ewg, ewu, ewd = (w.astype(BF16) for w in (expert_w_gate, expert_w_up, expert_w_down))
    v_first = None
    for l in range(depth):
        vres = l > 0
        lora_w = _pack_lora(rwkv_mu_wag[l], rwkv_w1[l], rwkv_a1[l], rwkv_g1[l],
                            rwkv_mu_v[l - 1] if vres else None, rwkv_v1[l - 1] if vres else None)
        ya, qkv0, qkv1, qkv2, rkv, lora = _inproj(
            x2, norm_mix_g[l][None], w_in_b, l, lora_w.astype(BF16), gmlp_ln_g[l][None],
            gmlp_ln_b[l][None], gmlp_ws[l], gmlp_bs[l].T, rope, batch, seq)

        attn = [_dilated_attention(qkv, dil)
                for qkv, (_, dil) in zip((qkv0, qkv1, qkv2), DILATED_PATTERNS)]

        zero = jnp.zeros((C_WIDTH,), F32)
        vecs = jnp.stack([rwkv_w0[l], rwkv_a0[l], rwkv_v0[l - 1] if vres else zero, rwkv_k_k[l],
                          rwkv_k_a[l], rwkv_r_k[l].reshape(C_WIDTH), rwkv_ln_w[l], rwkv_ln_b[l]])
        yc, v_c = _rwkv(rkv, lora, v_first, rwkv_mu_rkv[l], vecs,
                        _pad_rows(rwkv_w2[l], 0), _pad_rows(rwkv_a2[l], W_LORA), rwkv_g2[l],
                        _pad_rows(rwkv_v2[l - 1], 0) if vres else None, batch, seq)
        if l == 0:
            v_first = v_c

        wr = jnp.concatenate([router_group_w[l], router_expert_w[l]], axis=1).T
        wr = jnp.zeros((32, d), F32).at[:wr.shape[0]].set(wr)
        br = jnp.concatenate([router_group_b[l], router_expert_b[l]])
        br = jnp.zeros((32, LANES), F32).at[:br.shape[0], 0].set(br)
        xm, h2, route = _merge(
            x2, norm_mix_g[l][None], ya, attn, yc, l, w_gate_b, b_gate[l][None], wba, wbb, wbc,
            w_out_b, norm_ffn_g[l][None], wr, br, seq)

        x2 = _moe(xm, h2, route, l, ewg, ewu, ewd, final_norm_g[None], final_norm=(l == depth - 1))
    return x2.reshape(batch, seq, d)
```

```python
import functools

import numpy as np
import jax
import jax.numpy as jnp
from jax import lax
from jax.experimental import pallas as pl
from jax.experimental.pallas import tpu as pltpu

F32 = jnp.float32
BF16 = jnp.bfloat16
HIGHEST = lax.Precision.HIGHEST

D_MODEL = 1024
HEAD_DIM = 64
A_GROUPS = 4
A_WIDTH = 512
CHUNK = 128
DILATED_PATTERNS = ((128, 1), (512, 4), (2048, 16))
B_HEADS_PER_GROUP = 4
B_WIDTH = 768
B_OUT = 256
BLOCK = 128
ROPE_THETA = 500000.0
ROPE_DIMS = 16
C_HEADS = 8
C_WIDTH = 512
W_LORA, A_LORA, V_LORA, G_LORA = 64, 64, 32, 128
RWKV_GN_EPS = HEAD_DIM * 1e-5
IN_COLS = 2 * A_WIDTH + 3 * B_WIDTH + 3 * C_WIDTH
LORA_HALF = 384
LORA_COLS = 2 * LORA_HALF
N_GROUPS = 4
EXPERTS_PER_GROUP = 4
N_EXPERTS = 16
EXPERT_FF = 512
RMS_EPS = 1e-6
LN_EPS = 1e-5

LANES = 128
ROW_TILE = 512
MERGE_ROW_TILE = 512
RWKV_CHUNK = 64
RWKV_BATCH_ROWS = 2
MOE_ROW_TILE = 1024
MOE_CAP = 320
MOE_FFN_ROWS = 1280
ROUTE_ROWS = 24
VMEM_LIMIT = 56 * 1024 * 1024
NEG_BIG = -1e30


def _params(*sem):
    return pltpu.CompilerParams(dimension_semantics=sem, vmem_limit_bytes=VMEM_LIMIT)


def _full(shape):
    nd = len(shape)
    return pl.BlockSpec(shape, lambda *_: (0,) * nd)


def _gelu_tanh(x):
    return 0.5 * x * (1.0 + jnp.tanh(0.7978845608028654 * (x + 0.044715 * (x * x * x))))


def _sigmoid(x):
    return 1.0 / (1.0 + jnp.exp(-x))


def _rmsnorm(x, g):
    return x * lax.rsqrt(jnp.mean(x * x, axis=-1, keepdims=True) + RMS_EPS) * g


def _rope_table_kernel(pos_ref, consts_ref, c_ref, sa_ref, sb_ref):
    ang = pos_ref[...].astype(F32) * consts_ref[0:1, :]
    c_ref[...] = jnp.cos(ang)
    s = jnp.sin(ang)
    sa_ref[...] = -s * consts_ref[1:2, :]
    sb_ref[...] = s * consts_ref[2:3, :]


def _rope_tables(positions):
    n = positions.size
    half = ROPE_DIMS // 2
    inv_freq = ROPE_THETA ** (-jnp.arange(half, dtype=F32) / half)
    lane = np.arange(LANES) % HEAD_DIM
    in_rope = lane < ROPE_DIMS
    invf_row = jnp.where(in_rope, inv_freq[lane % half], 0.0)
    consts = jnp.zeros((8, LANES), F32)
    consts = consts.at[0].set(invf_row)
    consts = consts.at[1].set(jnp.asarray(lane < half, F32))
    consts = consts.at[2].set(jnp.asarray((lane >= half) & in_rope, F32))
    tm = 1024
    out = jax.ShapeDtypeStruct((n, LANES), F32)
    return pl.pallas_call(
        _rope_table_kernel,
        grid=(n // tm,),
        in_specs=[pl.BlockSpec((tm, 1), lambda i: (i, 0)), _full((8, LANES))],
        out_specs=[pl.BlockSpec((tm, LANES), lambda i: (i, 0))] * 3,
        out_shape=[out, out, out],
        compiler_params=_params("parallel"),
        name="rope_tables",
    )(positions.reshape(n, 1), consts)


def _inproj_kernel(x_ref, g_ref, w_ref, wl_ref, lng_ref, lnb_ref, ws_ref, bst_ref, c_ref, sa_ref,
                   sb_ref, ya_ref, qkv0_ref, qkv1_ref, qkv2_ref, rkv_ref, lora_ref, nat_ref):
    tm = x_ref.shape[0]
    hb = _rmsnorm(x_ref[...], g_ref[...]).astype(BF16)

    def proj(lo, width):
        return jnp.dot(hb, w_ref[0, :, lo:lo + width], preferred_element_type=F32)

    uv = proj(0, 2 * A_WIDTH)
    u = _gelu_tanh(uv[:, :A_WIDTH])
    v = _gelu_tanh(uv[:, A_WIDTH:])
    mu = jnp.mean(v, axis=-1, keepdims=True)
    d = v - mu
    var = jnp.mean(d * d, axis=-1, keepdims=True)
    vn = (d * lax.rsqrt(var + LN_EPS) * lng_ref[...] + lnb_ref[...]).astype(BF16)
    row = lax.broadcasted_iota(jnp.int32, (CHUNK, CHUNK), 0)
    col = lax.broadcasted_iota(jnp.int32, (CHUNK, CHUNK), 1)
    for g in range(A_GROUPS):
        wg = jnp.where(row >= col, ws_ref[g], 0.0).astype(BF16)
        bias = bst_ref[:, g:g + 1]
        cs = slice(g * LANES, (g + 1) * LANES)
        for c in range(tm // CHUNK):
            rs = slice(c * CHUNK, (c + 1) * CHUNK)
            mixed = jnp.dot(wg, vn[rs, cs], preferred_element_type=F32) + bias
            ya_ref[rs, cs] = (u[rs, cs] * mixed).astype(BF16)

    cos, sa, sb = c_ref[...], sa_ref[...], sb_ref[...]
    q_off = 2 * A_WIDTH
    for grp, (qkv_ref, (_, dil)) in enumerate(zip((qkv0_ref, qkv1_ref, qkv2_ref), DILATED_PATTERNS)):
        acc = [proj(q_off + part * B_WIDTH + grp * B_OUT, B_OUT) for part in range(3)]
        for c in range(B_WIDTH // LANES):
            t = acc[c * LANES // B_OUT][:, c * LANES % B_OUT:c * LANES % B_OUT + LANES]
            if c < 2 * B_OUT // LANES:
                t = t * cos + pltpu.roll(t, LANES - 8, 1) * sa + pltpu.roll(t, 8, 1) * sb
            if c < B_OUT // LANES:
                t = t * (HEAD_DIM ** -0.5)
            cs = slice(c * LANES, (c + 1) * LANES)
            if dil == 1:
                qkv_ref[0, 0, :, cs] = t.astype(BF16)
            else:
                nat_ref[c] = t
                for r in range(dil):
                    qkv_ref[0, r, :, cs] = nat_ref[c, pl.ds(r, tm // dil, stride=dil), :].astype(BF16)

    rkv_ref[...] = proj(q_off + 3 * B_WIDTH, 3 * C_WIDTH)
    lora_ref[...] = jnp.dot(hb, wl_ref[...], preferred_element_type=F32)


def _residue_spec(tm, dil, width, tiles_per_seq):
    return pl.BlockSpec((1, dil, tm // dil, width),
                        lambda i: (i // tiles_per_seq, 0, i % tiles_per_seq, 0))


def _inproj(x2, g, w_in, layer, w_lora, ln_g, ln_b, ws, bs_t, rope, batch, seq):
    n = x2.shape[0]
    tm = ROW_TILE
    row = lambda width: pl.BlockSpec((tm, width), lambda i: (i, 0))
    dils = [d for _, d in DILATED_PATTERNS]
    return pl.pallas_call(
        _inproj_kernel,
        grid=(n // tm,),
        in_specs=[row(D_MODEL), _full((1, D_MODEL)),
                  pl.BlockSpec((1, D_MODEL, IN_COLS), lambda i: (layer, 0, 0)),
                  _full((D_MODEL, LORA_COLS)),
                  _full((1, A_WIDTH)), _full((1, A_WIDTH)), _full((A_GROUPS, CHUNK, CHUNK)),
                  _full((CHUNK, A_GROUPS)), row(LANES), row(LANES), row(LANES)],
        out_specs=[row(A_WIDTH)] + [_residue_spec(tm, d, B_WIDTH, seq // tm) for d in dils]
        + [row(3 * C_WIDTH), row(LORA_COLS)],
        out_shape=[jax.ShapeDtypeStruct((n, A_WIDTH), BF16)]
        + [jax.ShapeDtypeStruct((batch, d, seq // d, B_WIDTH), BF16) for d in dils]
        + [jax.ShapeDtypeStruct((n, 3 * C_WIDTH), F32),
           jax.ShapeDtypeStruct((n, LORA_COLS), F32)],
        scratch_shapes=[pltpu.VMEM((B_WIDTH // LANES, tm, LANES), F32)],
        compiler_params=_params("parallel"),
        name="inproj",
    )(x2, g, w_in, w_lora, ln_g, ln_b, ws, bs_t, *rope)


def _attn_block(q, kw, vw, valid):
    lo = lax.broadcasted_iota(jnp.int32, (BLOCK, LANES), 1) < HEAD_DIM
    pairs = range(B_OUT // LANES)
    scores = []
    for p in pairs:
        ps = slice(p * LANES, (p + 1) * LANES)
        qp = q[:, ps]
        zero = jnp.zeros_like(qp)
        for part in (jnp.where(lo, qp, zero), jnp.where(lo, zero, qp)):
            s = lax.dot_general(part, kw[:, ps], NT, preferred_element_type=F32)
            scores.append(jnp.where(valid, s, NEG_BIG))
    m = [jnp.max(s, axis=-1, keepdims=True) for s in scores]
    e = [jnp.exp(s - mx) for s, mx in zip(scores, m)]
    l = [jnp.sum(t, axis=-1, keepdims=True) for t in e]
    pv = [jnp.dot(t.astype(BF16), vw[:, (h // 2) * LANES:(h // 2 + 1) * LANES],
                  preferred_element_type=F32) for h, t in enumerate(e)]
    out = [jnp.where(lo, pv[2 * p] / l[2 * p], pv[2 * p + 1] / l[2 * p + 1]) for p in pairs]
    lse = [mx + jnp.log(t) for mx, t in zip(m, l)]
    lse = [jnp.where(lo, lse[2 * p], lse[2 * p + 1]) for p in pairs]
    return jnp.concatenate(out, axis=-1), jnp.concatenate(lse, axis=-1)


def _attn_kernel(qkv_ref, o_ref, l_ref):
    dil, sub = qkv_ref.shape[1], qkv_ref.shape[2]
    nb = sub // BLOCK
    qs, ks, vs = (slice(i * B_OUT, (i + 1) * B_OUT) for i in range(3))
    qi = lax.broadcasted_iota(jnp.int32, (BLOCK, BLOCK), 0)
    kj = lax.broadcasted_iota(jnp.int32, (BLOCK, BLOCK), 1)
    qi2 = lax.broadcasted_iota(jnp.int32, (BLOCK, 2 * BLOCK), 0)
    kj2 = lax.broadcasted_iota(jnp.int32, (BLOCK, 2 * BLOCK), 1)
    valid2 = (kj2 >= qi2) & (kj2 <= qi2 + BLOCK)

    def residue(r):
        first = pl.ds(0, BLOCK)
        o, l = _attn_block(qkv_ref[0, r, first, qs], qkv_ref[0, r, first, ks],
                           qkv_ref[0, r, first, vs], kj <= qi)
        o_ref[0, r, first, :] = o
        l_ref[0, r, first, :] = l

        def body(n, carry):
            cur = pl.ds(pl.multiple_of(n * BLOCK, BLOCK), BLOCK)
            win = pl.ds(pl.multiple_of((n - 1) * BLOCK, BLOCK), 2 * BLOCK)
            o, l = _attn_block(qkv_ref[0, r, cur, qs], qkv_ref[0, r, win, ks],
                               qkv_ref[0, r, win, vs], valid2)
            o_ref[0, r, cur, :] = o
            l_ref[0, r, cur, :] = l
            return carry

        if nb > 1:
            lax.fori_loop(1, nb, body, 0)

    if dil <= 4:
        for r in range(dil):
            residue(r)
    else:
        def rbody(r, carry):
            residue(r)
            return carry
        lax.fori_loop(0, dil, rbody, 0)


def _dilated_attention(qkv, dilation):
    batch, _, sub, _ = qkv.shape
    ospec = pl.BlockSpec((1, dilation, sub, B_OUT), lambda b: (b, 0, 0, 0))
    oshape = jax.ShapeDtypeStruct((batch, dilation, sub, B_OUT), F32)
    return pl.pallas_call(
        _attn_kernel,
        grid=(batch,),
        in_specs=[pl.BlockSpec((1, dilation, sub, B_WIDTH), lambda b: (b, 0, 0, 0))],
        out_specs=[ospec, ospec],
        out_shape=[oshape, oshape],
        compiler_params=_params("parallel"),
        name=f"dilated_attn_d{dilation}",
    )(qkv)


def _shift_rows(x, prev_row):
    rolled = pltpu.roll(x, 1, 0)
    first = lax.broadcasted_iota(jnp.int32, x.shape, 0) == 0
    return jnp.where(first, prev_row, rolled)


NN = (((1,), (0,)), ((), ()))
NT = (((1,), (1,)), ((), ()))
TN = (((0,), (0,)), ((), ()))


def _dot(a, b, dims=NN):
    return lax.dot_general(a.astype(BF16), b.astype(BF16), dims, preferred_element_type=F32)


def _split(a, pieces):
    out = []
    for _ in range(pieces):
        p = a.astype(BF16)
        out.append(p)
        a = a - p.astype(F32)
    return out


def _dot3(a, b, dims=NN):
    (ah, al), (bh, bl) = _split(a, 2), _split(b, 2)
    return _dot(ah, bh, dims) + _dot(ah, bl, dims) + _dot(al, bh, dims)


def _head_sum(x):
    lo = lax.broadcasted_iota(jnp.int32, (x.shape[0], LANES), 1) < HEAD_DIM
    outs = []
    for c in range(x.shape[1] // LANES):
        xc = x[:, c * LANES:(c + 1) * LANES]
        s_lo = jnp.sum(jnp.where(lo, xc, 0.0), axis=-1, keepdims=True)
        s_hi = jnp.sum(jnp.where(lo, 0.0, xc), axis=-1, keepdims=True)
        outs.append(jnp.where(lo, s_lo, s_hi))
    return jnp.concatenate(outs, axis=-1)


def _rwkv_kernel(*refs, has_vres):
    if has_vres:
        (rkv_ref, lora_ref, vf_ref, mu_ref, vec_ref, w2_ref, a2_ref, g2_ref, v2_ref,
         yc_ref, vout_ref, state_ref, prkv_ref, plora_ref, y_ref) = refs
    else:
        (rkv_ref, lora_ref, mu_ref, vec_ref, w2_ref, a2_ref, g2_ref,
         yc_ref, vout_ref, state_ref, prkv_ref, plora_ref, y_ref) = refs
    nbat, ch = rkv_ref.shape[0], rkv_ref.shape[1]
    rows = nbat * ch
    hd = HEAD_DIM
    row_of = [slice(j * ch, (j + 1) * ch) for j in range(nbat)]

    @pl.when(pl.program_id(1) == 0)
    def _():
        state_ref[...] = jnp.zeros_like(state_ref)
        prkv_ref[...] = jnp.zeros_like(prkv_ref)
        plora_ref[...] = jnp.zeros_like(plora_ref)

    def shifted(x, prev_ref):
        parts = []
        for j, rs in enumerate(row_of):
            parts.append(_shift_rows(x[rs], prev_ref[8 * j:8 * j + 1, :]))
            prev_ref[8 * j:8 * j + 1, :] = x[(j + 1) * ch - 1:(j + 1) * ch, :]
        return jnp.concatenate(parts, axis=0)

    rkv = rkv_ref[...].reshape(rows, 3 * C_WIDTH)
    rkv_prev = shifted(rkv, prkv_ref)
    lora_all = lora_ref[...].reshape(rows, LORA_COLS)
    lora = lora_all[:, :LORA_HALF] + shifted(lora_all[:, LORA_HALF:], plora_ref)

    w0, a0, v0 = vec_ref[0:1, :], vec_ref[1:2, :], vec_ref[2:3, :]
    k_k, k_a, r_k = vec_ref[3:4, :], vec_ref[4:5, :], vec_ref[5:6, :]
    ln_w, ln_b = vec_ref[6:7, :], vec_ref[7:8, :]

    def mix(i):
        cur = rkv[:, i * C_WIDTH:(i + 1) * C_WIDTH]
        prev = rkv_prev[:, i * C_WIDTH:(i + 1) * C_WIDTH]
        return cur + (prev - cur) * mu_ref[i:i + 1, :]

    r, k, v = mix(0), mix(1), mix(2)
    wa = lora[:, 0:LANES]
    z = w0 + _dot3(jnp.tanh(wa), w2_ref[...])
    log_decay = -float(np.exp(-0.5)) * _sigmoid(z)
    a = _sigmoid(a0 + _dot(wa, a2_ref[...]))
    gate = _dot(_sigmoid(lora[:, LANES:2 * LANES]), g2_ref[...])
    if has_vres:
        vmix = _sigmoid(v0 + _dot(lora[:, 2 * LANES:3 * LANES], v2_ref[...]))
        v = v + (vf_ref[...].reshape(rows, C_WIDTH) - v) * vmix
    vout_ref[...] = v.reshape(nbat, ch, C_WIDTH)

    kk = k * k_k
    kk = kk * lax.rsqrt(jnp.maximum(_head_sum(kk * kk), 1e-24))
    k = k * (1.0 + (a - 1.0) * k_a)
    bonus = _head_sum(r * k * r_k) * v
    b = kk * a

    ti = lax.broadcasted_iota(jnp.int32, (rows, rows), 0)
    si = lax.broadcasted_iota(jnp.int32, (rows, rows), 1)
    tri = ((si <= ti) & (si // ch == ti // ch)).astype(BF16)
    cum = functools.reduce(lambda s, t: s + t, [_dot(tri, p) for p in _split(log_decay, 3)])
    ends = [cum[(j + 1) * ch - 1:(j + 1) * ch, :] for j in range(nbat)]
    cum_end = jnp.concatenate([jnp.broadcast_to(e, (ch, C_WIDTH)) for e in ends], axis=0)
    e_neg = jnp.exp(-cum)
    e_end = jnp.exp(cum_end - cum)
    kt = kk * jnp.exp(cum - log_decay)
    rt = r * jnp.exp(cum)
    bt = b * e_neg
    kq = k * e_neg
    kh = k * e_end
    bh = b * e_end
    unit = (lax.broadcasted_iota(jnp.int32, (8, LANES), 0) == 0).astype(BF16)
    w_end_t = [functools.reduce(lambda s, t: s + t,
                                [_dot(jnp.broadcast_to(p, (8, C_WIDTH)), unit, TN)
                                 for p in _split(jnp.exp(e), 2)])
               for e in ends]

    lane = lax.broadcasted_iota(jnp.int32, (ch, LANES), 1)
    lo = lane < hd
    lo2 = lax.broadcasted_iota(jnp.int32, (2 * ch, LANES), 1) < hd
    row4 = lax.broadcasted_iota(jnp.int32, (4 * ch, LANES), 0)
    col4 = lax.broadcasted_iota(jnp.int32, (4 * ch, LANES), 1) % ch
    t4 = row4 % ch
    tri_mask = (col4 < t4) | ((col4 == t4) & ((row4 // ch) % 2 == 1))
    eye_right = (lane - hd == lax.broadcasted_iota(jnp.int32, (ch, LANES), 0)).astype(F32)
    brow = lax.broadcasted_iota(jnp.int32, (LANES, LANES), 0) // hd
    bcol = lax.broadcasted_iota(jnp.int32, (LANES, LANES), 1) // hd
    block_diag = brow == bcol
    zeros = jnp.zeros((ch, LANES), F32)

    npair = C_HEADS // 2
    chains = [(j, p) for j in range(nbat) for p in range(npair)]
    at = lambda t, c: t[row_of[c[0]], c[1] * LANES:(c[1] + 1) * LANES]
    idx = range(len(chains))
    s_bd = [state_ref[j * npair + p] for j, p in chains]
    l2 = [jnp.concatenate([at(kt, c), at(rt, c)], axis=0) for c in chains]
    ss = [_dot(l2[i], s_bd[i]) for i in idx]
    g = [jnp.where(tri_mask,
                   _dot(jnp.concatenate([jnp.where(lo2, l2[i], 0.0), jnp.where(lo2, 0.0, l2[i])],
                                        axis=0),
                        jnp.concatenate([at(bt, c), at(kq, c)], axis=0), NT), 0.0)
         for i, c in zip(idx, chains)]
    xv = [_dot(jnp.concatenate([g[i][0:ch], g[i][2 * ch:3 * ch]], axis=0),
               jnp.concatenate([zeros, at(v, c)], axis=0))
          for i, c in zip(idx, chains)]
    heads = [(i, h) for i in idx for h in range(2)]
    n0 = [-g[i][2 * h * ch:(2 * h + 1) * ch] for i, h in heads]
    w = [_dot(n[:, :ch], jnp.where(lo, n, eye_right)) + jnp.where(lo, 0.0, eye_right) for n in n0]
    for _ in range(int(np.log2(ch)) - 1):
        w = [_dot(t[:, :ch], t) + jnp.where(lo, 0.0, t) for t in w]
    us = [_dot(w[2 * i + h],
               jnp.concatenate([zeros, ss[i][0:ch] + xv[i][h * ch:(h + 1) * ch]], axis=0))
          for i, h in heads]
    u = [jnp.where(lo, us[2 * i], us[2 * i + 1]) for i in idx]
    yv = [_dot(jnp.concatenate([g[i][ch:2 * ch], g[i][3 * ch:4 * ch]], axis=0),
               jnp.concatenate([-u[i], at(v, c)], axis=0))
          for i, c in zip(idx, chains)]
    upd = [_dot(jnp.concatenate([at(kh, c), at(bh, c)], axis=0),
                jnp.concatenate([at(v, c), -u[i]], axis=0), TN)
           for i, c in zip(idx, chains)]
    for i, (j, p) in zip(idx, chains):
        ps = slice(p * LANES, (p + 1) * LANES)
        y_ref[row_of[j], ps] = ss[i][ch:2 * ch] + jnp.where(lo, yv[i][0:ch], yv[i][ch:2 * ch])
        state_ref[j * npair + p] = w_end_t[j][ps, :] * s_bd[i] + jnp.where(block_diag, upd[i], 0.0)

    y = y_ref[...]
    d = y - _head_sum(y) * (1.0 / hd)
    var = _head_sum(d * d) * (1.0 / hd)
    yn = d * lax.rsqrt(var + RWKV_GN_EPS) * ln_w + ln_b
    yc_ref[...] = ((yn + bonus) * gate).astype(BF16).reshape(nbat, ch, C_WIDTH)


def _rwkv(rkv, lora, v_first, mu_rkv, vecs, w2, a2, g2, v2, batch, seq):
    ch = RWKV_CHUNK
    nbat = RWKV_BATCH_ROWS
    has_vres = v_first is not None
    per_seq = lambda t: t.reshape(batch, seq, t.shape[-1])
    blk = lambda width: pl.BlockSpec((nbat, ch, width), lambda i, c: (i, c, 0))
    ins = [per_seq(rkv), per_seq(lora)] + ([per_seq(v_first)] if has_vres else []) \
        + [mu_rkv, vecs, w2, a2, g2] + ([v2] if has_vres else [])
    specs = [blk(3 * C_WIDTH), blk(LORA_COLS)] + ([blk(C_WIDTH)] if has_vres else []) \
        + [_full((3, C_WIDTH)), _full((8, C_WIDTH)), _full((LANES, C_WIDTH)),
           _full((LANES, C_WIDTH)), _full((LANES, C_WIDTH))] \
        + ([_full((LANES, C_WIDTH))] if has_vres else [])
    yc, v_out = pl.pallas_call(
        functools.partial(_rwkv_kernel, has_vres=has_vres),
        grid=(batch // nbat, seq // ch),
        in_specs=specs,
        out_specs=[blk(C_WIDTH), blk(C_WIDTH)],
        out_shape=[jax.ShapeDtypeStruct((batch, seq, C_WIDTH), BF16),
                   jax.ShapeDtypeStruct((batch, seq, C_WIDTH), F32)],
        scratch_shapes=[pltpu.VMEM((nbat * C_HEADS // 2, LANES, LANES), F32),
                        pltpu.VMEM((8 * nbat, 3 * C_WIDTH), F32),
                        pltpu.VMEM((8 * nbat, LORA_HALF), F32),
                        pltpu.VMEM((nbat * ch, C_WIDTH), F32)],
        compiler_params=_params("parallel", "arbitrary"),
        name="rwkv7_chunked",
    )(*ins)
    return yc.reshape(batch * seq, C_WIDTH), v_out.reshape(batch * seq, C_WIDTH)


def _merge_kernel(x_ref, g_ref, ya_ref, o0_ref, l0_ref, o1_ref, l1_ref, o2_ref, l2_ref, yc_ref,
                  wg_ref, bg_ref, wa_ref, wb_ref, wc_ref, wo_ref, g2_ref, wr_ref, br_ref,
                  xm_ref, h2_ref, comb_ref, nat_ref):
    x = x_ref[...]
    tm = x.shape[0]
    hb = _rmsnorm(x, g_ref[...]).astype(BF16)

    def token_order(ref, slot):
        dil = ref.shape[1]
        if dil == 1:
            return ref[0, 0]
        chunks = []
        for c in range(B_OUT // LANES):
            for r in range(dil):
                nat_ref[slot, c, pl.ds(r, tm // dil, stride=dil), :] = \
                    ref[0, r, :, c * LANES:(c + 1) * LANES]
            chunks.append(nat_ref[slot, c])
        return jnp.concatenate(chunks, axis=-1)

    l0, l1, l2 = token_order(l0_ref, 0), token_order(l1_ref, 0), token_order(l2_ref, 1)
    lm = jnp.maximum(jnp.maximum(l0, l1), l2)
    e0, e1, e2 = jnp.exp(l0 - lm), jnp.exp(l1 - lm), jnp.exp(l2 - lm)
    o0, o1, o2 = token_order(o0_ref, 0), token_order(o1_ref, 2), token_order(o2_ref, 3)
    yb = ((e0 * o0 + e1 * o1 + e2 * o2) / (e0 + e1 + e2)).astype(BF16)

    merged = None
    for j, (y, w_ref) in enumerate(((ya_ref[...], wa_ref), (yb, wb_ref), (yc_ref[...], wc_ref))):
        cs = slice(j * D_MODEL, (j + 1) * D_MODEL)
        gate = _sigmoid(jnp.dot(hb, wg_ref[0, :, cs], preferred_element_type=F32) + bg_ref[:, cs])
        term = gate * jnp.dot(y, w_ref[0], preferred_element_type=F32)
        merged = term if merged is None else merged + term
    xm = x + jnp.dot(merged.astype(BF16), wo_ref[0], preferred_element_type=F32)
    xm_ref[...] = xm

    h2 = _rmsnorm(xm, g2_ref[...])
    h2_ref[...] = h2.astype(BF16)

    logits = _dot3(wr_ref[...], h2, NT) + br_ref[:, 0:1]
    gl = [logits[g:g + 1, :] for g in range(N_GROUPS)]
    gmax = functools.reduce(jnp.maximum, gl)
    gsum = functools.reduce(lambda s, t: s + t, [jnp.exp(t - gmax) for t in gl])
    gp = 1.0 / gsum
    taken = jnp.zeros_like(gmax, dtype=jnp.bool_)
    sel = []
    for g in range(N_GROUPS):
        pick = (gl[g] == gmax) & jnp.logical_not(taken)
        sel.append(pick)
        taken = taken | pick
    el = []
    for e in range(EXPERTS_PER_GROUP):
        acc = jnp.zeros_like(gmax)
        for g in range(N_GROUPS):
            r0 = N_GROUPS + g * EXPERTS_PER_GROUP + e
            acc = jnp.where(sel[g], logits[r0:r0 + 1, :], acc)
        el.append(acc)
    emax = functools.reduce(jnp.maximum, el)
    ex = [jnp.exp(t - emax) for t in el]
    esum = functools.reduce(lambda s, t: s + t, ex)
    p = [t / esum for t in ex]
    p1 = functools.reduce(jnp.maximum, p)
    taken = jnp.zeros_like(gmax, dtype=jnp.bool_)
    first = []
    for e in range(EXPERTS_PER_GROUP):
        pick = (p[e] == p1) & jnp.logical_not(taken)
        first.append(pick)
        taken = taken | pick
    rest = [jnp.where(first[e], -1.0, p[e]) for e in range(EXPERTS_PER_GROUP)]
    p2 = functools.reduce(jnp.maximum, rest)
    taken = jnp.zeros_like(gmax, dtype=jnp.bool_)
    second = []
    for e in range(EXPERTS_PER_GROUP):
        pick = (rest[e] == p2) & jnp.logical_not(taken)
        second.append(pick)
        taken = taken | pick
    denom = p1 + p2
    for e in range(EXPERTS_PER_GROUP):
        in_group = jnp.where(first[e], p1 / denom, jnp.where(second[e], p2 / denom, 0.0))
        for g in range(N_GROUPS):
            r0 = g * EXPERTS_PER_GROUP + e
            comb_ref[r0:r0 + 1, :] = jnp.where(sel[g], gp * in_group, 0.0)
    gid = functools.reduce(lambda s, t: s + t,
                           [jnp.where(sel[g], float(g), 0.0) for g in range(N_GROUPS)])
    comb_ref[N_EXPERTS:N_EXPERTS + 1, :] = gid
    comb_ref[N_EXPERTS + 1:, :] = jnp.zeros((ROUTE_ROWS - N_EXPERTS - 1, gid.shape[1]), F32)


def _merge(x2, g, ya, attn, yc, layer, wg, bg, wa, wb, wc, wo, g2, wr, br, seq):
    n = x2.shape[0]
    tm = MERGE_ROW_TILE
    row = lambda width: pl.BlockSpec((tm, width), lambda i: (i, 0))
    of_layer = lambda rows, cols: pl.BlockSpec((1, rows, cols), lambda i: (layer, 0, 0))
    attn_flat = [t for pair in attn for t in pair]
    attn_specs = [_residue_spec(tm, t.shape[1], B_OUT, seq // tm) for t in attn_flat]
    return pl.pallas_call(
        _merge_kernel,
        grid=(n // tm,),
        in_specs=[row(D_MODEL), _full((1, D_MODEL)), row(A_WIDTH)] + attn_specs
        + [row(C_WIDTH), of_layer(D_MODEL, 3 * D_MODEL), _full((1, 3 * D_MODEL)),
           of_layer(A_WIDTH, D_MODEL), of_layer(B_OUT, D_MODEL), of_layer(C_WIDTH, D_MODEL),
           of_layer(D_MODEL, D_MODEL), _full((1, D_MODEL)), _full((32, D_MODEL)), _full((32, LANES))],
        out_specs=[row(D_MODEL), row(D_MODEL), pl.BlockSpec((ROUTE_ROWS, tm), lambda i: (0, i))],
        out_shape=[jax.ShapeDtypeStruct((n, D_MODEL), F32),
                   jax.ShapeDtypeStruct((n, D_MODEL), BF16),
                   jax.ShapeDtypeStruct((ROUTE_ROWS, n), F32)],
        scratch_shapes=[pltpu.VMEM((4, B_OUT // LANES, tm, LANES), F32)],
        compiler_params=_params("parallel"),
        name="merge_router",
    )(x2, g, ya, *attn_flat, yc, wg, bg, wa, wb, wc, wo, g2, wr, br)


def _group_onehots(route_ref, cap):
    t = route_ref.shape[1]
    gid = route_ref[N_EXPERTS:N_EXPERTS + 1, :]
    grow = lax.broadcasted_iota(jnp.int32, (8, t), 0).astype(F32)
    member = gid == grow
    before = (lax.broadcasted_iota(jnp.int32, (t, t), 0)
              < lax.broadcasted_iota(jnp.int32, (t, t), 1)).astype(BF16)
    ranks = jnp.dot(member.astype(BF16), before, preferred_element_type=F32)
    rank = jnp.sum(jnp.where(member, ranks, 0.0), axis=0, keepdims=True)
    slot = lax.broadcasted_iota(jnp.int32, (cap, t), 0).astype(F32)
    return [jnp.where(jnp.where(gid == float(g), rank, -1.0) == slot, 1.0, 0.0).astype(BF16)
            for g in range(N_GROUPS)]


def _expert_column(cw3, e):
    lane = lax.broadcasted_iota(jnp.int32, cw3.shape, 1)
    mine = (lane % N_EXPERTS == e) & (lane < 3 * N_EXPERTS)
    return jnp.sum(jnp.where(mine, cw3, 0.0), axis=-1, keepdims=True)


def _ffn(h, cw, wg, wu, wd):
    gate = jnp.dot(h, wg, preferred_element_type=F32)
    up = jnp.dot(h, wu, preferred_element_type=F32)
    hid = gate * _sigmoid(gate) * up * cw
    return jnp.dot(hid.astype(BF16), wd, preferred_element_type=F32)


def _moe_compact_kernel(h2_ref, route_ref, comb3_ref, xc_ref, cw_ref):
    for g, p in enumerate(_group_onehots(route_ref, xc_ref.shape[1])):
        xc_ref[g] = jnp.dot(p, h2_ref[...], preferred_element_type=F32).astype(BF16)
        cw_ref[g] = jnp.dot(p, comb3_ref[...], preferred_element_type=F32)


def _moe_expert_kernel(xc_ref, cw_ref, wg_ref, wu_ref, wd_ref, y_ref):
    j = pl.program_id(2)
    e = pl.program_id(0) * EXPERTS_PER_GROUP + j
    wg, wu, wd = (r[0, 0].astype(BF16) for r in (wg_ref, wu_ref, wd_ref))
    rows = min(MOE_FFN_ROWS, xc_ref.shape[1])

    def chunk(c, carry):
        rs = pl.ds(pl.multiple_of(c * rows, rows), rows)
        out = _ffn(xc_ref[0, rs, :], _expert_column(cw_ref[0, rs, :], e), wg, wu, wd)

        @pl.when(j == 0)
        def _():
            y_ref[0, rs, :] = out

        @pl.when(j > 0)
        def _():
            y_ref[0, rs, :] += out
        return carry

    lax.fori_loop(0, xc_ref.shape[1] // rows, chunk, 0)


def _moe_scatter_kernel(xm_ref, route_ref, y_ref, gf_ref, out_ref, *, final_norm):
    cap = y_ref.shape[1]
    p = jnp.concatenate(_group_onehots(route_ref, cap), axis=0)
    y = y_ref[...].reshape(N_GROUPS * cap, D_MODEL).astype(BF16)
    res = xm_ref[...] + lax.dot_general(p, y, TN, preferred_element_type=F32)
    out_ref[...] = _rmsnorm(res, gf_ref[...]) if final_norm else res


def _moe_dense_kernel(flag_ref, prev_ref, xm_ref, h2_ref, comb3_ref, wg_ref, wu_ref, wd_ref, gf_ref,
                      out_ref, *, final_norm):
    i, e = pl.program_id(0), pl.program_id(1)
    flagged = flag_ref[i] != 0

    @pl.when(jnp.logical_not(flagged) & (e == 0))
    def _():
        out_ref[...] = prev_ref[...]

    @pl.when(flagged)
    def _():
        @pl.when(e == 0)
        def _():
            out_ref[...] = xm_ref[...]

        wg, wu, wd = (r[0, 0].astype(BF16) for r in (wg_ref, wu_ref, wd_ref))
        cw = _expert_column(comb3_ref[...].astype(F32), e)
        out_ref[...] += _ffn(h2_ref[...], cw, wg, wu, wd)

        if final_norm:
            @pl.when(e == N_EXPERTS - 1)
            def _():
                out_ref[...] = _rmsnorm(out_ref[...], gf_ref[...])


def _moe(xm, h2, route, layer, wg, wu, wd, g_final, final_norm):
    n = xm.shape[0]
    tm, cap = MOE_ROW_TILE, MOE_CAP
    nt = n // tm
    comb = route[:N_EXPERTS].T
    pieces, rest = [], comb
    for _ in range(3):
        pieces.append(rest.astype(BF16))
        rest = rest - pieces[-1].astype(F32)
    comb3 = jnp.concatenate(pieces + [jnp.zeros((n, LANES - 3 * N_EXPERTS), BF16)], axis=1)
    gid = route[N_EXPERTS].astype(jnp.int32).reshape(nt, tm)
    counts = jnp.sum(gid[:, :, None] == jnp.arange(N_GROUPS)[None, None, :], axis=1)
    overflow = (jnp.max(counts, axis=1) > cap).astype(jnp.int32)

    tile = lambda width: pl.BlockSpec((tm, width), lambda i: (i, 0))
    route_spec = pl.BlockSpec((ROUTE_ROWS, tm), lambda i: (0, i))
    slots = lambda width: pl.BlockSpec((N_GROUPS, cap, width), lambda i: (0, i, 0))
    xc, cw = pl.pallas_call(
        _moe_compact_kernel,
        grid=(nt,),
        in_specs=[tile(D_MODEL), route_spec, tile(LANES)],
        out_specs=[slots(D_MODEL), slots(LANES)],
        out_shape=[jax.ShapeDtypeStruct((N_GROUPS, nt * cap, D_MODEL), BF16),
                   jax.ShapeDtypeStruct((N_GROUPS, nt * cap, LANES), F32)],
        compiler_params=_params("parallel"),
        name="moe_compact",
    )(h2, route, comb3)

    halves = 4
    blk = nt * cap // halves
    rows_spec = lambda width: pl.BlockSpec((1, blk, width), lambda g, r, j: (g, r, 0))
    expert = lambda rows, cols: pl.BlockSpec(
        (1, 1, rows, cols), lambda g, r, j: (layer, g * EXPERTS_PER_GROUP + j, 0, 0))
    y = pl.pallas_call(
        _moe_expert_kernel,
        grid=(N_GROUPS, halves, EXPERTS_PER_GROUP),
        in_specs=[rows_spec(D_MODEL), rows_spec(LANES), expert(D_MODEL, EXPERT_FF),
                  expert(D_MODEL, EXPERT_FF), expert(EXPERT_FF, D_MODEL)],
        out_specs=rows_spec(D_MODEL),
        out_shape=jax.ShapeDtypeStruct((N_GROUPS, nt * cap, D_MODEL), F32),
        compiler_params=_params("parallel", "parallel", "arbitrary"),
        name="moe_experts",
    )(xc, cw, wg, wu, wd)

    out = pl.pallas_call(
        functools.partial(_moe_scatter_kernel, final_norm=final_norm),
        grid=(nt,),
        in_specs=[tile(D_MODEL), route_spec, slots(D_MODEL), _full((1, D_MODEL))],
        out_specs=tile(D_MODEL),
        out_shape=jax.ShapeDtypeStruct((n, D_MODEL), F32),
        compiler_params=_params("parallel"),
        name="moe_scatter",
    )(xm, route, y, g_final)

    def dense_fix(out):
        tile2 = lambda width: pl.BlockSpec((tm, width), lambda i, e, f: (i, 0))
        expert2 = lambda rows, cols: pl.BlockSpec(
            (1, 1, rows, cols), lambda i, e, f: (layer, jnp.where(f[i] != 0, e, 0), 0, 0))
        grid_spec = pltpu.PrefetchScalarGridSpec(
            num_scalar_prefetch=1,
            grid=(nt, N_EXPERTS),
            in_specs=[tile2(D_MODEL), tile2(D_MODEL), tile2(D_MODEL), tile2(LANES),
                      expert2(D_MODEL, EXPERT_FF), expert2(D_MODEL, EXPERT_FF),
                      expert2(EXPERT_FF, D_MODEL),
                      pl.BlockSpec((1, D_MODEL), lambda i, e, f: (0, 0))],
            out_specs=tile2(D_MODEL))
        return pl.pallas_call(
            functools.partial(_moe_dense_kernel, final_norm=final_norm),
            grid_spec=grid_spec,
            out_shape=jax.ShapeDtypeStruct((n, D_MODEL), F32),
            compiler_params=_params("parallel", "arbitrary"),
            name="moe_dense_fix",
        )(overflow, out, xm, h2, comb3, wg, wu, wd, g_final)

    return lax.cond(jnp.any(overflow != 0), dense_fix, lambda out: out, out)


def _pack_lora(mu_wag, w1, a1, g1, mu_v, v1):
    d = D_MODEL
    if v1 is None:
        mu_v = jnp.zeros((d,), F32)
        v1 = jnp.zeros((d, V_LORA), F32)
    pad = jnp.zeros((d, LORA_HALF - (W_LORA + A_LORA + G_LORA + V_LORA)), F32)
    mus = (mu_wag[0], mu_wag[1], mu_wag[2], mu_v)
    ws = (w1, a1, g1, v1)
    cur = [w * (1.0 - m)[:, None] for w, m in zip(ws, mus)]
    prev = [w * m[:, None] for w, m in zip(ws, mus)]
    return jnp.concatenate(cur + [pad] + prev + [pad], axis=1)


def _pad_rows(w, lo, total=LANES):
    return jnp.zeros((total, w.shape[1]), F32).at[lo:lo + w.shape[0]].set(w)


def kernel(x, positions, norm_mix_g, w_in, gmlp_ln_g, gmlp_ln_b, gmlp_ws, gmlp_bs, rwkv_mu_rkv, rwkv_mu_wag, rwkv_w0, rwkv_w1, rwkv_w2, rwkv_a0, rwkv_a1, rwkv_a2, rwkv_g1, rwkv_g2, rwkv_k_k, rwkv_k_a, rwkv_r_k, rwkv_ln_w, rwkv_ln_b, rwkv_mu_v, rwkv_v0, rwkv_v1, rwkv_v2, w_branch_a, w_branch_b, w_branch_c, w_gate, b_gate, w_out, norm_ffn_g, router_group_w, router_group_b, router_expert_w, router_expert_b, expert_w_gate, expert_w_up, expert_w_down, final_norm_g):
    batch, seq, d = x.shape
    depth = w_in.shape[0]
    n = batch * seq
    x2 = x.reshape(n, d)
    rope = _rope_tables(positions)
    w_in_b, w_gate_b, w_out_b = w_in.astype(BF16), w_gate.astype(BF16), w_out.astype(BF16)
    wba, wbb, wbc = w_branch_a.astype(BF16), w_branch_b.astype(BF16), w_branch_c.astype(BF16)
    v_first = None
    for l in range(depth):
        vres = l > 0
        lora_w = _pack_lora(rwkv_mu_wag[l], rwkv_w1[l], rwkv_a1[l], rwkv_g1[l],
                            rwkv_mu_v[l - 1] if vres else None, rwkv_v1[l - 1] if vres else None)
        ya, qkv0, qkv1, qkv2, rkv, lora = _inproj(
            x2, norm_mix_g[l][None], w_in_b, l, lora_w.astype(BF16), gmlp_ln_g[l][None],
            gmlp_ln_b[l][None], gmlp_ws[l], gmlp_bs[l].T, rope, batch, seq)

        attn = [_dilated_attention(qkv, dil)
                for qkv, (_, dil) in zip((qkv0, qkv1, qkv2), DILATED_PATTERNS)]

        zero = jnp.zeros((C_WIDTH,), F32)
        vecs = jnp.stack([rwkv_w0[l], rwkv_a0[l], rwkv_v0[l - 1] if vres else zero, rwkv_k_k[l],
                          rwkv_k_a[l], rwkv_r_k[l].reshape(C_WIDTH), rwkv_ln_w[l], rwkv_ln_b[l]])
        yc, v_c = _rwkv(rkv, lora, v_first, rwkv_mu_rkv[l], vecs,
                        _pad_rows(rwkv_w2[l], 0), _pad_rows(rwkv_a2[l], W_LORA), rwkv_g2[l],
                        _pad_rows(rwkv_v2[l - 1], 0) if vres else None, batch, seq)
        if l == 0:
            v_first = v_c

        wr = jnp.concatenate([router_group_w[l], router_expert_w[l]], axis=1).T
        wr = jnp.zeros((32, d), F32).at[:wr.shape[0]].set(wr)
        br = jnp.concatenate([router_group_b[l], router_expert_b[l]])
        br = jnp.zeros((32, LANES), F32).at[:br.shape[0], 0].set(br)
        xm, h2, route = _merge(
            x2, norm_mix_g[l][None], ya, attn, yc, l, w_gate_b, b_gate[l][None], wba, wbb, wbc,
            w_out_b, norm_ffn_g[l][None], wr, br, seq)

        x2 = _moe(xm, h2, route, l, expert_w_gate, expert_w_up, expert_w_down, final_norm_g[None],
                  final_norm=(l == depth - 1))
    return x2.reshape(batch, seq, d)
```

```python
import functools

import numpy as np
import jax
import jax.numpy as jnp
from jax import lax
from jax.experimental import pallas as pl
from jax.experimental.pallas import tpu as pltpu

F32 = jnp.float32
BF16 = jnp.bfloat16
HIGHEST = lax.Precision.HIGHEST

D_MODEL = 1024
HEAD_DIM = 64
A_GROUPS = 4
A_WIDTH = 512
CHUNK = 128
DILATED_PATTERNS = ((128, 1), (512, 4), (2048, 16))
B_HEADS_PER_GROUP = 4
B_WIDTH = 768
B_OUT = 256
BLOCK = 128
ROPE_THETA = 500000.0
ROPE_DIMS = 16
C_HEADS = 8
C_WIDTH = 512
W_LORA, A_LORA, V_LORA, G_LORA = 64, 64, 32, 128
RWKV_GN_EPS = HEAD_DIM * 1e-5
IN_COLS = 2 * A_WIDTH + 3 * B_WIDTH + 3 * C_WIDTH
LORA_HALF = 384
LORA_COLS = 2 * LORA_HALF
N_GROUPS = 4
EXPERTS_PER_GROUP = 4
N_EXPERTS = 16
EXPERT_FF = 512
RMS_EPS = 1e-6
LN_EPS = 1e-5

LANES = 128
ROW_TILE = 512
MERGE_ROW_TILE = 512
RWKV_CHUNK = 64
RWKV_BATCH_ROWS = 2
MOE_ROW_TILE = 1024
MOE_CAP = 352
MOE_FIX_TILES = 2
ROUTE_ROWS = 24
VMEM_LIMIT = 56 * 1024 * 1024
NEG_BIG = -1e30


def _params(*sem):
    return pltpu.CompilerParams(dimension_semantics=sem, vmem_limit_bytes=VMEM_LIMIT)


def _full(shape):
    nd = len(shape)
    return pl.BlockSpec(shape, lambda *_: (0,) * nd)


def _gelu_tanh(x):
    return 0.5 * x * (1.0 + jnp.tanh(0.7978845608028654 * (x + 0.044715 * (x * x * x))))


def _sigmoid(x):
    return 1.0 / (1.0 + jnp.exp(-x))


def _rmsnorm(x, g):
    return x * lax.rsqrt(jnp.mean(x * x, axis=-1, keepdims=True) + RMS_EPS) * g


def _rope_table_kernel(pos_ref, consts_ref, c_ref, sa_ref, sb_ref):
    ang = pos_ref[...].astype(F32) * consts_ref[0:1, :]
    c_ref[...] = jnp.cos(ang)
    s = jnp.sin(ang)
    sa_ref[...] = -s * consts_ref[1:2, :]
    sb_ref[...] = s * consts_ref[2:3, :]


def _rope_tables(positions):
    n = positions.size
    half = ROPE_DIMS // 2
    inv_freq = ROPE_THETA ** (-jnp.arange(half, dtype=F32) / half)
    lane = np.arange(LANES) % HEAD_DIM
    in_rope = lane < ROPE_DIMS
    invf_row = jnp.where(in_rope, inv_freq[lane % half], 0.0)
    consts = jnp.zeros((8, LANES), F32)
    consts = consts.at[0].set(invf_row)
    consts = consts.at[1].set(jnp.asarray(lane < half, F32))
    consts = consts.at[2].set(jnp.asarray((lane >= half) & in_rope, F32))
    tm = 1024
    out = jax.ShapeDtypeStruct((n, LANES), F32)
    return pl.pallas_call(
        _rope_table_kernel,
        grid=(n // tm,),
        in_specs=[pl.BlockSpec((tm, 1), lambda i: (i, 0)), _full((8, LANES))],
        out_specs=[pl.BlockSpec((tm, LANES), lambda i: (i, 0))] * 3,
        out_shape=[out, out, out],
        compiler_params=_params("parallel"),
        name="rope_tables",
    )(positions.reshape(n, 1), consts)


def _inproj_kernel(x_ref, g_ref, w_ref, wl_ref, lng_ref, lnb_ref, ws_ref, bst_ref, c_ref, sa_ref,
                   sb_ref, ya_ref, qkv0_ref, qkv1_ref, qkv2_ref, rkv_ref, lora_ref, nat_ref):
    tm = x_ref.shape[0]
    hb = _rmsnorm(x_ref[...], g_ref[...]).astype(BF16)

    def proj(lo, width):
        return jnp.dot(hb, w_ref[0, :, lo:lo + width], preferred_element_type=F32)

    uv = proj(0, 2 * A_WIDTH)
    u = _gelu_tanh(uv[:, :A_WIDTH])
    v = _gelu_tanh(uv[:, A_WIDTH:])
    mu = jnp.mean(v, axis=-1, keepdims=True)
    d = v - mu
    var = jnp.mean(d * d, axis=-1, keepdims=True)
    vn = (d * lax.rsqrt(var + LN_EPS) * lng_ref[...] + lnb_ref[...]).astype(BF16)
    row = lax.broadcasted_iota(jnp.int32, (CHUNK, CHUNK), 0)
    col = lax.broadcasted_iota(jnp.int32, (CHUNK, CHUNK), 1)
    for g in range(A_GROUPS):
        wg = jnp.where(row >= col, ws_ref[g], 0.0).astype(BF16)
        bias = bst_ref[:, g:g + 1]
        cs = slice(g * LANES, (g + 1) * LANES)
        for c in range(tm // CHUNK):
            rs = slice(c * CHUNK, (c + 1) * CHUNK)
            mixed = jnp.dot(wg, vn[rs, cs], preferred_element_type=F32) + bias
            ya_ref[rs, cs] = (u[rs, cs] * mixed).astype(BF16)

    cos, sa, sb = c_ref[...], sa_ref[...], sb_ref[...]
    q_off = 2 * A_WIDTH
    for grp, (qkv_ref, (_, dil)) in enumerate(zip((qkv0_ref, qkv1_ref, qkv2_ref), DILATED_PATTERNS)):
        acc = [proj(q_off + part * B_WIDTH + grp * B_OUT, B_OUT) for part in range(3)]
        for c in range(B_WIDTH // LANES):
            t = acc[c * LANES // B_OUT][:, c * LANES % B_OUT:c * LANES % B_OUT + LANES]
            if c < 2 * B_OUT // LANES:
                t = t * cos + pltpu.roll(t, LANES - 8, 1) * sa + pltpu.roll(t, 8, 1) * sb
            if c < B_OUT // LANES:
                t = t * (HEAD_DIM ** -0.5)
            cs = slice(c * LANES, (c + 1) * LANES)
            if dil == 1:
                qkv_ref[0, 0, :, cs] = t.astype(BF16)
            else:
                nat_ref[c] = t
                for r in range(dil):
                    qkv_ref[0, r, :, cs] = nat_ref[c, pl.ds(r, tm // dil, stride=dil), :].astype(BF16)

    rkv_ref[...] = proj(q_off + 3 * B_WIDTH, 3 * C_WIDTH)
    lora_ref[...] = jnp.dot(hb, wl_ref[...], preferred_element_type=F32)


def _residue_spec(tm, dil, width, tiles_per_seq):
    return pl.BlockSpec((1, dil, tm // dil, width),
                        lambda i: (i // tiles_per_seq, 0, i % tiles_per_seq, 0))


def _inproj(x2, g, w_in, layer, w_lora, ln_g, ln_b, ws, bs_t, rope, batch, seq):
    n = x2.shape[0]
    tm = ROW_TILE
    row = lambda width: pl.BlockSpec((tm, width), lambda i: (i, 0))
    dils = [d for _, d in DILATED_PATTERNS]
    return pl.pallas_call(
        _inproj_kernel,
        grid=(n // tm,),
        in_specs=[row(D_MODEL), _full((1, D_MODEL)),
                  pl.BlockSpec((1, D_MODEL, IN_COLS), lambda i: (layer, 0, 0)),
                  _full((D_MODEL, LORA_COLS)),
                  _full((1, A_WIDTH)), _full((1, A_WIDTH)), _full((A_GROUPS, CHUNK, CHUNK)),
                  _full((CHUNK, A_GROUPS)), row(LANES), row(LANES), row(LANES)],
        out_specs=[row(A_WIDTH)] + [_residue_spec(tm, d, B_WIDTH, seq // tm) for d in dils]
        + [row(3 * C_WIDTH), row(LORA_COLS)],
        out_shape=[jax.ShapeDtypeStruct((n, A_WIDTH), BF16)]
        + [jax.ShapeDtypeStruct((batch, d, seq // d, B_WIDTH), BF16) for d in dils]
        + [jax.ShapeDtypeStruct((n, 3 * C_WIDTH), F32),
           jax.ShapeDtypeStruct((n, LORA_COLS), F32)],
        scratch_shapes=[pltpu.VMEM((B_WIDTH // LANES, tm, LANES), F32)],
        compiler_params=_params("parallel"),
        name="inproj",
    )(x2, g, w_in, w_lora, ln_g, ln_b, ws, bs_t, *rope)


def _attn_block(q, kw, vw, valid):
    lo = lax.broadcasted_iota(jnp.int32, (BLOCK, LANES), 1) < HEAD_DIM
    pairs = range(B_OUT // LANES)
    scores = []
    for p in pairs:
        ps = slice(p * LANES, (p + 1) * LANES)
        qp = q[:, ps]
        zero = jnp.zeros_like(qp)
        for part in (jnp.where(lo, qp, zero), jnp.where(lo, zero, qp)):
            s = lax.dot_general(part, kw[:, ps], NT, preferred_element_type=F32)
            scores.append(jnp.where(valid, s, NEG_BIG))
    m = [jnp.max(s, axis=-1, keepdims=True) for s in scores]
    e = [jnp.exp(s - mx) for s, mx in zip(scores, m)]
    l = [jnp.sum(t, axis=-1, keepdims=True) for t in e]
    pv = [jnp.dot(t.astype(BF16), vw[:, (h // 2) * LANES:(h // 2 + 1) * LANES],
                  preferred_element_type=F32) for h, t in enumerate(e)]
    out = [jnp.where(lo, pv[2 * p] / l[2 * p], pv[2 * p + 1] / l[2 * p + 1]) for p in pairs]
    lse = [mx + jnp.log(t) for mx, t in zip(m, l)]
    lse = [jnp.where(lo, lse[2 * p], lse[2 * p + 1]) for p in pairs]
    return jnp.concatenate(out, axis=-1), jnp.concatenate(lse, axis=-1)


def _attn_kernel(qkv_ref, o_ref, l_ref):
    dil, sub = qkv_ref.shape[1], qkv_ref.shape[2]
    nb = sub // BLOCK
    qs, ks, vs = (slice(i * B_OUT, (i + 1) * B_OUT) for i in range(3))
    qi = lax.broadcasted_iota(jnp.int32, (BLOCK, BLOCK), 0)
    kj = lax.broadcasted_iota(jnp.int32, (BLOCK, BLOCK), 1)
    qi2 = lax.broadcasted_iota(jnp.int32, (BLOCK, 2 * BLOCK), 0)
    kj2 = lax.broadcasted_iota(jnp.int32, (BLOCK, 2 * BLOCK), 1)
    valid2 = (kj2 >= qi2) & (kj2 <= qi2 + BLOCK)

    def residue(r):
        first = pl.ds(0, BLOCK)
        o, l = _attn_block(qkv_ref[0, r, first, qs], qkv_ref[0, r, first, ks],
                           qkv_ref[0, r, first, vs], kj <= qi)
        o_ref[0, r, first, :] = o
        l_ref[0, r, first, :] = l

        def body(n, carry):
            cur = pl.ds(pl.multiple_of(n * BLOCK, BLOCK), BLOCK)
            win = pl.ds(pl.multiple_of((n - 1) * BLOCK, BLOCK), 2 * BLOCK)
            o, l = _attn_block(qkv_ref[0, r, cur, qs], qkv_ref[0, r, win, ks],
                               qkv_ref[0, r, win, vs], valid2)
            o_ref[0, r, cur, :] = o
            l_ref[0, r, cur, :] = l
            return carry

        if nb > 1:
            lax.fori_loop(1, nb, body, 0)

    if dil <= 4:
        for r in range(dil):
            residue(r)
    else:
        def rbody(r, carry):
            residue(r)
            return carry
        lax.fori_loop(0, dil, rbody, 0)


def _dilated_attention(qkv, dilation):
    batch, _, sub, _ = qkv.shape
    ospec = pl.BlockSpec((1, dilation, sub, B_OUT), lambda b: (b, 0, 0, 0))
    oshape = jax.ShapeDtypeStruct((batch, dilation, sub, B_OUT), F32)
    return pl.pallas_call(
        _attn_kernel,
        grid=(batch,),
        in_specs=[pl.BlockSpec((1, dilation, sub, B_WIDTH), lambda b: (b, 0, 0, 0))],
        out_specs=[ospec, ospec],
        out_shape=[oshape, oshape],
        compiler_params=_params("parallel"),
        name=f"dilated_attn_d{dilation}",
    )(qkv)


def _shift_rows(x, prev_row):
    rolled = pltpu.roll(x, 1, 0)
    first = lax.broadcasted_iota(jnp.int32, x.shape, 0) == 0
    return jnp.where(first, prev_row, rolled)


NN = (((1,), (0,)), ((), ()))
NT = (((1,), (1,)), ((), ()))
TN = (((0,), (0,)), ((), ()))


def _dot(a, b, dims=NN):
    return lax.dot_general(a.astype(BF16), b.astype(BF16), dims, preferred_element_type=F32)


def _split(a, pieces):
    out = []
    for _ in range(pieces):
        p = a.astype(BF16)
        out.append(p)
        a = a - p.astype(F32)
    return out


def _dot3(a, b, dims=NN):
    (ah, al), (bh, bl) = _split(a, 2), _split(b, 2)
    return _dot(ah, bh, dims) + _dot(ah, bl, dims) + _dot(al, bh, dims)


def _head_sum(x):
    lo = lax.broadcasted_iota(jnp.int32, (x.shape[0], LANES), 1) < HEAD_DIM
    outs = []
    for c in range(x.shape[1] // LANES):
        xc = x[:, c * LANES:(c + 1) * LANES]
        s_lo = jnp.sum(jnp.where(lo, xc, 0.0), axis=-1, keepdims=True)
        s_hi = jnp.sum(jnp.where(lo, 0.0, xc), axis=-1, keepdims=True)
        outs.append(jnp.where(lo, s_lo, s_hi))
    return jnp.concatenate(outs, axis=-1)


def _rwkv_kernel(*refs, has_vres):
    if has_vres:
        (rkv_ref, lora_ref, vf_ref, mu_ref, vec_ref, w2_ref, a2_ref, g2_ref, v2_ref,
         yc_ref, vout_ref, state_ref, prkv_ref, plora_ref, y_ref) = refs
    else:
        (rkv_ref, lora_ref, mu_ref, vec_ref, w2_ref, a2_ref, g2_ref,
         yc_ref, vout_ref, state_ref, prkv_ref, plora_ref, y_ref) = refs
    nbat, ch = rkv_ref.shape[0], rkv_ref.shape[1]
    rows = nbat * ch
    hd = HEAD_DIM
    row_of = [slice(j * ch, (j + 1) * ch) for j in range(nbat)]

    @pl.when(pl.program_id(1) == 0)
    def _():
        state_ref[...] = jnp.zeros_like(state_ref)
        prkv_ref[...] = jnp.zeros_like(prkv_ref)
        plora_ref[...] = jnp.zeros_like(plora_ref)

    def shifted(x, prev_ref):
        parts = []
        for j, rs in enumerate(row_of):
            parts.append(_shift_rows(x[rs], prev_ref[8 * j:8 * j + 1, :]))
            prev_ref[8 * j:8 * j + 1, :] = x[(j + 1) * ch - 1:(j + 1) * ch, :]
        return jnp.concatenate(parts, axis=0)

    rkv = rkv_ref[...].reshape(rows, 3 * C_WIDTH)
    rkv_prev = shifted(rkv, prkv_ref)
    lora_all = lora_ref[...].reshape(rows, LORA_COLS)
    lora = lora_all[:, :LORA_HALF] + shifted(lora_all[:, LORA_HALF:], plora_ref)

    w0, a0, v0 = vec_ref[0:1, :], vec_ref[1:2, :], vec_ref[2:3, :]
    k_k, k_a, r_k = vec_ref[3:4, :], vec_ref[4:5, :], vec_ref[5:6, :]
    ln_w, ln_b = vec_ref[6:7, :], vec_ref[7:8, :]

    def mix(i):
        cur = rkv[:, i * C_WIDTH:(i + 1) * C_WIDTH]
        prev = rkv_prev[:, i * C_WIDTH:(i + 1) * C_WIDTH]
        return cur + (prev - cur) * mu_ref[i:i + 1, :]

    r, k, v = mix(0), mix(1), mix(2)
    wa = lora[:, 0:LANES]
    z = w0 + _dot3(jnp.tanh(wa), w2_ref[...])
    log_decay = -float(np.exp(-0.5)) * _sigmoid(z)
    a = _sigmoid(a0 + _dot(wa, a2_ref[...]))
    gate = _dot(_sigmoid(lora[:, LANES:2 * LANES]), g2_ref[...])
    if has_vres:
        vmix = _sigmoid(v0 + _dot(lora[:, 2 * LANES:3 * LANES], v2_ref[...]))
        v = v + (vf_ref[...].reshape(rows, C_WIDTH) - v) * vmix
    vout_ref[...] = v.reshape(nbat, ch, C_WIDTH)

    kk = k * k_k
    kk = kk * lax.rsqrt(jnp.maximum(_head_sum(kk * kk), 1e-24))
    k = k * (1.0 + (a - 1.0) * k_a)
    bonus = _head_sum(r * k * r_k) * v
    b = kk * a

    ti = lax.broadcasted_iota(jnp.int32, (rows, rows), 0)
    si = lax.broadcasted_iota(jnp.int32, (rows, rows), 1)
    tri = ((si <= ti) & (si // ch == ti // ch)).astype(BF16)
    cum = functools.reduce(lambda s, t: s + t, [_dot(tri, p) for p in _split(log_decay, 3)])
    ends = [cum[(j + 1) * ch - 1:(j + 1) * ch, :] for j in range(nbat)]
    cum_end = jnp.concatenate([jnp.broadcast_to(e, (ch, C_WIDTH)) for e in ends], axis=0)
    e_neg = jnp.exp(-cum)
    e_end = jnp.exp(cum_end - cum)
    kt = kk * jnp.exp(cum - log_decay)
    rt = r * jnp.exp(cum)
    bt = b * e_neg
    kq = k * e_neg
    kh = k * e_end
    bh = b * e_end
    unit = (lax.broadcasted_iota(jnp.int32, (8, LANES), 0) == 0).astype(BF16)
    w_end_t = [functools.reduce(lambda s, t: s + t,
                                [_dot(jnp.broadcast_to(p, (8, C_WIDTH)), unit, TN)
                                 for p in _split(jnp.exp(e), 2)])
               for e in ends]

    lane = lax.broadcasted_iota(jnp.int32, (ch, LANES), 1)
    lo = lane < hd
    lo2 = lax.broadcasted_iota(jnp.int32, (2 * ch, LANES), 1) < hd
    row4 = lax.broadcasted_iota(jnp.int32, (4 * ch, LANES), 0)
    col4 = lax.broadcasted_iota(jnp.int32, (4 * ch, LANES), 1) % ch
    t4 = row4 % ch
    tri_mask = (col4 < t4) | ((col4 == t4) & ((row4 // ch) % 2 == 1))
    eye_right = (lane - hd == lax.broadcasted_iota(jnp.int32, (ch, LANES), 0)).astype(F32)
    brow = lax.broadcasted_iota(jnp.int32, (LANES, LANES), 0) // hd
    bcol = lax.broadcasted_iota(jnp.int32, (LANES, LANES), 1) // hd
    block_diag = brow == bcol
    zeros = jnp.zeros((ch, LANES), F32)

    npair = C_HEADS // 2
    chains = [(j, p) for j in range(nbat) for p in range(npair)]
    at = lambda t, c: t[row_of[c[0]], c[1] * LANES:(c[1] + 1) * LANES]
    idx = range(len(chains))
    s_bd = [state_ref[j * npair + p] for j, p in chains]
    l2 = [jnp.concatenate([at(kt, c), at(rt, c)], axis=0) for c in chains]
    ss = [_dot(l2[i], s_bd[i]) for i in idx]
    g = [jnp.where(tri_mask,
                   _dot(jnp.concatenate([jnp.where(lo2, l2[i], 0.0), jnp.where(lo2, 0.0, l2[i])],
                                        axis=0),
                        jnp.concatenate([at(bt, c), at(kq, c)], axis=0), NT), 0.0)
         for i, c in zip(idx, chains)]
    xv = [_dot(jnp.concatenate([g[i][0:ch], g[i][2 * ch:3 * ch]], axis=0),
               jnp.concatenate([zeros, at(v, c)], axis=0))
          for i, c in zip(idx, chains)]
    heads = [(i, h) for i in idx for h in range(2)]
    n0 = [-g[i][2 * h * ch:(2 * h + 1) * ch] for i, h in heads]
    w = [_dot(n[:, :ch], jnp.where(lo, n, eye_right)) + jnp.where(lo, 0.0, eye_right) for n in n0]
    for _ in range(int(np.log2(ch)) - 1):
        w = [_dot(t[:, :ch], t) + jnp.where(lo, 0.0, t) for t in w]
    us = [_dot(w[2 * i + h],
               jnp.concatenate([zeros, ss[i][0:ch] + xv[i][h * ch:(h + 1) * ch]], axis=0))
          for i, h in heads]
    u = [jnp.where(lo, us[2 * i], us[2 * i + 1]) for i in idx]
    yv = [_dot(jnp.concatenate([g[i][ch:2 * ch], g[i][3 * ch:4 * ch]], axis=0),
               jnp.concatenate([-u[i], at(v, c)], axis=0))
          for i, c in zip(idx, chains)]
    upd = [_dot(jnp.concatenate([at(kh, c), at(bh, c)], axis=0),
                jnp.concatenate([at(v, c), -u[i]], axis=0), TN)
           for i, c in zip(idx, chains)]
    for i, (j, p) in zip(idx, chains):
        ps = slice(p * LANES, (p + 1) * LANES)
        y_ref[row_of[j], ps] = ss[i][ch:2 * ch] + jnp.where(lo, yv[i][0:ch], yv[i][ch:2 * ch])
        state_ref[j * npair + p] = w_end_t[j][ps, :] * s_bd[i] + jnp.where(block_diag, upd[i], 0.0)

    y = y_ref[...]
    d = y - _head_sum(y) * (1.0 / hd)
    var = _head_sum(d * d) * (1.0 / hd)
    yn = d * lax.rsqrt(var + RWKV_GN_EPS) * ln_w + ln_b
    yc_ref[...] = ((yn + bonus) * gate).astype(BF16).reshape(nbat, ch, C_WIDTH)


def _rwkv(rkv, lora, v_first, mu_rkv, vecs, w2, a2, g2, v2, batch, seq):
    ch = RWKV_CHUNK
    nbat = RWKV_BATCH_ROWS
    has_vres = v_first is not None
    per_seq = lambda t: t.reshape(batch, seq, t.shape[-1])
    blk = lambda width: pl.BlockSpec((nbat, ch, width), lambda i, c: (i, c, 0))
    ins = [per_seq(rkv), per_seq(lora)] + ([per_seq(v_first)] if has_vres else []) \
        + [mu_rkv, vecs, w2, a2, g2] + ([v2] if has_vres else [])
    specs = [blk(3 * C_WIDTH), blk(LORA_COLS)] + ([blk(C_WIDTH)] if has_vres else []) \
        + [_full((3, C_WIDTH)), _full((8, C_WIDTH)), _full((LANES, C_WIDTH)),
           _full((LANES, C_WIDTH)), _full((LANES, C_WIDTH))] \
        + ([_full((LANES, C_WIDTH))] if has_vres else [])
    yc, v_out = pl.pallas_call(
        functools.partial(_rwkv_kernel, has_vres=has_vres),
        grid=(batch // nbat, seq // ch),
        in_specs=specs,
        out_specs=[blk(C_WIDTH), blk(C_WIDTH)],
        out_shape=[jax.ShapeDtypeStruct((batch, seq, C_WIDTH), BF16),
                   jax.ShapeDtypeStruct((batch, seq, C_WIDTH), F32)],
        scratch_shapes=[pltpu.VMEM((nbat * C_HEADS // 2, LANES, LANES), F32),
                        pltpu.VMEM((8 * nbat, 3 * C_WIDTH), F32),
                        pltpu.VMEM((8 * nbat, LORA_HALF), F32),
                        pltpu.VMEM((nbat * ch, C_WIDTH), F32)],
        compiler_params=_params("parallel", "arbitrary"),
        name="rwkv7_chunked",
    )(*ins)
    return yc.reshape(batch * seq, C_WIDTH), v_out.reshape(batch * seq, C_WIDTH)


def _merge_kernel(x_ref, g_ref, ya_ref, o0_ref, l0_ref, o1_ref, l1_ref, o2_ref, l2_ref, yc_ref,
                  wg_ref, bg_ref, wa_ref, wb_ref, wc_ref, wo_ref, g2_ref, wr_ref, br_ref,
                  xm_ref, h2_ref, comb_ref, nat_ref):
    x = x_ref[...]
    tm = x.shape[0]
    hb = _rmsnorm(x, g_ref[...]).astype(BF16)

    def token_order(ref, slot):
        dil = ref.shape[1]
        if dil == 1:
            return ref[0, 0]
        chunks = []
        for c in range(B_OUT // LANES):
            for r in range(dil):
                nat_ref[slot, c, pl.ds(r, tm // dil, stride=dil), :] = \
                    ref[0, r, :, c * LANES:(c + 1) * LANES]
            chunks.append(nat_ref[slot, c])
        return jnp.concatenate(chunks, axis=-1)

    l0, l1, l2 = token_order(l0_ref, 0), token_order(l1_ref, 0), token_order(l2_ref, 1)
    lm = jnp.maximum(jnp.maximum(l0, l1), l2)
    e0, e1, e2 = jnp.exp(l0 - lm), jnp.exp(l1 - lm), jnp.exp(l2 - lm)
    o0, o1, o2 = token_order(o0_ref, 0), token_order(o1_ref, 2), token_order(o2_ref, 3)
    yb = ((e0 * o0 + e1 * o1 + e2 * o2) / (e0 + e1 + e2)).astype(BF16)

    merged = None
    for j, (y, w_ref) in enumerate(((ya_ref[...], wa_ref), (yb, wb_ref), (yc_ref[...], wc_ref))):
        cs = slice(j * D_MODEL, (j + 1) * D_MODEL)
        gate = _sigmoid(jnp.dot(hb, wg_ref[0, :, cs], preferred_element_type=F32) + bg_ref[:, cs])
        term = gate * jnp.dot(y, w_ref[0], preferred_element_type=F32)
        merged = term if merged is None else merged + term
    xm = x + jnp.dot(merged.astype(BF16), wo_ref[0], preferred_element_type=F32)
    xm_ref[...] = xm

    h2 = _rmsnorm(xm, g2_ref[...])
    h2_ref[...] = h2.astype(BF16)

    logits = _dot3(wr_ref[...], h2, NT) + br_ref[:, 0:1]
    gl = [logits[g:g + 1, :] for g in range(N_GROUPS)]
    gmax = functools.reduce(jnp.maximum, gl)
    gsum = functools.reduce(lambda s, t: s + t, [jnp.exp(t - gmax) for t in gl])
    gp = 1.0 / gsum
    taken = jnp.zeros_like(gmax, dtype=jnp.bool_)
    sel = []
    for g in range(N_GROUPS):
        pick = (gl[g] == gmax) & jnp.logical_not(taken)
        sel.append(pick)
        taken = taken | pick
    el = []
    for e in range(EXPERTS_PER_GROUP):
        acc = jnp.zeros_like(gmax)
        for g in range(N_GROUPS):
            r0 = N_GROUPS + g * EXPERTS_PER_GROUP + e
            acc = jnp.where(sel[g], logits[r0:r0 + 1, :], acc)
        el.append(acc)
    emax = functools.reduce(jnp.maximum, el)
    ex = [jnp.exp(t - emax) for t in el]
    esum = functools.reduce(lambda s, t: s + t, ex)
    p = [t / esum for t in ex]
    p1 = functools.reduce(jnp.maximum, p)
    taken = jnp.zeros_like(gmax, dtype=jnp.bool_)
    first = []
    for e in range(EXPERTS_PER_GROUP):
        pick = (p[e] == p1) & jnp.logical_not(taken)
        first.append(pick)
        taken = taken | pick
    rest = [jnp.where(first[e], -1.0, p[e]) for e in range(EXPERTS_PER_GROUP)]
    p2 = functools.reduce(jnp.maximum, rest)
    taken = jnp.zeros_like(gmax, dtype=jnp.bool_)
    second = []
    for e in range(EXPERTS_PER_GROUP):
        pick = (rest[e] == p2) & jnp.logical_not(taken)
        second.append(pick)
        taken = taken | pick
    denom = p1 + p2
    for e in range(EXPERTS_PER_GROUP):
        in_group = jnp.where(first[e], p1 / denom, jnp.where(second[e], p2 / denom, 0.0))
        for g in range(N_GROUPS):
            r0 = g * EXPERTS_PER_GROUP + e
            comb_ref[r0:r0 + 1, :] = jnp.where(sel[g], gp * in_group, 0.0)
    gid = functools.reduce(lambda s, t: s + t,
                           [jnp.where(sel[g], float(g), 0.0) for g in range(N_GROUPS)])
    comb_ref[N_EXPERTS:N_EXPERTS + 1, :] = gid
    comb_ref[N_EXPERTS + 1:, :] = jnp.zeros((ROUTE_ROWS - N_EXPERTS - 1, gid.shape[1]), F32)


def _merge(x2, g, ya, attn, yc, layer, wg, bg, wa, wb, wc, wo, g2, wr, br, seq):
    n = x2.shape[0]
    tm = MERGE_ROW_TILE
    row = lambda width: pl.BlockSpec((tm, width), lambda i: (i, 0))
    of_layer = lambda rows, cols: pl.BlockSpec((1, rows, cols), lambda i: (layer, 0, 0))
    attn_flat = [t for pair in attn for t in pair]
    attn_specs = [_residue_spec(tm, t.shape[1], B_OUT, seq // tm) for t in attn_flat]
    return pl.pallas_call(
        _merge_kernel,
        grid=(n // tm,),
        in_specs=[row(D_MODEL), _full((1, D_MODEL)), row(A_WIDTH)] + attn_specs
        + [row(C_WIDTH), of_layer(D_MODEL, 3 * D_MODEL), _full((1, 3 * D_MODEL)),
           of_layer(A_WIDTH, D_MODEL), of_layer(B_OUT, D_MODEL), of_layer(C_WIDTH, D_MODEL),
           of_layer(D_MODEL, D_MODEL), _full((1, D_MODEL)), _full((32, D_MODEL)), _full((32, LANES))],
        out_specs=[row(D_MODEL), row(D_MODEL), pl.BlockSpec((ROUTE_ROWS, tm), lambda i: (0, i))],
        out_shape=[jax.ShapeDtypeStruct((n, D_MODEL), F32),
                   jax.ShapeDtypeStruct((n, D_MODEL), BF16),
                   jax.ShapeDtypeStruct((ROUTE_ROWS, n), F32)],
        scratch_shapes=[pltpu.VMEM((4, B_OUT // LANES, tm, LANES), F32)],
        compiler_params=_params("parallel"),
        name="merge_router",
    )(x2, g, ya, *attn_flat, yc, wg, bg, wa, wb, wc, wo, g2, wr, br)


def _group_onehots(route_ref, cap):
    t = route_ref.shape[1]
    gid = route_ref[N_EXPERTS:N_EXPERTS + 1, :]
    grow = lax.broadcasted_iota(jnp.int32, (8, t), 0).astype(F32)
    member = gid == grow
    before = (lax.broadcasted_iota(jnp.int32, (t, t), 0)
              < lax.broadcasted_iota(jnp.int32, (t, t), 1)).astype(BF16)
    ranks = jnp.dot(member.astype(BF16), before, preferred_element_type=F32)
    rank = jnp.sum(jnp.where(member, ranks, 0.0), axis=0, keepdims=True)
    slot = lax.broadcasted_iota(jnp.int32, (cap, t), 0).astype(F32)
    return [jnp.where(jnp.where(gid == float(g), rank, -1.0) == slot, 1.0, 0.0).astype(BF16)
            for g in range(N_GROUPS)]


def _expert_column(cw3, e):
    lane = lax.broadcasted_iota(jnp.int32, cw3.shape, 1)
    mine = (lane % N_EXPERTS == e) & (lane < 3 * N_EXPERTS)
    return jnp.sum(jnp.where(mine, cw3, 0.0), axis=-1, keepdims=True)


def _ffn(h, cw, wg, wu, wd):
    gate = jnp.dot(h, wg, preferred_element_type=F32)
    up = jnp.dot(h, wu, preferred_element_type=F32)
    hid = gate * _sigmoid(gate) * up * cw
    return jnp.dot(hid.astype(BF16), wd, preferred_element_type=F32)


def _moe_compact_kernel(h2_ref, route_ref, comb3_ref, xc_ref, cw_ref):
    for g, p in enumerate(_group_onehots(route_ref, xc_ref.shape[1])):
        xc_ref[g] = jnp.dot(p, h2_ref[...], preferred_element_type=F32).astype(BF16)
        cw_ref[g] = jnp.dot(p, comb3_ref[...], preferred_element_type=F32)


def _moe_expert_kernel(xc_ref, cw_ref, wg_ref, wu_ref, wd_ref, y_ref):
    j = pl.program_id(2)
    e = pl.program_id(0) * EXPERTS_PER_GROUP + j
    wg, wu, wd = (r[0, 0].astype(BF16) for r in (wg_ref, wu_ref, wd_ref))
    out = _ffn(xc_ref[0], _expert_column(cw_ref[0], e), wg, wu, wd)

    @pl.when(j == 0)
    def _():
        y_ref[0] = out

    @pl.when(j > 0)
    def _():
        y_ref[0] += out


def _moe_scatter_kernel(xm_ref, route_ref, y_ref, gf_ref, out_ref, *, final_norm):
    cap = y_ref.shape[1]
    p = jnp.concatenate(_group_onehots(route_ref, cap), axis=0)
    y = y_ref[...].reshape(N_GROUPS * cap, D_MODEL).astype(BF16)
    res = xm_ref[...] + lax.dot_general(p, y, TN, preferred_element_type=F32)
    out_ref[...] = _rmsnorm(res, gf_ref[...]) if final_norm else res


def _moe_dense_kernel(ids_ref, cnt_ref, xm_ref, h2_ref, comb3_ref, wg_ref, wu_ref, wd_ref, gf_ref,
                      prev_ref, out_ref, *, final_norm):
    del ids_ref, prev_ref
    e = pl.program_id(1)

    @pl.when(pl.program_id(0) < cnt_ref[0])
    def _():
        @pl.when(e == 0)
        def _():
            out_ref[...] = xm_ref[...]

        wg, wu, wd = (r[0, 0].astype(BF16) for r in (wg_ref, wu_ref, wd_ref))
        cw = _expert_column(comb3_ref[...].astype(F32), e)
        out_ref[...] += _ffn(h2_ref[...], cw, wg, wu, wd)

        if final_norm:
            @pl.when(e == N_EXPERTS - 1)
            def _():
                out_ref[...] = _rmsnorm(out_ref[...], gf_ref[...])


def _combine_pieces(route):
    n = route.shape[1]
    pieces, rest = [], route[:N_EXPERTS].T
    for _ in range(3):
        pieces.append(rest.astype(BF16))
        rest = rest - pieces[-1].astype(F32)
    return jnp.concatenate(pieces + [jnp.zeros((n, LANES - 3 * N_EXPERTS), BF16)], axis=1)


def _overflow_flags(route):
    gid = route[N_EXPERTS].astype(jnp.int32).reshape(-1, MOE_ROW_TILE)
    counts = jnp.sum(gid[:, :, None] == jnp.arange(N_GROUPS)[None, None, :], axis=1)
    return (jnp.max(counts, axis=1) > MOE_CAP).astype(jnp.int32)


def _moe_compacted(xm, h2, route, layer, wg, wu, wd, g_final, final_norm):
    n = xm.shape[0]
    tm, cap = MOE_ROW_TILE, MOE_CAP
    nt = n // tm
    comb3 = _combine_pieces(route)
    overflow = _overflow_flags(route)

    tile = lambda width: pl.BlockSpec((tm, width), lambda i: (i, 0))
    route_spec = pl.BlockSpec((ROUTE_ROWS, tm), lambda i: (0, i))
    slots = lambda width: pl.BlockSpec((N_GROUPS, cap, width), lambda i: (0, i, 0))
    xc, cw = pl.pallas_call(
        _moe_compact_kernel,
        grid=(nt,),
        in_specs=[tile(D_MODEL), route_spec, tile(LANES)],
        out_specs=[slots(D_MODEL), slots(LANES)],
        out_shape=[jax.ShapeDtypeStruct((N_GROUPS, nt * cap, D_MODEL), BF16),
                   jax.ShapeDtypeStruct((N_GROUPS, nt * cap, LANES), F32)],
        compiler_params=_params("parallel"),
        name="moe_compact",
    )(h2, route, comb3)

    halves = 4
    blk = nt * cap // halves
    rows_spec = lambda width: pl.BlockSpec((1, blk, width), lambda g, r, j: (g, r, 0))
    expert = lambda rows, cols: pl.BlockSpec(
        (1, 1, rows, cols), lambda g, r, j: (layer, g * EXPERTS_PER_GROUP + j, 0, 0))
    y = pl.pallas_call(
        _moe_expert_kernel,
        grid=(N_GROUPS, halves, EXPERTS_PER_GROUP),
        in_specs=[rows_spec(D_MODEL), rows_spec(LANES), expert(D_MODEL, EXPERT_FF),
                  expert(D_MODEL, EXPERT_FF), expert(EXPERT_FF, D_MODEL)],
        out_specs=rows_spec(D_MODEL),
        out_shape=jax.ShapeDtypeStruct((N_GROUPS, nt * cap, D_MODEL), F32),
        compiler_params=_params("parallel", "parallel", "arbitrary"),
        name="moe_experts",
    )(xc, cw, wg, wu, wd)

    out = pl.pallas_call(
        functools.partial(_moe_scatter_kernel, final_norm=final_norm),
        grid=(nt,),
        in_specs=[tile(D_MODEL), route_spec, slots(D_MODEL), _full((1, D_MODEL))],
        out_specs=tile(D_MODEL),
        out_shape=jax.ShapeDtypeStruct((n, D_MODEL), F32),
        compiler_params=_params("parallel"),
        name="moe_scatter",
    )(xm, route, y, g_final)

    return out, comb3, overflow


def _moe_dense(ids, count, prev, xm, h2, comb3, layer, wg, wu, wd, g_final, final_norm):
    n = xm.shape[0]
    tm = MOE_ROW_TILE
    tile = lambda width: pl.BlockSpec((tm, width), lambda s, e, ids, cnt: (ids[s], 0))
    expert = lambda rows, cols: pl.BlockSpec(
        (1, 1, rows, cols), lambda s, e, ids, cnt: (layer, jnp.where(s < cnt[0], e, 0), 0, 0))
    grid_spec = pltpu.PrefetchScalarGridSpec(
        num_scalar_prefetch=2,
        grid=(ids.shape[0], N_EXPERTS),
        in_specs=[tile(D_MODEL), tile(D_MODEL), tile(LANES),
                  expert(D_MODEL, EXPERT_FF), expert(D_MODEL, EXPERT_FF), expert(EXPERT_FF, D_MODEL),
                  pl.BlockSpec((1, D_MODEL), lambda s, e, ids, cnt: (0, 0)),
                  pl.BlockSpec(memory_space=pl.ANY)],
        out_specs=tile(D_MODEL))
    return pl.pallas_call(
        functools.partial(_moe_dense_kernel, final_norm=final_norm),
        grid_spec=grid_spec,
        out_shape=jax.ShapeDtypeStruct((n, D_MODEL), F32),
        input_output_aliases={9: 0},
        compiler_params=_params("arbitrary", "arbitrary"),
        name="moe_dense",
    )(ids, count, xm, h2, comb3, wg, wu, wd, g_final, prev)


def _moe(xm, h2, route, layer, wg, wu, wd, g_final, final_norm):
    nt = xm.shape[0] // MOE_ROW_TILE
    dense_args = (layer, wg, wu, wd, g_final, final_norm)

    def compacted(_):
        return _moe_compacted(xm, h2, route, layer, wg, wu, wd, g_final, final_norm)

    def with_fix(_):
        out, comb3, overflow = compacted(None)
        order = jnp.argsort(-overflow, stable=True).astype(jnp.int32)[:MOE_FIX_TILES]
        count = jnp.sum(overflow).astype(jnp.int32)
        ids = jnp.where(jnp.arange(MOE_FIX_TILES) < count, order, order[jnp.maximum(count - 1, 0)])
        return _moe_dense(ids, count[None], out, xm, h2, comb3, *dense_args)

    def all_dense(_):
        comb3 = _combine_pieces(route)
        return _moe_dense(jnp.arange(nt, dtype=jnp.int32), jnp.full((1,), nt, jnp.int32), xm, xm, h2,
                          comb3, *dense_args)

    n_over = jnp.sum(_overflow_flags(route))
    case = jnp.where(n_over == 0, 0, jnp.where(n_over <= MOE_FIX_TILES, 1, 2))
    return lax.switch(case, [lambda _: compacted(None)[0], with_fix, all_dense], None)


def _pack_lora(mu_wag, w1, a1, g1, mu_v, v1):
    d = D_MODEL
    if v1 is None:
        mu_v = jnp.zeros((d,), F32)
        v1 = jnp.zeros((d, V_LORA), F32)
    pad = jnp.zeros((d, LORA_HALF - (W_LORA + A_LORA + G_LORA + V_LORA)), F32)
    mus = (mu_wag[0], mu_wag[1], mu_wag[2], mu_v)
    ws = (w1, a1, g1, v1)
    cur = [w * (1.0 - m)[:, None] for w, m in zip(ws, mus)]
    prev = [w * m[:, None] for w, m in zip(ws, mus)]
    return jnp.concatenate(cur + [pad] + prev + [pad], axis=1)


def _pad_rows(w, lo, total=LANES):
    return jnp.zeros((total, w.shape[1]), F32).at[lo:lo + w.shape[0]].set(w)


def kernel(x, positions, norm_mix_g, w_in, gmlp_ln_g, gmlp_ln_b, gmlp_ws, gmlp_bs, rwkv_mu_rkv, rwkv_mu_wag, rwkv_w0, rwkv_w1, rwkv_w2, rwkv_a0, rwkv_a1, rwkv_a2, rwkv_g1, rwkv_g2, rwkv_k_k, rwkv_k_a, rwkv_r_k, rwkv_ln_w, rwkv_ln_b, rwkv_mu_v, rwkv_v0, rwkv_v1, rwkv_v2, w_branch_a, w_branch_b, w_branch_c, w_gate, b_gate, w_out, norm_ffn_g, router_group_w, router_group_b, router_expert_w, router_expert_b, expert_w_gate, expert_w_up, expert_w_down, final_norm_g):
    batch, seq, d = x.shape
    depth = w_in.shape[0]
    n = batch * seq
    x2 = x.reshape(n, d)
    rope = _rope_tables(positions)
    w_in_b, w_gate_b, w_out_b = w_in.astype(BF16), w_gate.astype(BF16), w_out.astype(BF16)
    wba, wbb, wbc = w_branch_a.astype(BF16), w_branch_b.astype(BF16), w_branch_c.astype(BF16)
    v_first = None
    for l in range(depth):
        vres = l > 0
        lora_w = _pack_lora(rwkv_mu_wag[l], rwkv_w1[l], rwkv_a1[l], rwkv_g1[l],
                            rwkv_mu_v[l - 1] if vres else None, rwkv_v1[l - 1] if vres else None)
        ya, qkv0, qkv1, qkv2, rkv, lora = _inproj(
            x2, norm_mix_g[l][None], w_in_b, l, lora_w.astype(BF16), gmlp_ln_g[l][None],
            gmlp_ln_b[l][None], gmlp_ws[l], gmlp_bs[l].T, rope, batch, seq)

        attn = [_dilated_attention(qkv, dil)
                for qkv, (_, dil) in zip((qkv0, qkv1, qkv2), DILATED_PATTERNS)]

        zero = jnp.zeros((C_WIDTH,), F32)
        vecs = jnp.stack([rwkv_w0[l], rwkv_a0[l], rwkv_v0[l - 1] if vres else zero, rwkv_k_k[l],
                          rwkv_k_a[l], rwkv_r_k[l].reshape(C_WIDTH), rwkv_ln_w[l], rwkv_ln_b[l]])
        yc, v_c = _rwkv(rkv, lora, v_first, rwkv_mu_rkv[l], vecs,
                        _pad_rows(rwkv_w2[l], 0), _pad_rows(rwkv_a2[l], W_LORA), rwkv_g2[l],
                        _pad_rows(rwkv_v2[l - 1], 0) if vres else None, batch, seq)
        if l == 0:
            v_first = v_c

        wr = jnp.concatenate([router_group_w[l], router_expert_w[l]], axis=1).T
        wr = jnp.zeros((32, d), F32).at[:wr.shape[0]].set(wr)
        br = jnp.concatenate([router_group_b[l], router_expert_b[l]])
        br = jnp.zeros((32, LANES), F32).at[:br.shape[0], 0].set(br)
        xm, h2, route = _merge(
            x2, norm_mix_g[l][None], ya, attn, yc, l, w_gate_b, b_gate[l][None], wba, wbb, wbc,
            w_out_b, norm_ffn_g[l][None], wr, br, seq)

        x2 = _moe(xm, h2, route, l, expert_w_gate, expert_w_up, expert_w_down, final_norm_g[None],
                  final_norm=(l == depth - 1))
    return x2.reshape(batch, seq, d)
```

```python
import functools

import numpy as np
import jax
import jax.numpy as jnp
from jax import lax
from jax.experimental import pallas as pl
from jax.experimental.pallas import tpu as pltpu

F32 = jnp.float32
BF16 = jnp.bfloat16
HIGHEST = lax.Precision.HIGHEST

D_MODEL = 1024
HEAD_DIM = 64
A_GROUPS = 4
A_WIDTH = 512
CHUNK = 128
DILATED_PATTERNS = ((128, 1), (512, 4), (2048, 16))
B_HEADS_PER_GROUP = 4
B_WIDTH = 768
B_OUT = 256
BLOCK = 128
ROPE_THETA = 500000.0
ROPE_DIMS = 16
C_HEADS = 8
C_WIDTH = 512
W_LORA, A_LORA, V_LORA, G_LORA = 64, 64, 32, 128
RWKV_GN_EPS = HEAD_DIM * 1e-5
IN_COLS = 2 * A_WIDTH + 3 * B_WIDTH + 3 * C_WIDTH
LORA_HALF = 384
LORA_COLS = 2 * LORA_HALF
N_GROUPS = 4
EXPERTS_PER_GROUP = 4
N_EXPERTS = 16
EXPERT_FF = 512
RMS_EPS = 1e-6
LN_EPS = 1e-5

LANES = 128
ROW_TILE = 512
MERGE_ROW_TILE = 512
RWKV_CHUNK = 64
RWKV_BATCH_ROWS = 8
RWKV_SUB_ROWS = 2
MOE_ROW_TILE = 1024
MOE_CAP = 352
MOE_FIX_TILES = 2
ROUTE_ROWS = 24
VMEM_LIMIT = 56 * 1024 * 1024
NEG_BIG = -1e30


def _params(*sem):
    return pltpu.CompilerParams(dimension_semantics=sem, vmem_limit_bytes=VMEM_LIMIT)


def _full(shape):
    nd = len(shape)
    return pl.BlockSpec(shape, lambda *_: (0,) * nd)


def _gelu_tanh(x):
    return 0.5 * x * (1.0 + jnp.tanh(0.7978845608028654 * (x + 0.044715 * (x * x * x))))


def _sigmoid(x):
    return 1.0 / (1.0 + jnp.exp(-x))


def _rmsnorm(x, g):
    return x * lax.rsqrt(jnp.mean(x * x, axis=-1, keepdims=True) + RMS_EPS) * g


def _rope_table_kernel(pos_ref, consts_ref, c_ref, sa_ref, sb_ref):
    ang = pos_ref[...].astype(F32) * consts_ref[0:1, :]
    c_ref[...] = jnp.cos(ang)
    s = jnp.sin(ang)
    sa_ref[...] = -s * consts_ref[1:2, :]
    sb_ref[...] = s * consts_ref[2:3, :]


def _rope_tables(positions):
    n = positions.size
    half = ROPE_DIMS // 2
    inv_freq = ROPE_THETA ** (-jnp.arange(half, dtype=F32) / half)
    lane = np.arange(LANES) % HEAD_DIM
    in_rope = lane < ROPE_DIMS
    invf_row = jnp.where(in_rope, inv_freq[lane % half], 0.0)
    consts = jnp.zeros((8, LANES), F32)
    consts = consts.at[0].set(invf_row)
    consts = consts.at[1].set(jnp.asarray(lane < half, F32))
    consts = consts.at[2].set(jnp.asarray((lane >= half) & in_rope, F32))
    tm = 1024
    out = jax.ShapeDtypeStruct((n, LANES), F32)
    return pl.pallas_call(
        _rope_table_kernel,
        grid=(n // tm,),
        in_specs=[pl.BlockSpec((tm, 1), lambda i: (i, 0)), _full((8, LANES))],
        out_specs=[pl.BlockSpec((tm, LANES), lambda i: (i, 0))] * 3,
        out_shape=[out, out, out],
        compiler_params=_params("parallel"),
        name="rope_tables",
    )(positions.reshape(n, 1), consts)


def _inproj_kernel(x_ref, g_ref, w_ref, wl_ref, lng_ref, lnb_ref, ws_ref, bst_ref, c_ref, sa_ref,
                   sb_ref, ya_ref, qkv0_ref, qkv1_ref, qkv2_ref, rkv_ref, lora_ref, nat_ref):
    tm = x_ref.shape[0]
    hb = _rmsnorm(x_ref[...], g_ref[...]).astype(BF16)

    def proj(lo, width):
        return jnp.dot(hb, w_ref[0, :, lo:lo + width], preferred_element_type=F32)

    uv = proj(0, 2 * A_WIDTH)
    u = _gelu_tanh(uv[:, :A_WIDTH])
    v = _gelu_tanh(uv[:, A_WIDTH:])
    mu = jnp.mean(v, axis=-1, keepdims=True)
    d = v - mu
    var = jnp.mean(d * d, axis=-1, keepdims=True)
    vn = (d * lax.rsqrt(var + LN_EPS) * lng_ref[...] + lnb_ref[...]).astype(BF16)
    row = lax.broadcasted_iota(jnp.int32, (CHUNK, CHUNK), 0)
    col = lax.broadcasted_iota(jnp.int32, (CHUNK, CHUNK), 1)
    for g in range(A_GROUPS):
        wg = jnp.where(row >= col, ws_ref[g], 0.0).astype(BF16)
        bias = bst_ref[:, g:g + 1]
        cs = slice(g * LANES, (g + 1) * LANES)
        for c in range(tm // CHUNK):
            rs = slice(c * CHUNK, (c + 1) * CHUNK)
            mixed = jnp.dot(wg, vn[rs, cs], preferred_element_type=F32) + bias
            ya_ref[rs, cs] = (u[rs, cs] * mixed).astype(BF16)

    cos, sa, sb = c_ref[...], sa_ref[...], sb_ref[...]
    q_off = 2 * A_WIDTH
    for grp, (qkv_ref, (_, dil)) in enumerate(zip((qkv0_ref, qkv1_ref, qkv2_ref), DILATED_PATTERNS)):
        acc = [proj(q_off + part * B_WIDTH + grp * B_OUT, B_OUT) for part in range(3)]
        for c in range(B_WIDTH // LANES):
            t = acc[c * LANES // B_OUT][:, c * LANES % B_OUT:c * LANES % B_OUT + LANES]
            if c < 2 * B_OUT // LANES:
                t = t * cos + pltpu.roll(t, LANES - 8, 1) * sa + pltpu.roll(t, 8, 1) * sb
            if c < B_OUT // LANES:
                t = t * (HEAD_DIM ** -0.5)
            cs = slice(c * LANES, (c + 1) * LANES)
            if dil == 1:
                qkv_ref[0, 0, :, cs] = t.astype(BF16)
            else:
                nat_ref[c] = t
                for r in range(dil):
                    qkv_ref[0, r, :, cs] = nat_ref[c, pl.ds(r, tm // dil, stride=dil), :].astype(BF16)

    rkv_ref[...] = proj(q_off + 3 * B_WIDTH, 3 * C_WIDTH)
    lora_ref[...] = jnp.dot(hb, wl_ref[...], preferred_element_type=F32)


def _residue_spec(tm, dil, width, tiles_per_seq):
    return pl.BlockSpec((1, dil, tm // dil, width),
                        lambda i: (i // tiles_per_seq, 0, i % tiles_per_seq, 0))


def _inproj(x2, g, w_in, layer, w_lora, ln_g, ln_b, ws, bs_t, rope, batch, seq):
    n = x2.shape[0]
    tm = ROW_TILE
    row = lambda width: pl.BlockSpec((tm, width), lambda i: (i, 0))
    dils = [d for _, d in DILATED_PATTERNS]
    return pl.pallas_call(
        _inproj_kernel,
        grid=(n // tm,),
        in_specs=[row(D_MODEL), _full((1, D_MODEL)),
                  pl.BlockSpec((1, D_MODEL, IN_COLS), lambda i: (layer, 0, 0)),
                  _full((D_MODEL, LORA_COLS)),
                  _full((1, A_WIDTH)), _full((1, A_WIDTH)), _full((A_GROUPS, CHUNK, CHUNK)),
                  _full((CHUNK, A_GROUPS)), row(LANES), row(LANES), row(LANES)],
        out_specs=[row(A_WIDTH)] + [_residue_spec(tm, d, B_WIDTH, seq // tm) for d in dils]
        + [row(3 * C_WIDTH), row(LORA_COLS)],
        out_shape=[jax.ShapeDtypeStruct((n, A_WIDTH), BF16)]
        + [jax.ShapeDtypeStruct((batch, d, seq // d, B_WIDTH), BF16) for d in dils]
        + [jax.ShapeDtypeStruct((n, 3 * C_WIDTH), F32),
           jax.ShapeDtypeStruct((n, LORA_COLS), F32)],
        scratch_shapes=[pltpu.VMEM((B_WIDTH // LANES, tm, LANES), F32)],
        compiler_params=_params("parallel"),
        name="inproj",
    )(x2, g, w_in, w_lora, ln_g, ln_b, ws, bs_t, *rope)


def _attn_block(q, kw, vw, valid):
    lo = lax.broadcasted_iota(jnp.int32, (BLOCK, LANES), 1) < HEAD_DIM
    pairs = range(B_OUT // LANES)
    scores = []
    for p in pairs:
        ps = slice(p * LANES, (p + 1) * LANES)
        qp = q[:, ps]
        zero = jnp.zeros_like(qp)
        for part in (jnp.where(lo, qp, zero), jnp.where(lo, zero, qp)):
            s = lax.dot_general(part, kw[:, ps], NT, preferred_element_type=F32)
            scores.append(jnp.where(valid, s, NEG_BIG))
    m = [jnp.max(s, axis=-1, keepdims=True) for s in scores]
    e = [jnp.exp(s - mx) for s, mx in zip(scores, m)]
    l = [jnp.sum(t, axis=-1, keepdims=True) for t in e]
    pv = [jnp.dot(t.astype(BF16), vw[:, (h // 2) * LANES:(h // 2 + 1) * LANES],
                  preferred_element_type=F32) for h, t in enumerate(e)]
    out = [jnp.where(lo, pv[2 * p] / l[2 * p], pv[2 * p + 1] / l[2 * p + 1]) for p in pairs]
    lse = [mx + jnp.log(t) for mx, t in zip(m, l)]
    lse = [jnp.where(lo, lse[2 * p], lse[2 * p + 1]) for p in pairs]
    return jnp.concatenate(out, axis=-1), jnp.concatenate(lse, axis=-1)


def _attn_kernel(qkv_ref, o_ref, l_ref):
    dil, sub = qkv_ref.shape[1], qkv_ref.shape[2]
    nb = sub // BLOCK
    qs, ks, vs = (slice(i * B_OUT, (i + 1) * B_OUT) for i in range(3))
    qi = lax.broadcasted_iota(jnp.int32, (BLOCK, BLOCK), 0)
    kj = lax.broadcasted_iota(jnp.int32, (BLOCK, BLOCK), 1)
    qi2 = lax.broadcasted_iota(jnp.int32, (BLOCK, 2 * BLOCK), 0)
    kj2 = lax.broadcasted_iota(jnp.int32, (BLOCK, 2 * BLOCK), 1)
    valid2 = (kj2 >= qi2) & (kj2 <= qi2 + BLOCK)

    def residue(r):
        first = pl.ds(0, BLOCK)
        o, l = _attn_block(qkv_ref[0, r, first, qs], qkv_ref[0, r, first, ks],
                           qkv_ref[0, r, first, vs], kj <= qi)
        o_ref[0, r, first, :] = o
        l_ref[0, r, first, :] = l

        def body(n, carry):
            cur = pl.ds(pl.multiple_of(n * BLOCK, BLOCK), BLOCK)
            win = pl.ds(pl.multiple_of((n - 1) * BLOCK, BLOCK), 2 * BLOCK)
            o, l = _attn_block(qkv_ref[0, r, cur, qs], qkv_ref[0, r, win, ks],
                               qkv_ref[0, r, win, vs], valid2)
            o_ref[0, r, cur, :] = o
            l_ref[0, r, cur, :] = l
            return carry

        if nb > 1:
            lax.fori_loop(1, nb, body, 0)

    if dil <= 4:
        for r in range(dil):
            residue(r)
    else:
        def rbody(r, carry):
            residue(r)
            return carry
        lax.fori_loop(0, dil, rbody, 0)


def _dilated_attention(qkv, dilation):
    batch, _, sub, _ = qkv.shape
    ospec = pl.BlockSpec((1, dilation, sub, B_OUT), lambda b: (b, 0, 0, 0))
    oshape = jax.ShapeDtypeStruct((batch, dilation, sub, B_OUT), F32)
    return pl.pallas_call(
        _attn_kernel,
        grid=(batch,),
        in_specs=[pl.BlockSpec((1, dilation, sub, B_WIDTH), lambda b: (b, 0, 0, 0))],
        out_specs=[ospec, ospec],
        out_shape=[oshape, oshape],
        compiler_params=_params("parallel"),
        name=f"dilated_attn_d{dilation}",
    )(qkv)


def _shift_rows(x, prev_row):
    rolled = pltpu.roll(x, 1, 0)
    first = lax.broadcasted_iota(jnp.int32, x.shape, 0) == 0
    return jnp.where(first, prev_row, rolled)


NN = (((1,), (0,)), ((), ()))
NT = (((1,), (1,)), ((), ()))
TN = (((0,), (0,)), ((), ()))


def _dot(a, b, dims=NN):
    return lax.dot_general(a.astype(BF16), b.astype(BF16), dims, preferred_element_type=F32)


def _split(a, pieces):
    out = []
    for _ in range(pieces):
        p = a.astype(BF16)
        out.append(p)
        a = a - p.astype(F32)
    return out


def _dot3(a, b, dims=NN):
    (ah, al), (bh, bl) = _split(a, 2), _split(b, 2)
    return _dot(ah, bh, dims) + _dot(ah, bl, dims) + _dot(al, bh, dims)


def _head_sum(x):
    lo = lax.broadcasted_iota(jnp.int32, (x.shape[0], LANES), 1) < HEAD_DIM
    outs = []
    for c in range(x.shape[1] // LANES):
        xc = x[:, c * LANES:(c + 1) * LANES]
        s_lo = jnp.sum(jnp.where(lo, xc, 0.0), axis=-1, keepdims=True)
        s_hi = jnp.sum(jnp.where(lo, 0.0, xc), axis=-1, keepdims=True)
        outs.append(jnp.where(lo, s_lo, s_hi))
    return jnp.concatenate(outs, axis=-1)


def _finish(gen):
    try:
        while True:
            next(gen)
    except StopIteration as stop:
        return stop.value


def _interleave(first, second):
    values, gens = [None, None], [first, second]
    live = [True, True]
    while any(live):
        for i, gen in enumerate(gens):
            if live[i]:
                try:
                    next(gen)
                except StopIteration as stop:
                    values[i], live[i] = stop.value, False
    return values


def _rwkv_kernel(*refs, has_vres):
    if has_vres:
        (rkv_ref, lora_ref, vf_ref, mu_ref, vec_ref, w2_ref, a2_ref, g2_ref, v2_ref,
         yc_ref, vout_ref, state_ref, prkv_ref, plora_ref, y_ref) = refs
    else:
        (rkv_ref, lora_ref, mu_ref, vec_ref, w2_ref, a2_ref, g2_ref,
         yc_ref, vout_ref, state_ref, prkv_ref, plora_ref, y_ref) = refs
    nbat, ch = rkv_ref.shape[0], rkv_ref.shape[1]
    sub = RWKV_SUB_ROWS
    rows = sub * ch
    hd = HEAD_DIM
    npair = C_HEADS // 2
    row_of = [slice(j * ch, (j + 1) * ch) for j in range(sub)]

    @pl.when(pl.program_id(1) == 0)
    def _():
        state_ref[...] = jnp.zeros_like(state_ref)
        prkv_ref[...] = jnp.zeros_like(prkv_ref)
        plora_ref[...] = jnp.zeros_like(plora_ref)

    w0, a0, v0 = vec_ref[0:1, :], vec_ref[1:2, :], vec_ref[2:3, :]
    k_k, k_a, r_k = vec_ref[3:4, :], vec_ref[4:5, :], vec_ref[5:6, :]
    ln_w, ln_b = vec_ref[6:7, :], vec_ref[7:8, :]

    ti = lax.broadcasted_iota(jnp.int32, (rows, rows), 0)
    si = lax.broadcasted_iota(jnp.int32, (rows, rows), 1)
    tri = ((si <= ti) & (si // ch == ti // ch)).astype(BF16)
    unit = (lax.broadcasted_iota(jnp.int32, (8, LANES), 0) == 0).astype(BF16)
    lane = lax.broadcasted_iota(jnp.int32, (ch, LANES), 1)
    lo = lane < hd
    lo2 = lax.broadcasted_iota(jnp.int32, (2 * ch, LANES), 1) < hd
    row4 = lax.broadcasted_iota(jnp.int32, (4 * ch, LANES), 0)
    col4 = lax.broadcasted_iota(jnp.int32, (4 * ch, LANES), 1) % ch
    t4 = row4 % ch
    tri_mask = (col4 < t4) | ((col4 == t4) & ((row4 // ch) % 2 == 1))
    eye_right = (lane - hd == lax.broadcasted_iota(jnp.int32, (ch, LANES), 0)).astype(F32)
    brow = lax.broadcasted_iota(jnp.int32, (LANES, LANES), 0) // hd
    bcol = lax.broadcasted_iota(jnp.int32, (LANES, LANES), 1) // hd
    block_diag = brow == bcol
    zeros = jnp.zeros((ch, LANES), F32)

    def prepare(grp):
        bs = slice(grp * sub, (grp + 1) * sub)

        def shifted(x, prev_ref):
            parts = []
            for j, rs in enumerate(row_of):
                slot = 8 * (grp * sub + j)
                parts.append(_shift_rows(x[rs], prev_ref[slot:slot + 1, :]))
                prev_ref[slot:slot + 1, :] = x[(j + 1) * ch - 1:(j + 1) * ch, :]
            return jnp.concatenate(parts, axis=0)

        rkv = rkv_ref[bs].reshape(rows, 3 * C_WIDTH)
        rkv_prev = shifted(rkv, prkv_ref)
        lora_all = lora_ref[bs].reshape(rows, LORA_COLS)
        lora = lora_all[:, :LORA_HALF] + shifted(lora_all[:, LORA_HALF:], plora_ref)
        yield

        def mix(i):
            cur = rkv[:, i * C_WIDTH:(i + 1) * C_WIDTH]
            prev = rkv_prev[:, i * C_WIDTH:(i + 1) * C_WIDTH]
            return cur + (prev - cur) * mu_ref[i:i + 1, :]

        r, k, v = mix(0), mix(1), mix(2)
        yield
        wa = lora[:, 0:LANES]
        z = w0 + _dot3(jnp.tanh(wa), w2_ref[...])
        log_decay = -float(np.exp(-0.5)) * _sigmoid(z)
        yield
        a = _sigmoid(a0 + _dot(wa, a2_ref[...]))
        gate = _dot(_sigmoid(lora[:, LANES:2 * LANES]), g2_ref[...])
        if has_vres:
            vmix = _sigmoid(v0 + _dot(lora[:, 2 * LANES:3 * LANES], v2_ref[...]))
            v = v + (vf_ref[bs].reshape(rows, C_WIDTH) - v) * vmix
        vout_ref[bs] = v.reshape(sub, ch, C_WIDTH)
        yield

        kk = k * k_k
        kk = kk * lax.rsqrt(jnp.maximum(_head_sum(kk * kk), 1e-24))
        yield
        k = k * (1.0 + (a - 1.0) * k_a)
        bonus = _head_sum(r * k * r_k) * v
        b = kk * a
        yield

        cum = functools.reduce(lambda s, t: s + t, [_dot(tri, p) for p in _split(log_decay, 3)])
        ends = [cum[(j + 1) * ch - 1:(j + 1) * ch, :] for j in range(sub)]
        cum_end = jnp.concatenate([jnp.broadcast_to(e, (ch, C_WIDTH)) for e in ends], axis=0)
        yield
        e_neg = jnp.exp(-cum)
        e_end = jnp.exp(cum_end - cum)
        kt = kk * jnp.exp(cum - log_decay)
        yield
        rt = r * jnp.exp(cum)
        bt = b * e_neg
        kq = k * e_neg
        yield
        kh = k * e_end
        bh = b * e_end
        w_end_t = [functools.reduce(lambda s, t: s + t,
                                    [_dot(jnp.broadcast_to(p, (8, C_WIDTH)), unit, TN)
                                     for p in _split(jnp.exp(e), 2)])
                   for e in ends]
        yield
        return dict(kt=kt, rt=rt, bt=bt, kq=kq, kh=kh, bh=bh, v=v, bonus=bonus, gate=gate,
                    w_end_t=w_end_t)

    def solve(grp, t):
        chains = [(j, p) for j in range(sub) for p in range(npair)]
        at = lambda x, c: x[row_of[c[0]], c[1] * LANES:(c[1] + 1) * LANES]
        idx = range(len(chains))
        slot_of = lambda c: (grp * sub + c[0]) * npair + c[1]
        s_bd = [state_ref[slot_of(c)] for c in chains]
        l2 = [jnp.concatenate([at(t['kt'], c), at(t['rt'], c)], axis=0) for c in chains]
        ss = [_dot(l2[i], s_bd[i]) for i in idx]
        yield
        g = [jnp.where(tri_mask,
                       _dot(jnp.concatenate([jnp.where(lo2, l2[i], 0.0),
                                             jnp.where(lo2, 0.0, l2[i])], axis=0),
                            jnp.concatenate([at(t['bt'], c), at(t['kq'], c)], axis=0), NT), 0.0)
             for i, c in zip(idx, chains)]
        yield
        xv = [_dot(jnp.concatenate([g[i][0:ch], g[i][2 * ch:3 * ch]], axis=0),
                   jnp.concatenate([zeros, at(t['v'], c)], axis=0))
              for i, c in zip(idx, chains)]
        yield
        heads = [(i, h) for i in idx for h in range(2)]
        n0 = [-g[i][2 * h * ch:(2 * h + 1) * ch] for i, h in heads]
        w = [_dot(n[:, :ch], jnp.where(lo, n, eye_right)) + jnp.where(lo, 0.0, eye_right)
             for n in n0]
        yield
        for _ in range(int(np.log2(ch)) - 1):
            w = [_dot(x[:, :ch], x) + jnp.where(lo, 0.0, x) for x in w]
            yield
        us = [_dot(w[2 * i + h],
                   jnp.concatenate([zeros, ss[i][0:ch] + xv[i][h * ch:(h + 1) * ch]], axis=0))
              for i, h in heads]
        u = [jnp.where(lo, us[2 * i], us[2 * i + 1]) for i in idx]
        yield
        yv = [_dot(jnp.concatenate([g[i][ch:2 * ch], g[i][3 * ch:4 * ch]], axis=0),
                   jnp.concatenate([-u[i], at(t['v'], c)], axis=0))
              for i, c in zip(idx, chains)]
        yield
        upd = [_dot(jnp.concatenate([at(t['kh'], c), at(t['bh'], c)], axis=0),
                    jnp.concatenate([at(t['v'], c), -u[i]], axis=0), TN)
               for i, c in zip(idx, chains)]
        yield
        ys = slice(grp * rows, (grp + 1) * rows)
        for i, c in zip(idx, chains):
            ps = slice(c[1] * LANES, (c[1] + 1) * LANES)
            y_ref[grp * rows + c[0] * ch:grp * rows + (c[0] + 1) * ch, ps] = (
                ss[i][ch:2 * ch] + jnp.where(lo, yv[i][0:ch], yv[i][ch:2 * ch]))
            state_ref[slot_of(c)] = (t['w_end_t'][c[0]][ps, :] * s_bd[i]
                                     + jnp.where(block_diag, upd[i], 0.0))
        yield
        y = y_ref[ys, :]
        d = y - _head_sum(y) * (1.0 / hd)
        var = _head_sum(d * d) * (1.0 / hd)
        yn = d * lax.rsqrt(var + RWKV_GN_EPS) * ln_w + ln_b
        yc_ref[grp * sub:(grp + 1) * sub] = (
            ((yn + t['bonus']) * t['gate']).astype(BF16).reshape(sub, ch, C_WIDTH))

    groups = nbat // sub
    prepared = _finish(prepare(0))
    for grp in range(groups):
        if grp + 1 < groups:
            _, prepared_next = _interleave(solve(grp, prepared), prepare(grp + 1))
            prepared = prepared_next
        else:
            _finish(solve(grp, prepared))


def _rwkv(rkv, lora, v_first, mu_rkv, vecs, w2, a2, g2, v2, batch, seq):
    ch = RWKV_CHUNK
    nbat = RWKV_BATCH_ROWS
    has_vres = v_first is not None
    per_seq = lambda t: t.reshape(batch, seq, t.shape[-1])
    blk = lambda width: pl.BlockSpec((nbat, ch, width), lambda i, c: (i, c, 0))
    ins = [per_seq(rkv), per_seq(lora)] + ([per_seq(v_first)] if has_vres else []) \
        + [mu_rkv, vecs, w2, a2, g2] + ([v2] if has_vres else [])
    specs = [blk(3 * C_WIDTH), blk(LORA_COLS)] + ([blk(C_WIDTH)] if has_vres else []) \
        + [_full((3, C_WIDTH)), _full((8, C_WIDTH)), _full((LANES, C_WIDTH)),
           _full((LANES, C_WIDTH)), _full((LANES, C_WIDTH))] \
        + ([_full((LANES, C_WIDTH))] if has_vres else [])
    yc, v_out = pl.pallas_call(
        functools.partial(_rwkv_kernel, has_vres=has_vres),
        grid=(batch // nbat, seq // ch),
        in_specs=specs,
        out_specs=[blk(C_WIDTH), blk(C_WIDTH)],
        out_shape=[jax.ShapeDtypeStruct((batch, seq, C_WIDTH), BF16),
                   jax.ShapeDtypeStruct((batch, seq, C_WIDTH), F32)],
        scratch_shapes=[pltpu.VMEM((nbat * C_HEADS // 2, LANES, LANES), F32),
                        pltpu.VMEM((8 * nbat, 3 * C_WIDTH), F32),
                        pltpu.VMEM((8 * nbat, LORA_HALF), F32),
                        pltpu.VMEM((nbat * ch, C_WIDTH), F32)],
        compiler_params=_params("parallel", "arbitrary"),
        name="rwkv7_chunked",
    )(*ins)
    return yc.reshape(batch * seq, C_WIDTH), v_out.reshape(batch * seq, C_WIDTH)


def _merge_kernel(x_ref, g_ref, ya_ref, o0_ref, l0_ref, o1_ref, l1_ref, o2_ref, l2_ref, yc_ref,
                  wg_ref, bg_ref, wa_ref, wb_ref, wc_ref, wo_ref, g2_ref, wr_ref, br_ref,
                  xm_ref, h2_ref, comb_ref, nat_ref):
    x = x_ref[...]
    tm = x.shape[0]
    hb = _rmsnorm(x, g_ref[...]).astype(BF16)

    def token_order(ref, slot):
        dil = ref.shape[1]
        if dil == 1:
            return ref[0, 0]
        chunks = []
        for c in range(B_OUT // LANES):
            for r in range(dil):
                nat_ref[slot, c, pl.ds(r, tm // dil, stride=dil), :] = \
                    ref[0, r, :, c * LANES:(c + 1) * LANES]
            chunks.append(nat_ref[slot, c])
        return jnp.concatenate(chunks, axis=-1)

    l0, l1, l2 = token_order(l0_ref, 0), token_order(l1_ref, 0), token_order(l2_ref, 1)
    lm = jnp.maximum(jnp.maximum(l0, l1), l2)
    e0, e1, e2 = jnp.exp(l0 - lm), jnp.exp(l1 - lm), jnp.exp(l2 - lm)
    o0, o1, o2 = token_order(o0_ref, 0), token_order(o1_ref, 2), token_order(o2_ref, 3)
    yb = ((e0 * o0 + e1 * o1 + e2 * o2) / (e0 + e1 + e2)).astype(BF16)

    merged = None
    for j, (y, w_ref) in enumerate(((ya_ref[...], wa_ref), (yb, wb_ref), (yc_ref[...], wc_ref))):
        cs = slice(j * D_MODEL, (j + 1) * D_MODEL)
        gate = _sigmoid(jnp.dot(hb, wg_ref[0, :, cs], preferred_element_type=F32) + bg_ref[:, cs])
        term = gate * jnp.dot(y, w_ref[0], preferred_element_type=F32)
        merged = term if merged is None else merged + term
    xm = x + jnp.dot(merged.astype(BF16), wo_ref[0], preferred_element_type=F32)
    xm_ref[...] = xm

    h2 = _rmsnorm(xm, g2_ref[...])
    h2_ref[...] = h2.astype(BF16)

    logits = _dot3(wr_ref[...], h2, NT) + br_ref[:, 0:1]
    gl = [logits[g:g + 1, :] for g in range(N_GROUPS)]
    gmax = functools.reduce(jnp.maximum, gl)
    gsum = functools.reduce(lambda s, t: s + t, [jnp.exp(t - gmax) for t in gl])
    gp = 1.0 / gsum
    taken = jnp.zeros_like(gmax, dtype=jnp.bool_)
    sel = []
    for g in range(N_GROUPS):
        pick = (gl[g] == gmax) & jnp.logical_not(taken)
        sel.append(pick)
        taken = taken | pick
    el = []
    for e in range(EXPERTS_PER_GROUP):
        acc = jnp.zeros_like(gmax)
        for g in range(N_GROUPS):
            r0 = N_GROUPS + g * EXPERTS_PER_GROUP + e
            acc = jnp.where(sel[g], logits[r0:r0 + 1, :], acc)
        el.append(acc)
    emax = functools.reduce(jnp.maximum, el)
    ex = [jnp.exp(t - emax) for t in el]
    esum = functools.reduce(lambda s, t: s + t, ex)
    p = [t / esum for t in ex]
    p1 = functools.reduce(jnp.maximum, p)
    taken = jnp.zeros_like(gmax, dtype=jnp.bool_)
    first = []
    for e in range(EXPERTS_PER_GROUP):
        pick = (p[e] == p1) & jnp.logical_not(taken)
        first.append(pick)
        taken = taken | pick
    rest = [jnp.where(first[e], -1.0, p[e]) for e in range(EXPERTS_PER_GROUP)]
    p2 = functools.reduce(jnp.maximum, rest)
    taken = jnp.zeros_like(gmax, dtype=jnp.bool_)
    second = []
    for e in range(EXPERTS_PER_GROUP):
        pick = (rest[e] == p2) & jnp.logical_not(taken)
        second.append(pick)
        taken = taken | pick
    denom = p1 + p2
    for e in range(EXPERTS_PER_GROUP):
        in_group = jnp.where(first[e], p1 / denom, jnp.where(second[e], p2 / denom, 0.0))
        for g in range(N_GROUPS):
            r0 = g * EXPERTS_PER_GROUP + e
            comb_ref[r0:r0 + 1, :] = jnp.where(sel[g], gp * in_group, 0.0)
    gid = functools.reduce(lambda s, t: s + t,
                           [jnp.where(sel[g], float(g), 0.0) for g in range(N_GROUPS)])
    comb_ref[N_EXPERTS:N_EXPERTS + 1, :] = gid
    comb_ref[N_EXPERTS + 1:, :] = jnp.zeros((ROUTE_ROWS - N_EXPERTS - 1, gid.shape[1]), F32)


def _merge(x2, g, ya, attn, yc, layer, wg, bg, wa, wb, wc, wo, g2, wr, br, seq):
    n = x2.shape[0]
    tm = MERGE_ROW_TILE
    row = lambda width: pl.BlockSpec((tm, width), lambda i: (i, 0))
    of_layer = lambda rows, cols: pl.BlockSpec((1, rows, cols), lambda i: (layer, 0, 0))
    attn_flat = [t for pair in attn for t in pair]
    attn_specs = [_residue_spec(tm, t.shape[1], B_OUT, seq // tm) for t in attn_flat]
    return pl.pallas_call(
        _merge_kernel,
        grid=(n // tm,),
        in_specs=[row(D_MODEL), _full((1, D_MODEL)), row(A_WIDTH)] + attn_specs
        + [row(C_WIDTH), of_layer(D_MODEL, 3 * D_MODEL), _full((1, 3 * D_MODEL)),
           of_layer(A_WIDTH, D_MODEL), of_layer(B_OUT, D_MODEL), of_layer(C_WIDTH, D_MODEL),
           of_layer(D_MODEL, D_MODEL), _full((1, D_MODEL)), _full((32, D_MODEL)), _full((32, LANES))],
        out_specs=[row(D_MODEL), row(D_MODEL), pl.BlockSpec((ROUTE_ROWS, tm), lambda i: (0, i))],
        out_shape=[jax.ShapeDtypeStruct((n, D_MODEL), F32),
                   jax.ShapeDtypeStruct((n, D_MODEL), BF16),
                   jax.ShapeDtypeStruct((ROUTE_ROWS, n), F32)],
        scratch_shapes=[pltpu.VMEM((4, B_OUT // LANES, tm, LANES), F32)],
        compiler_params=_params("parallel"),
        name="merge_router",
    )(x2, g, ya, *attn_flat, yc, wg, bg, wa, wb, wc, wo, g2, wr, br)


def _group_onehots(route_ref, cap):
    t = route_ref.shape[1]
    gid = route_ref[N_EXPERTS:N_EXPERTS + 1, :]
    grow = lax.broadcasted_iota(jnp.int32, (8, t), 0).astype(F32)
    member = gid == grow
    before = (lax.broadcasted_iota(jnp.int32, (t, t), 0)
              < lax.broadcasted_iota(jnp.int32, (t, t), 1)).astype(BF16)
    ranks = jnp.dot(member.astype(BF16), before, preferred_element_type=F32)
    rank = jnp.sum(jnp.where(member, ranks, 0.0), axis=0, keepdims=True)
    slot = lax.broadcasted_iota(jnp.int32, (cap, t), 0).astype(F32)
    return [jnp.where(jnp.where(gid == float(g), rank, -1.0) == slot, 1.0, 0.0).astype(BF16)
            for g in range(N_GROUPS)]


def _expert_column(cw3, e):
    lane = lax.broadcasted_iota(jnp.int32, cw3.shape, 1)
    mine = (lane % N_EXPERTS == e) & (lane < 3 * N_EXPERTS)
    return jnp.sum(jnp.where(mine, cw3, 0.0), axis=-1, keepdims=True)


def _ffn(h, cw, wg, wu, wd):
    gate = jnp.dot(h, wg, preferred_element_type=F32)
    up = jnp.dot(h, wu, preferred_element_type=F32)
    hid = gate * _sigmoid(gate) * up * cw
    return jnp.dot(hid.astype(BF16), wd, preferred_element_type=F32)


def _moe_compact_kernel(h2_ref, route_ref, comb3_ref, xc_ref, cw_ref):
    for g, p in enumerate(_group_onehots(route_ref, xc_ref.shape[1])):
        xc_ref[g] = jnp.dot(p, h2_ref[...], preferred_element_type=F32).astype(BF16)
        cw_ref[g] = jnp.dot(p, comb3_ref[...], preferred_element_type=F32)


def _moe_expert_kernel(xc_ref, cw_ref, wg_ref, wu_ref, wd_ref, y_ref):
    j = pl.program_id(2)
    e = pl.program_id(0) * EXPERTS_PER_GROUP + j
    wg, wu, wd = (r[0, 0].astype(BF16) for r in (wg_ref, wu_ref, wd_ref))
    out = _ffn(xc_ref[0], _expert_column(cw_ref[0], e), wg, wu, wd)

    @pl.when(j == 0)
    def _():
        y_ref[0] = out

    @pl.when(j > 0)
    def _():
        y_ref[0] += out


def _moe_scatter_kernel(xm_ref, route_ref, y_ref, gf_ref, out_ref, *, final_norm):
    cap = y_ref.shape[1]
    p = jnp.concatenate(_group_onehots(route_ref, cap), axis=0)
    y = y_ref[...].reshape(N_GROUPS * cap, D_MODEL).astype(BF16)
    res = xm_ref[...] + lax.dot_general(p, y, TN, preferred_element_type=F32)
    out_ref[...] = _rmsnorm(res, gf_ref[...]) if final_norm else res


def _moe_dense_kernel(ids_ref, cnt_ref, xm_ref, h2_ref, comb3_ref, wg_ref, wu_ref, wd_ref, gf_ref,
                      prev_ref, out_ref, *, final_norm):
    del ids_ref, prev_ref
    e = pl.program_id(1)

    @pl.when(pl.program_id(0) < cnt_ref[0])
    def _():
        @pl.when(e == 0)
        def _():
            out_ref[...] = xm_ref[...]

        wg, wu, wd = (r[0, 0].astype(BF16) for r in (wg_ref, wu_ref, wd_ref))
        cw = _expert_column(comb3_ref[...].astype(F32), e)
        out_ref[...] += _ffn(h2_ref[...], cw, wg, wu, wd)

        if final_norm:
            @pl.when(e == N_EXPERTS - 1)
            def _():
                out_ref[...] = _rmsnorm(out_ref[...], gf_ref[...])


def _combine_pieces(route):
    n = route.shape[1]
    pieces, rest = [], route[:N_EXPERTS].T
    for _ in range(3):
        pieces.append(rest.astype(BF16))
        rest = rest - pieces[-1].astype(F32)
    return jnp.concatenate(pieces + [jnp.zeros((n, LANES - 3 * N_EXPERTS), BF16)], axis=1)


def _overflow_flags(route):
    gid = route[N_EXPERTS].astype(jnp.int32).reshape(-1, MOE_ROW_TILE)
    counts = jnp.sum(gid[:, :, None] == jnp.arange(N_GROUPS)[None, None, :], axis=1)
    return (jnp.max(counts, axis=1) > MOE_CAP).astype(jnp.int32)


def _moe_compacted(xm, h2, route, layer, wg, wu, wd, g_final, final_norm):
    n = xm.shape[0]
    tm, cap = MOE_ROW_TILE, MOE_CAP
    nt = n // tm
    comb3 = _combine_pieces(route)
    overflow = _overflow_flags(route)

    tile = lambda width: pl.BlockSpec((tm, width), lambda i: (i, 0))
    route_spec = pl.BlockSpec((ROUTE_ROWS, tm), lambda i: (0, i))
    slots = lambda width: pl.BlockSpec((N_GROUPS, cap, width), lambda i: (0, i, 0))
    xc, cw = pl.pallas_call(
        _moe_compact_kernel,
        grid=(nt,),
        in_specs=[tile(D_MODEL), route_spec, tile(LANES)],
        out_specs=[slots(D_MODEL), slots(LANES)],
        out_shape=[jax.ShapeDtypeStruct((N_GROUPS, nt * cap, D_MODEL), BF16),
                   jax.ShapeDtypeStruct((N_GROUPS, nt * cap, LANES), F32)],
        compiler_params=_params("parallel"),
        name="moe_compact",
    )(h2, route, comb3)

    halves = 4
    blk = nt * cap // halves
    rows_spec = lambda width: pl.BlockSpec((1, blk, width), lambda g, r, j: (g, r, 0))
    expert = lambda rows, cols: pl.BlockSpec(
        (1, 1, rows, cols), lambda g, r, j: (layer, g * EXPERTS_PER_GROUP + j, 0, 0))
    y = pl.pallas_call(
        _moe_expert_kernel,
        grid=(N_GROUPS, halves, EXPERTS_PER_GROUP),
        in_specs=[rows_spec(D_MODEL), rows_spec(LANES), expert(D_MODEL, EXPERT_FF),
                  expert(D_MODEL, EXPERT_FF), expert(EXPERT_FF, D_MODEL)],
        out_specs=rows_spec(D_MODEL),
        out_shape=jax.ShapeDtypeStruct((N_GROUPS, nt * cap, D_MODEL), F32),
        compiler_params=_params("parallel", "parallel", "arbitrary"),
        name="moe_experts",
    )(xc, cw, wg, wu, wd)

    out = pl.pallas_call(
        functools.partial(_moe_scatter_kernel, final_norm=final_norm),
        grid=(nt,),
        in_specs=[tile(D_MODEL), route_spec, slots(D_MODEL), _full((1, D_MODEL))],
        out_specs=tile(D_MODEL),
        out_shape=jax.ShapeDtypeStruct((n, D_MODEL), F32),
        compiler_params=_params("parallel"),
        name="moe_scatter",
    )(xm, route, y, g_final)

    return out, comb3, overflow


def _moe_dense(ids, count, prev, xm, h2, comb3, layer, wg, wu, wd, g_final, final_norm):
    n = xm.shape[0]
    tm = MOE_ROW_TILE
    tile = lambda width: pl.BlockSpec((tm, width), lambda s, e, ids, cnt: (ids[s], 0))
    expert = lambda rows, cols: pl.BlockSpec(
        (1, 1, rows, cols), lambda s, e, ids, cnt: (layer, jnp.where(s < cnt[0], e, 0), 0, 0))
    grid_spec = pltpu.PrefetchScalarGridSpec(
        num_scalar_prefetch=2,
        grid=(ids.shape[0], N_EXPERTS),
        in_specs=[tile(D_MODEL), tile(D_MODEL), tile(LANES),
                  expert(D_MODEL, EXPERT_FF), expert(D_MODEL, EXPERT_FF), expert(EXPERT_FF, D_MODEL),
                  pl.BlockSpec((1, D_MODEL), lambda s, e, ids, cnt: (0, 0)),
                  pl.BlockSpec(memory_space=pl.ANY)],
        out_specs=tile(D_MODEL))
    return pl.pallas_call(
        functools.partial(_moe_dense_kernel, final_norm=final_norm),
        grid_spec=grid_spec,
        out_shape=jax.ShapeDtypeStruct((n, D_MODEL), F32),
        input_output_aliases={9: 0},
        compiler_params=_params("arbitrary", "arbitrary"),
        name="moe_dense",
    )(ids, count, xm, h2, comb3, wg, wu, wd, g_final, prev)


def _moe(xm, h2, route, layer, wg, wu, wd, g_final, final_norm):
    nt = xm.shape[0] // MOE_ROW_TILE
    dense_args = (layer, wg, wu, wd, g_final, final_norm)

    def compacted(_):
        return _moe_compacted(xm, h2, route, layer, wg, wu, wd, g_final, final_norm)

    def with_fix(_):
        out, comb3, overflow = compacted(None)
        order = jnp.argsort(-overflow, stable=True).astype(jnp.int32)[:MOE_FIX_TILES]
        count = jnp.sum(overflow).astype(jnp.int32)
        ids = jnp.where(jnp.arange(MOE_FIX_TILES) < count, order, order[jnp.maximum(count - 1, 0)])
        return _moe_dense(ids, count[None], out, xm, h2, comb3, *dense_args)

    def all_dense(_):
        comb3 = _combine_pieces(route)
        return _moe_dense(jnp.arange(nt, dtype=jnp.int32), jnp.full((1,), nt, jnp.int32), xm, xm, h2,
                          comb3, *dense_args)

    n_over = jnp.sum(_overflow_flags(route))
    case = jnp.where(n_over == 0, 0, jnp.where(n_over <= MOE_FIX_TILES, 1, 2))
    return lax.switch(case, [lambda _: compacted(None)[0], with_fix, all_dense], None)


def _pack_lora(mu_wag, w1, a1, g1, mu_v, v1):
    d = D_MODEL
    if v1 is None:
        mu_v = jnp.zeros((d,), F32)
        v1 = jnp.zeros((d, V_LORA), F32)
    pad = jnp.zeros((d, LORA_HALF - (W_LORA + A_LORA + G_LORA + V_LORA)), F32)
    mus = (mu_wag[0], mu_wag[1], mu_wag[2], mu_v)
    ws = (w1, a1, g1, v1)
    cur = [w * (1.0 - m)[:, None] for w, m in zip(ws, mus)]
    prev = [w * m[:, None] for w, m in zip(ws, mus)]
    return jnp.concatenate(cur + [pad] + prev + [pad], axis=1)


def _pad_rows(w, lo, total=LANES):
    return jnp.zeros((total, w.shape[1]), F32).at[lo:lo + w.shape[0]].set(w)


def kernel(x, positions, norm_mix_g, w_in, gmlp_ln_g, gmlp_ln_b, gmlp_ws, gmlp_bs, rwkv_mu_rkv, rwkv_mu_wag, rwkv_w0, rwkv_w1, rwkv_w2, rwkv_a0, rwkv_a1, rwkv_a2, rwkv_g1, rwkv_g2, rwkv_k_k, rwkv_k_a, rwkv_r_k, rwkv_ln_w, rwkv_ln_b, rwkv_mu_v, rwkv_v0, rwkv_v1, rwkv_v2, w_branch_a, w_branch_b, w_branch_c, w_gate, b_gate, w_out, norm_ffn_g, router_group_w, router_group_b, router_expert_w, router_expert_b, expert_w_gate, expert_w_up, expert_w_down, final_norm_g):
    batch, seq, d = x.shape
    depth = w_in.shape[0]
    n = batch * seq
    x2 = x.reshape(n, d)
    rope = _rope_tables(positions)
    w_in_b, w_gate_b, w_out_b = w_in.astype(BF16), w_gate.astype(BF16), w_out.astype(BF16)
    wba, wbb, wbc = w_branch_a.astype(BF16), w_branch_b.astype(BF16), w_branch_c.astype(BF16)
    v_first = None
    for l in range(depth):
        vres = l > 0
        lora_w = _pack_lora(rwkv_mu_wag[l], rwkv_w1[l], rwkv_a1[l], rwkv_g1[l],
                            rwkv_mu_v[l - 1] if vres else None, rwkv_v1[l - 1] if vres else None)
        ya, qkv0, qkv1, qkv2, rkv, lora = _inproj(
            x2, norm_mix_g[l][None], w_in_b, l, lora_w.astype(BF16), gmlp_ln_g[l][None],
            gmlp_ln_b[l][None], gmlp_ws[l], gmlp_bs[l].T, rope, batch, seq)

        attn = [_dilated_attention(qkv, dil)
                for qkv, (_, dil) in zip((qkv0, qkv1, qkv2), DILATED_PATTERNS)]

        zero = jnp.zeros((C_WIDTH,), F32)
        vecs = jnp.stack([rwkv_w0[l], rwkv_a0[l], rwkv_v0[l - 1] if vres else zero, rwkv_k_k[l],
                          rwkv_k_a[l], rwkv_r_k[l].reshape(C_WIDTH), rwkv_ln_w[l], rwkv_ln_b[l]])
        yc, v_c = _rwkv(rkv, lora, v_first, rwkv_mu_rkv[l], vecs,
                        _pad_rows(rwkv_w2[l], 0), _pad_rows(rwkv_a2[l], W_LORA), rwkv_g2[l],
                        _pad_rows(rwkv_v2[l - 1], 0) if vres else None, batch, seq)
        if l == 0:
            v_first = v_c

        wr = jnp.concatenate([router_group_w[l], router_expert_w[l]], axis=1).T
        wr = jnp.zeros((32, d), F32).at[:wr.shape[0]].set(wr)
        br = jnp.concatenate([router_group_b[l], router_expert_b[l]])
        br = jnp.zeros((32, LANES), F32).at[:br.shape[0], 0].set(br)
        xm, h2, route = _merge(
            x2, norm_mix_g[l][None], ya, attn, yc, l, w_gate_b, b_gate[l][None], wba, wbb, wbc,
            w_out_b, norm_ffn_g[l][None], wr, br, seq)

        x2 = _moe(xm, h2, route, l, expert_w_gate, expert_w_up, expert_w_down, final_norm_g[None],
                  final_norm=(l == depth - 1))
    return x2.reshape(batch, seq, d)
```

```python
import functools

import numpy as np
import jax
import jax.numpy as jnp
from jax import lax
from jax.experimental import pallas as pl
from jax.experimental.pallas import tpu as pltpu

F32 = jnp.float32
BF16 = jnp.bfloat16
HIGHEST = lax.Precision.HIGHEST

D_MODEL = 1024
HEAD_DIM = 64
A_GROUPS = 4
A_WIDTH = 512
CHUNK = 128
DILATED_PATTERNS = ((128, 1), (512, 4), (2048, 16))
B_HEADS_PER_GROUP = 4
B_WIDTH = 768
B_OUT = 256
BLOCK = 128
ROPE_THETA = 500000.0
ROPE_DIMS = 16
C_HEADS = 8
C_WIDTH = 512
W_LORA, A_LORA, V_LORA, G_LORA = 64, 64, 32, 128
RWKV_GN_EPS = HEAD_DIM * 1e-5
IN_COLS = 2 * A_WIDTH + 3 * B_WIDTH + 3 * C_WIDTH
LORA_HALF = 384
LORA_COLS = 2 * LORA_HALF
N_GROUPS = 4
EXPERTS_PER_GROUP = 4
N_EXPERTS = 16
EXPERT_FF = 512
RMS_EPS = 1e-6
LN_EPS = 1e-5

LANES = 128
ROW_TILE = 512
MERGE_ROW_TILE = 512
RWKV_CHUNK = 64
RWKV_BATCH_ROWS = 8
RWKV_SUB_ROWS = 2
MOE_ROW_TILE = 1024
MOE_CAP = 352
MOE_FIX_TILES = 2
ROUTE_ROWS = 24
VMEM_LIMIT = 56 * 1024 * 1024
NEG_BIG = -1e30


def _params(*sem):
    return pltpu.CompilerParams(dimension_semantics=sem, vmem_limit_bytes=VMEM_LIMIT)


def _full(shape):
    nd = len(shape)
    return pl.BlockSpec(shape, lambda *_: (0,) * nd)


def _gelu_tanh(x):
    return 0.5 * x * (1.0 + jnp.tanh(0.7978845608028654 * (x + 0.044715 * (x * x * x))))


def _sigmoid(x):
    return 1.0 / (1.0 + jnp.exp(-x))


def _rmsnorm(x, g):
    return x * lax.rsqrt(jnp.mean(x * x, axis=-1, keepdims=True) + RMS_EPS) * g


def _rope_table_kernel(pos_ref, consts_ref, c_ref, sa_ref, sb_ref):
    ang = pos_ref[...].astype(F32) * consts_ref[0:1, :]
    c_ref[...] = jnp.cos(ang)
    s = jnp.sin(ang)
    sa_ref[...] = -s * consts_ref[1:2, :]
    sb_ref[...] = s * consts_ref[2:3, :]


def _rope_tables(positions):
    n = positions.size
    half = ROPE_DIMS // 2
    inv_freq = ROPE_THETA ** (-jnp.arange(half, dtype=F32) / half)
    lane = np.arange(LANES) % HEAD_DIM
    in_rope = lane < ROPE_DIMS
    invf_row = jnp.where(in_rope, inv_freq[lane % half], 0.0)
    consts = jnp.zeros((8, LANES), F32)
    consts = consts.at[0].set(invf_row)
    consts = consts.at[1].set(jnp.asarray(lane < half, F32))
    consts = consts.at[2].set(jnp.asarray((lane >= half) & in_rope, F32))
    tm = 1024
    out = jax.ShapeDtypeStruct((n, LANES), F32)
    return pl.pallas_call(
        _rope_table_kernel,
        grid=(n // tm,),
        in_specs=[pl.BlockSpec((tm, 1), lambda i: (i, 0)), _full((8, LANES))],
        out_specs=[pl.BlockSpec((tm, LANES), lambda i: (i, 0))] * 3,
        out_shape=[out, out, out],
        compiler_params=_params("parallel"),
        name="rope_tables",
    )(positions.reshape(n, 1), consts)


def _inproj_kernel(x_ref, g_ref, w_ref, wl_ref, lng_ref, lnb_ref, ws_ref, bst_ref, c_ref, sa_ref,
                   sb_ref, ya_ref, qkv0_ref, qkv1_ref, qkv2_ref, rkv_ref, lora_ref, nat_ref):
    tm = x_ref.shape[0]
    hb = _rmsnorm(x_ref[...], g_ref[...]).astype(BF16)

    def proj(lo, width):
        return jnp.dot(hb, w_ref[0, :, lo:lo + width], preferred_element_type=F32)

    uv = proj(0, 2 * A_WIDTH)
    u = _gelu_tanh(uv[:, :A_WIDTH])
    v = _gelu_tanh(uv[:, A_WIDTH:])
    mu = jnp.mean(v, axis=-1, keepdims=True)
    d = v - mu
    var = jnp.mean(d * d, axis=-1, keepdims=True)
    vn = (d * lax.rsqrt(var + LN_EPS) * lng_ref[...] + lnb_ref[...]).astype(BF16)
    row = lax.broadcasted_iota(jnp.int32, (CHUNK, CHUNK), 0)
    col = lax.broadcasted_iota(jnp.int32, (CHUNK, CHUNK), 1)
    for g in range(A_GROUPS):
        wg = jnp.where(row >= col, ws_ref[g], 0.0).astype(BF16)
        bias = bst_ref[:, g:g + 1]
        cs = slice(g * LANES, (g + 1) * LANES)
        for c in range(tm // CHUNK):
            rs = slice(c * CHUNK, (c + 1) * CHUNK)
            mixed = jnp.dot(wg, vn[rs, cs], preferred_element_type=F32) + bias
            ya_ref[rs, cs] = (u[rs, cs] * mixed).astype(BF16)

    cos, sa, sb = c_ref[...], sa_ref[...], sb_ref[...]
    q_off = 2 * A_WIDTH
    for grp, (qkv_ref, (_, dil)) in enumerate(zip((qkv0_ref, qkv1_ref, qkv2_ref), DILATED_PATTERNS)):
        acc = [proj(q_off + part * B_WIDTH + grp * B_OUT, B_OUT) for part in range(3)]
        for c in range(B_WIDTH // LANES):
            t = acc[c * LANES // B_OUT][:, c * LANES % B_OUT:c * LANES % B_OUT + LANES]
            if c < 2 * B_OUT // LANES:
                t = t * cos + pltpu.roll(t, LANES - 8, 1) * sa + pltpu.roll(t, 8, 1) * sb
            if c < B_OUT // LANES:
                t = t * (HEAD_DIM ** -0.5)
            cs = slice(c * LANES, (c + 1) * LANES)
            if dil == 1:
                qkv_ref[0, 0, :, cs] = t.astype(BF16)
            else:
                nat_ref[c] = t
                for r in range(dil):
                    qkv_ref[0, r, :, cs] = nat_ref[c, pl.ds(r, tm // dil, stride=dil), :].astype(BF16)

    rkv_ref[...] = proj(q_off + 3 * B_WIDTH, 3 * C_WIDTH)
    lora_ref[...] = jnp.dot(hb, wl_ref[...], preferred_element_type=F32)


def _residue_spec(tm, dil, width, tiles_per_seq):
    return pl.BlockSpec((1, dil, tm // dil, width),
                        lambda i: (i // tiles_per_seq, 0, i % tiles_per_seq, 0))


def _inproj(x2, g, w_in, layer, w_lora, ln_g, ln_b, ws, bs_t, rope, batch, seq):
    n = x2.shape[0]
    tm = ROW_TILE
    row = lambda width: pl.BlockSpec((tm, width), lambda i: (i, 0))
    dils = [d for _, d in DILATED_PATTERNS]
    return pl.pallas_call(
        _inproj_kernel,
        grid=(n // tm,),
        in_specs=[row(D_MODEL), _full((1, D_MODEL)),
                  pl.BlockSpec((1, D_MODEL, IN_COLS), lambda i: (layer, 0, 0)),
                  _full((D_MODEL, LORA_COLS)),
                  _full((1, A_WIDTH)), _full((1, A_WIDTH)), _full((A_GROUPS, CHUNK, CHUNK)),
                  _full((CHUNK, A_GROUPS)), row(LANES), row(LANES), row(LANES)],
        out_specs=[row(A_WIDTH)] + [_residue_spec(tm, d, B_WIDTH, seq // tm) for d in dils]
        + [row(3 * C_WIDTH), row(LORA_COLS)],
        out_shape=[jax.ShapeDtypeStruct((n, A_WIDTH), BF16)]
        + [jax.ShapeDtypeStruct((batch, d, seq // d, B_WIDTH), BF16) for d in dils]
        + [jax.ShapeDtypeStruct((n, 3 * C_WIDTH), F32),
           jax.ShapeDtypeStruct((n, LORA_COLS), F32)],
        scratch_shapes=[pltpu.VMEM((B_WIDTH // LANES, tm, LANES), F32)],
        compiler_params=_params("parallel"),
        name="inproj",
    )(x2, g, w_in, w_lora, ln_g, ln_b, ws, bs_t, *rope)


def _attn_blocks(blocks):
    lo = lax.broadcasted_iota(jnp.int32, (BLOCK, LANES), 1) < HEAD_DIM
    pairs = range(B_OUT // LANES)
    heads = range(B_HEADS_PER_GROUP)
    scores = []
    for q, kw, _, valid in blocks:
        for p in pairs:
            ps = slice(p * LANES, (p + 1) * LANES)
            qp = q[:, ps]
            zero = jnp.zeros_like(qp)
            for part in (jnp.where(lo, qp, zero), jnp.where(lo, zero, qp)):
                s = lax.dot_general(part, kw[:, ps], NT, preferred_element_type=F32)
                scores.append(jnp.where(valid, s, NEG_BIG))
    m = [jnp.max(s, axis=-1, keepdims=True) for s in scores]
    e = [jnp.exp(s - mx) for s, mx in zip(scores, m)]
    l = [jnp.sum(t, axis=-1, keepdims=True) for t in e]
    pv = [jnp.dot(e[len(heads) * b + h].astype(BF16),
                  blocks[b][2][:, (h // 2) * LANES:(h // 2 + 1) * LANES],
                  preferred_element_type=F32)
          for b in range(len(blocks)) for h in heads]
    lse = [mx + jnp.log(t) for mx, t in zip(m, l)]
    results = []
    for b in range(len(blocks)):
        at = lambda p, h: len(heads) * b + 2 * p + h
        out = [jnp.where(lo, pv[at(p, 0)] / l[at(p, 0)], pv[at(p, 1)] / l[at(p, 1)]) for p in pairs]
        ls = [jnp.where(lo, lse[at(p, 0)], lse[at(p, 1)]) for p in pairs]
        results.append((jnp.concatenate(out, axis=-1), jnp.concatenate(ls, axis=-1)))
    return results


def _attn_kernel(qkv_ref, o_ref, l_ref):
    dil, sub = qkv_ref.shape[1], qkv_ref.shape[2]
    nb = sub // BLOCK
    qs, ks, vs = (slice(i * B_OUT, (i + 1) * B_OUT) for i in range(3))
    qi = lax.broadcasted_iota(jnp.int32, (BLOCK, BLOCK), 0)
    kj = lax.broadcasted_iota(jnp.int32, (BLOCK, BLOCK), 1)
    qi2 = lax.broadcasted_iota(jnp.int32, (BLOCK, 2 * BLOCK), 0)
    kj2 = lax.broadcasted_iota(jnp.int32, (BLOCK, 2 * BLOCK), 1)
    valid2 = (kj2 >= qi2) & (kj2 <= qi2 + BLOCK)

    def first_block(r):
        rows = pl.ds(0, BLOCK)
        return rows, (qkv_ref[0, r, rows, qs], qkv_ref[0, r, rows, ks], qkv_ref[0, r, rows, vs],
                      kj <= qi)

    def later_block(r, n):
        rows = pl.ds(pl.multiple_of(n * BLOCK, BLOCK), BLOCK)
        win = pl.ds(pl.multiple_of((n - 1) * BLOCK, BLOCK), 2 * BLOCK)
        return rows, (qkv_ref[0, r, rows, qs], qkv_ref[0, r, win, ks], qkv_ref[0, r, win, vs], valid2)

    def run(work):
        for (o, l), (r, rows, _) in zip(_attn_blocks([w[2] for w in work]), work):
            o_ref[0, r, rows, :] = o
            l_ref[0, r, rows, :] = l

    if nb == 1:
        def body(i, carry):
            run([(r,) + first_block(r) for r in (2 * i, 2 * i + 1)])
            return carry
        lax.fori_loop(0, dil // 2, body, 0)
    else:
        for r in range(dil):
            run([(r,) + first_block(r), (r,) + later_block(r, 1)])

            def body(i, carry):
                run([(r,) + later_block(r, n) for n in (2 * i, 2 * i + 1)])
                return carry
            lax.fori_loop(1, nb // 2, body, 0)


def _dilated_attention(qkv, dilation):
    batch, _, sub, _ = qkv.shape
    ospec = pl.BlockSpec((1, dilation, sub, B_OUT), lambda b: (b, 0, 0, 0))
    oshape = jax.ShapeDtypeStruct((batch, dilation, sub, B_OUT), F32)
    return pl.pallas_call(
        _attn_kernel,
        grid=(batch,),
        in_specs=[pl.BlockSpec((1, dilation, sub, B_WIDTH), lambda b: (b, 0, 0, 0))],
        out_specs=[ospec, ospec],
        out_shape=[oshape, oshape],
        compiler_params=_params("parallel"),
        name=f"dilated_attn_d{dilation}",
    )(qkv)


def _shift_rows(x, prev_row):
    rolled = pltpu.roll(x, 1, 0)
    first = lax.broadcasted_iota(jnp.int32, x.shape, 0) == 0
    return jnp.where(first, prev_row, rolled)


NN = (((1,), (0,)), ((), ()))
NT = (((1,), (1,)), ((), ()))
TN = (((0,), (0,)), ((), ()))


def _dot(a, b, dims=NN):
    return lax.dot_general(a.astype(BF16), b.astype(BF16), dims, preferred_element_type=F32)


def _split(a, pieces):
    out = []
    for _ in range(pieces):
        p = a.astype(BF16)
        out.append(p)
        a = a - p.astype(F32)
    return out


def _dot3(a, b, dims=NN):
    (ah, al), (bh, bl) = _split(a, 2), _split(b, 2)
    return _dot(ah, bh, dims) + _dot(ah, bl, dims) + _dot(al, bh, dims)


def _head_sum(x):
    lo = lax.broadcasted_iota(jnp.int32, (x.shape[0], LANES), 1) < HEAD_DIM
    outs = []
    for c in range(x.shape[1] // LANES):
        xc = x[:, c * LANES:(c + 1) * LANES]
        s_lo = jnp.sum(jnp.where(lo, xc, 0.0), axis=-1, keepdims=True)
        s_hi = jnp.sum(jnp.where(lo, 0.0, xc), axis=-1, keepdims=True)
        outs.append(jnp.where(lo, s_lo, s_hi))
    return jnp.concatenate(outs, axis=-1)


def _finish(gen):
    try:
        while True:
            next(gen)
    except StopIteration as stop:
        return stop.value


def _interleave(first, second):
    values, gens = [None, None], [first, second]
    live = [True, True]
    while any(live):
        for i, gen in enumerate(gens):
            if live[i]:
                try:
                    next(gen)
                except StopIteration as stop:
                    values[i], live[i] = stop.value, False
    return values


def _rwkv_kernel(*refs, has_vres):
    if has_vres:
        (rkv_ref, lora_ref, vf_ref, mu_ref, vec_ref, w2_ref, a2_ref, g2_ref, v2_ref,
         yc_ref, vout_ref, state_ref, prkv_ref, plora_ref, y_ref) = refs
    else:
        (rkv_ref, lora_ref, mu_ref, vec_ref, w2_ref, a2_ref, g2_ref,
         yc_ref, vout_ref, state_ref, prkv_ref, plora_ref, y_ref) = refs
    nbat, ch = rkv_ref.shape[0], rkv_ref.shape[1]
    sub = RWKV_SUB_ROWS
    rows = sub * ch
    hd = HEAD_DIM
    npair = C_HEADS // 2
    row_of = [slice(j * ch, (j + 1) * ch) for j in range(sub)]

    @pl.when(pl.program_id(1) == 0)
    def _():
        state_ref[...] = jnp.zeros_like(state_ref)
        prkv_ref[...] = jnp.zeros_like(prkv_ref)
        plora_ref[...] = jnp.zeros_like(plora_ref)

    w0, a0, v0 = vec_ref[0:1, :], vec_ref[1:2, :], vec_ref[2:3, :]
    k_k, k_a, r_k = vec_ref[3:4, :], vec_ref[4:5, :], vec_ref[5:6, :]
    ln_w, ln_b = vec_ref[6:7, :], vec_ref[7:8, :]

    ti = lax.broadcasted_iota(jnp.int32, (rows, rows), 0)
    si = lax.broadcasted_iota(jnp.int32, (rows, rows), 1)
    tri = ((si <= ti) & (si // ch == ti // ch)).astype(BF16)
    unit = (lax.broadcasted_iota(jnp.int32, (8, LANES), 0) == 0).astype(BF16)
    lane = lax.broadcasted_iota(jnp.int32, (ch, LANES), 1)
    lo = lane < hd
    lo2 = lax.broadcasted_iota(jnp.int32, (2 * ch, LANES), 1) < hd
    row4 = lax.broadcasted_iota(jnp.int32, (4 * ch, LANES), 0)
    col4 = lax.broadcasted_iota(jnp.int32, (4 * ch, LANES), 1) % ch
    t4 = row4 % ch
    tri_mask = (col4 < t4) | ((col4 == t4) & ((row4 // ch) % 2 == 1))
    eye_right = (lane - hd == lax.broadcasted_iota(jnp.int32, (ch, LANES), 0)).astype(F32)
    brow = lax.broadcasted_iota(jnp.int32, (LANES, LANES), 0) // hd
    bcol = lax.broadcasted_iota(jnp.int32, (LANES, LANES), 1) // hd
    block_diag = brow == bcol
    zeros = jnp.zeros((ch, LANES), F32)

    def prepare(grp):
        bs = slice(grp * sub, (grp + 1) * sub)

        def shifted(x, prev_ref):
            parts = []
            for j, rs in enumerate(row_of):
                slot = 8 * (grp * sub + j)
                parts.append(_shift_rows(x[rs], prev_ref[slot:slot + 1, :]))
                prev_ref[slot:slot + 1, :] = x[(j + 1) * ch - 1:(j + 1) * ch, :]
            return jnp.concatenate(parts, axis=0)

        rkv = rkv_ref[bs].reshape(rows, 3 * C_WIDTH)
        rkv_prev = shifted(rkv, prkv_ref)
        lora_all = lora_ref[bs].reshape(rows, LORA_COLS)
        lora = lora_all[:, :LORA_HALF] + shifted(lora_all[:, LORA_HALF:], plora_ref)
        yield

        def mix(i):
            cur = rkv[:, i * C_WIDTH:(i + 1) * C_WIDTH]
            prev = rkv_prev[:, i * C_WIDTH:(i + 1) * C_WIDTH]
            return cur + (prev - cur) * mu_ref[i:i + 1, :]

        r, k, v = mix(0), mix(1), mix(2)
        yield
        wa = lora[:, 0:LANES]
        z = w0 + _dot3(jnp.tanh(wa), w2_ref[...])
        log_decay = -float(np.exp(-0.5)) * _sigmoid(z)
        yield
        a = _sigmoid(a0 + _dot(wa, a2_ref[...]))
        gate = _dot(_sigmoid(lora[:, LANES:2 * LANES]), g2_ref[...])
        if has_vres:
            vmix = _sigmoid(v0 + _dot(lora[:, 2 * LANES:3 * LANES], v2_ref[...]))
            v = v + (vf_ref[bs].reshape(rows, C_WIDTH) - v) * vmix
        vout_ref[bs] = v.reshape(sub, ch, C_WIDTH)
        yield

        kk = k * k_k
        kk = kk * lax.rsqrt(jnp.maximum(_head_sum(kk * kk), 1e-24))
        yield
        k = k * (1.0 + (a - 1.0) * k_a)
        bonus = _head_sum(r * k * r_k) * v
        b = kk * a
        yield

        cum = functools.reduce(lambda s, t: s + t, [_dot(tri, p) for p in _split(log_decay, 3)])
        ends = [cum[(j + 1) * ch - 1:(j + 1) * ch, :] for j in range(sub)]
        cum_end = jnp.concatenate([jnp.broadcast_to(e, (ch, C_WIDTH)) for e in ends], axis=0)
        yield
        e_neg = jnp.exp(-cum)
        e_end = jnp.exp(cum_end - cum)
        kt = kk * jnp.exp(cum - log_decay)
        yield
        rt = r * jnp.exp(cum)
        bt = b * e_neg
        kq = k * e_neg
        yield
        kh = k * e_end
        bh = b * e_end
        w_end_t = [functools.reduce(lambda s, t: s + t,
                                    [_dot(jnp.broadcast_to(p, (8, C_WIDTH)), unit, TN)
                                     for p in _split(jnp.exp(e), 2)])
                   for e in ends]
        yield
        return dict(kt=kt, rt=rt, bt=bt, kq=kq, kh=kh, bh=bh, v=v, bonus=bonus, gate=gate,
                    w_end_t=w_end_t)

    def solve(grp, t):
        chains = [(j, p) for j in range(sub) for p in range(npair)]
        at = lambda x, c: x[row_of[c[0]], c[1] * LANES:(c[1] + 1) * LANES]
        idx = range(len(chains))
        slot_of = lambda c: (grp * sub + c[0]) * npair + c[1]
        s_bd = [state_ref[slot_of(c)] for c in chains]
        l2 = [jnp.concatenate([at(t['kt'], c), at(t['rt'], c)], axis=0) for c in chains]
        ss = [_dot(l2[i], s_bd[i]) for i in idx]
        yield
        g = [jnp.where(tri_mask,
                       _dot(jnp.concatenate([jnp.where(lo2, l2[i], 0.0),
                                             jnp.where(lo2, 0.0, l2[i])], axis=0),
                            jnp.concatenate([at(t['bt'], c), at(t['kq'], c)], axis=0), NT), 0.0)
             for i, c in zip(idx, chains)]
        yield
        xv = [_dot(jnp.concatenate([g[i][0:ch], g[i][2 * ch:3 * ch]], axis=0),
                   jnp.concatenate([zeros, at(t['v'], c)], axis=0))
              for i, c in zip(idx, chains)]
        yield
        heads = [(i, h) for i in idx for h in range(2)]
        n0 = [-g[i][2 * h * ch:(2 * h + 1) * ch] for i, h in heads]
        w = [_dot(n[:, :ch], jnp.where(lo, n, eye_right)) + jnp.where(lo, 0.0, eye_right)
             for n in n0]
        yield
        for _ in range(int(np.log2(ch)) - 1):
            w = [_dot(x[:, :ch], x) + jnp.where(lo, 0.0, x) for x in w]
            yield
        us = [_dot(w[2 * i + h],
                   jnp.concatenate([zeros, ss[i][0:ch] + xv[i][h * ch:(h + 1) * ch]], axis=0))
              for i, h in heads]
        u = [jnp.where(lo, us[2 * i], us[2 * i + 1]) for i in idx]
        yield
        yv = [_dot(jnp.concatenate([g[i][ch:2 * ch], g[i][3 * ch:4 * ch]], axis=0),
                   jnp.concatenate([-u[i], at(t['v'], c)], axis=0))
              for i, c in zip(idx, chains)]
        yield
        upd = [_dot(jnp.concatenate([at(t['kh'], c), at(t['bh'], c)], axis=0),
                    jnp.concatenate([at(t['v'], c), -u[i]], axis=0), TN)
               for i, c in zip(idx, chains)]
        yield
        ys = slice(grp * rows, (grp + 1) * rows)
        for i, c in zip(idx, chains):
            ps = slice(c[1] * LANES, (c[1] + 1) * LANES)
            y_ref[grp * rows + c[0] * ch:grp * rows + (c[0] + 1) * ch, ps] = (
                ss[i][ch:2 * ch] + jnp.where(lo, yv[i][0:ch], yv[i][ch:2 * ch]))
            state_ref[slot_of(c)] = (t['w_end_t'][c[0]][ps, :] * s_bd[i]
                                     + jnp.where(block_diag, upd[i], 0.0))
        yield
        y = y_ref[ys, :]
        d = y - _head_sum(y) * (1.0 / hd)
        var = _head_sum(d * d) * (1.0 / hd)
        yn = d * lax.rsqrt(var + RWKV_GN_EPS) * ln_w + ln_b
        yc_ref[grp * sub:(grp + 1) * sub] = (
            ((yn + t['bonus']) * t['gate']).astype(BF16).reshape(sub, ch, C_WIDTH))

    groups = nbat // sub
    prepared = _finish(prepare(0))
    for grp in range(groups):
        if grp + 1 < groups:
            _, prepared_next = _interleave(solve(grp, prepared), prepare(grp + 1))
            prepared = prepared_next
        else:
            _finish(solve(grp, prepared))


def _rwkv(rkv, lora, v_first, mu_rkv, vecs, w2, a2, g2, v2, batch, seq):
    ch = RWKV_CHUNK
    nbat = RWKV_BATCH_ROWS
    has_vres = v_first is not None
    per_seq = lambda t: t.reshape(batch, seq, t.shape[-1])
    blk = lambda width: pl.BlockSpec((nbat, ch, width), lambda i, c: (i, c, 0))
    ins = [per_seq(rkv), per_seq(lora)] + ([per_seq(v_first)] if has_vres else []) \
        + [mu_rkv, vecs, w2, a2, g2] + ([v2] if has_vres else [])
    specs = [blk(3 * C_WIDTH), blk(LORA_COLS)] + ([blk(C_WIDTH)] if has_vres else []) \
        + [_full((3, C_WIDTH)), _full((8, C_WIDTH)), _full((LANES, C_WIDTH)),
           _full((LANES, C_WIDTH)), _full((LANES, C_WIDTH))] \
        + ([_full((LANES, C_WIDTH))] if has_vres else [])
    yc, v_out = pl.pallas_call(
        functools.partial(_rwkv_kernel, has_vres=has_vres),
        grid=(batch // nbat, seq // ch),
        in_specs=specs,
        out_specs=[blk(C_WIDTH), blk(C_WIDTH)],
        out_shape=[jax.ShapeDtypeStruct((batch, seq, C_WIDTH), BF16),
                   jax.ShapeDtypeStruct((batch, seq, C_WIDTH), F32)],
        scratch_shapes=[pltpu.VMEM((nbat * C_HEADS // 2, LANES, LANES), F32),
                        pltpu.VMEM((8 * nbat, 3 * C_WIDTH), F32),
                        pltpu.VMEM((8 * nbat, LORA_HALF), F32),
                        pltpu.VMEM((nbat * ch, C_WIDTH), F32)],
        compiler_params=_params("parallel", "arbitrary"),
        name="rwkv7_chunked",
    )(*ins)
    return yc.reshape(batch * seq, C_WIDTH), v_out.reshape(batch * seq, C_WIDTH)


def _merge_kernel(x_ref, g_ref, ya_ref, o0_ref, l0_ref, o1_ref, l1_ref, o2_ref, l2_ref, yc_ref,
                  wg_ref, bg_ref, wa_ref, wb_ref, wc_ref, wo_ref, g2_ref, wr_ref, br_ref,
                  xm_ref, h2_ref, comb_ref, nat_ref):
    x = x_ref[...]
    tm = x.shape[0]
    hb = _rmsnorm(x, g_ref[...]).astype(BF16)

    def token_order(ref, slot):
        dil = ref.shape[1]
        if dil == 1:
            return ref[0, 0]
        chunks = []
        for c in range(B_OUT // LANES):
            for r in range(dil):
                nat_ref[slot, c, pl.ds(r, tm // dil, stride=dil), :] = \
                    ref[0, r, :, c * LANES:(c + 1) * LANES]
            chunks.append(nat_ref[slot, c])
        return jnp.concatenate(chunks, axis=-1)

    l0, l1, l2 = token_order(l0_ref, 0), token_order(l1_ref, 0), token_order(l2_ref, 1)
    lm = jnp.maximum(jnp.maximum(l0, l1), l2)
    e0, e1, e2 = jnp.exp(l0 - lm), jnp.exp(l1 - lm), jnp.exp(l2 - lm)
    o0, o1, o2 = token_order(o0_ref, 0), token_order(o1_ref, 2), token_order(o2_ref, 3)
    yb = ((e0 * o0 + e1 * o1 + e2 * o2) / (e0 + e1 + e2)).astype(BF16)

    merged = None
    for j, (y, w_ref) in enumerate(((ya_ref[...], wa_ref), (yb, wb_ref), (yc_ref[...], wc_ref))):
        cs = slice(j * D_MODEL, (j + 1) * D_MODEL)
        gate = _sigmoid(jnp.dot(hb, wg_ref[0, :, cs], preferred_element_type=F32) + bg_ref[:, cs])
        term = gate * jnp.dot(y, w_ref[0], preferred_element_type=F32)
        merged = term if merged is None else merged + term
    xm = x + jnp.dot(merged.astype(BF16), wo_ref[0], preferred_element_type=F32)
    xm_ref[...] = xm

    h2 = _rmsnorm(xm, g2_ref[...])
    h2_ref[...] = h2.astype(BF16)

    logits = _dot3(wr_ref[...], h2, NT) + br_ref[:, 0:1]
    gl = [logits[g:g + 1, :] for g in range(N_GROUPS)]
    gmax = functools.reduce(jnp.maximum, gl)
    gsum = functools.reduce(lambda s, t: s + t, [jnp.exp(t - gmax) for t in gl])
    gp = 1.0 / gsum
    taken = jnp.zeros_like(gmax, dtype=jnp.bool_)
    sel = []
    for g in range(N_GROUPS):
        pick = (gl[g] == gmax) & jnp.logical_not(taken)
        sel.append(pick)
        taken = taken | pick
    el = []
    for e in range(EXPERTS_PER_GROUP):
        acc = jnp.zeros_like(gmax)
        for g in range(N_GROUPS):
            r0 = N_GROUPS + g * EXPERTS_PER_GROUP + e
            acc = jnp.where(sel[g], logits[r0:r0 + 1, :], acc)
        el.append(acc)
    emax = functools.reduce(jnp.maximum, el)
    ex = [jnp.exp(t - emax) for t in el]
    esum = functools.reduce(lambda s, t: s + t, ex)
    p = [t / esum for t in ex]
    p1 = functools.reduce(jnp.maximum, p)
    taken = jnp.zeros_like(gmax, dtype=jnp.bool_)
    first = []
    for e in range(EXPERTS_PER_GROUP):
        pick = (p[e] == p1) & jnp.logical_not(taken)
        first.append(pick)
        taken = taken | pick
    rest = [jnp.where(first[e], -1.0, p[e]) for e in range(EXPERTS_PER_GROUP)]
    p2 = functools.reduce(jnp.maximum, rest)
    taken = jnp.zeros_like(gmax, dtype=jnp.bool_)
    second = []
    for e in range(EXPERTS_PER_GROUP):
        pick = (rest[e] == p2) & jnp.logical_not(taken)
        second.append(pick)
        taken = taken | pick
    denom = p1 + p2
    for e in range(EXPERTS_PER_GROUP):
        in_group = jnp.where(first[e], p1 / denom, jnp.where(second[e], p2 / denom, 0.0))
        for g in range(N_GROUPS):
            r0 = g * EXPERTS_PER_GROUP + e
            comb_ref[r0:r0 + 1, :] = jnp.where(sel[g], gp * in_group, 0.0)
    gid = functools.reduce(lambda s, t: s + t,
                           [jnp.where(sel[g], float(g), 0.0) for g in range(N_GROUPS)])
    comb_ref[N_EXPERTS:N_EXPERTS + 1, :] = gid
    comb_ref[N_EXPERTS + 1:, :] = jnp.zeros((ROUTE_ROWS - N_EXPERTS - 1, gid.shape[1]), F32)


def _merge(x2, g, ya, attn, yc, layer, wg, bg, wa, wb, wc, wo, g2, wr, br, seq):
    n = x2.shape[0]
    tm = MERGE_ROW_TILE
    row = lambda width: pl.BlockSpec((tm, width), lambda i: (i, 0))
    of_layer = lambda rows, cols: pl.BlockSpec((1, rows, cols), lambda i: (layer, 0, 0))
    attn_flat = [t for pair in attn for t in pair]
    attn_specs = [_residue_spec(tm, t.shape[1], B_OUT, seq // tm) for t in attn_flat]
    return pl.pallas_call(
        _merge_kernel,
        grid=(n // tm,),
        in_specs=[row(D_MODEL), _full((1, D_MODEL)), row(A_WIDTH)] + attn_specs
        + [row(C_WIDTH), of_layer(D_MODEL, 3 * D_MODEL), _full((1, 3 * D_MODEL)),
           of_layer(A_WIDTH, D_MODEL), of_layer(B_OUT, D_MODEL), of_layer(C_WIDTH, D_MODEL),
           of_layer(D_MODEL, D_MODEL), _full((1, D_MODEL)), _full((32, D_MODEL)), _full((32, LANES))],
        out_specs=[row(D_MODEL), row(D_MODEL), pl.BlockSpec((ROUTE_ROWS, tm), lambda i: (0, i))],
        out_shape=[jax.ShapeDtypeStruct((n, D_MODEL), F32),
                   jax.ShapeDtypeStruct((n, D_MODEL), BF16),
                   jax.ShapeDtypeStruct((ROUTE_ROWS, n), F32)],
        scratch_shapes=[pltpu.VMEM((4, B_OUT // LANES, tm, LANES), F32)],
        compiler_params=_params("parallel"),
        name="merge_router",
    )(x2, g, ya, *attn_flat, yc, wg, bg, wa, wb, wc, wo, g2, wr, br)


def _group_ranks(gid):
    t = gid.shape[1]
    member = gid == lax.broadcasted_iota(jnp.int32, (8, t), 0).astype(F32)
    before = (lax.broadcasted_iota(jnp.int32, (t, t), 0)
              < lax.broadcasted_iota(jnp.int32, (t, t), 1)).astype(BF16)
    ranks = jnp.dot(member.astype(BF16), before, preferred_element_type=F32)
    return jnp.sum(jnp.where(member, ranks, 0.0), axis=0, keepdims=True)


def _group_onehots(gid, rank, cap):
    slot = lax.broadcasted_iota(jnp.int32, (cap, gid.shape[1]), 0).astype(F32)
    return [jnp.where(jnp.where(gid == float(g), rank, -1.0) == slot, 1.0, 0.0).astype(BF16)
            for g in range(N_GROUPS)]


def _expert_column(cw3, e):
    lane = lax.broadcasted_iota(jnp.int32, cw3.shape, 1)
    mine = (lane % N_EXPERTS == e) & (lane < 3 * N_EXPERTS)
    return jnp.sum(jnp.where(mine, cw3, 0.0), axis=-1, keepdims=True)


def _ffn(h, cw, wg, wu, wd):
    gate = jnp.dot(h, wg, preferred_element_type=F32)
    up = jnp.dot(h, wu, preferred_element_type=F32)
    hid = gate * _sigmoid(gate) * up * cw
    return jnp.dot(hid.astype(BF16), wd, preferred_element_type=F32)


def _moe_compact_kernel(h2_ref, route_ref, comb3_ref, xc_ref, cw_ref, rank_ref):
    gid = route_ref[N_EXPERTS:N_EXPERTS + 1, :]
    rank = _group_ranks(gid)
    rank_ref[0] = jnp.broadcast_to(rank, rank_ref.shape[1:])
    for g, p in enumerate(_group_onehots(gid, rank, xc_ref.shape[1])):
        xc_ref[g] = jnp.dot(p, h2_ref[...], preferred_element_type=F32).astype(BF16)
        cw_ref[g] = jnp.dot(p, comb3_ref[...], preferred_element_type=F32)


def _moe_expert_kernel(xc_ref, cw_ref, wg_ref, wu_ref, wd_ref, y_ref):
    j = pl.program_id(2)
    e = pl.program_id(0) * EXPERTS_PER_GROUP + j
    wg, wu, wd = (r[0, 0].astype(BF16) for r in (wg_ref, wu_ref, wd_ref))
    out = _ffn(xc_ref[0], _expert_column(cw_ref[0], e), wg, wu, wd)

    @pl.when(j == 0)
    def _():
        y_ref[0] = out

    @pl.when(j > 0)
    def _():
        y_ref[0] += out


def _moe_scatter_kernel(xm_ref, route_ref, rank_ref, y_ref, gf_ref, out_ref, *, final_norm):
    cap = y_ref.shape[1]
    gid = route_ref[N_EXPERTS:N_EXPERTS + 1, :]
    p = jnp.concatenate(_group_onehots(gid, rank_ref[0, 0:1, :], cap), axis=0)
    y = y_ref[...].reshape(N_GROUPS * cap, D_MODEL).astype(BF16)
    res = xm_ref[...] + lax.dot_general(p, y, TN, preferred_element_type=F32)
    out_ref[...] = _rmsnorm(res, gf_ref[...]) if final_norm else res


def _moe_dense_kernel(ids_ref, cnt_ref, xm_ref, h2_ref, comb3_ref, wg_ref, wu_ref, wd_ref, gf_ref,
                      prev_ref, out_ref, *, final_norm):
    del ids_ref, prev_ref
    e = pl.program_id(1)

    @pl.when(pl.program_id(0) < cnt_ref[0])
    def _():
        @pl.when(e == 0)
        def _():
            out_ref[...] = xm_ref[...]

        wg, wu, wd = (r[0, 0].astype(BF16) for r in (wg_ref, wu_ref, wd_ref))
        cw = _expert_column(comb3_ref[...].astype(F32), e)
        out_ref[...] += _ffn(h2_ref[...], cw, wg, wu, wd)

        if final_norm:
            @pl.when(e == N_EXPERTS - 1)
            def _():
                out_ref[...] = _rmsnorm(out_ref[...], gf_ref[...])


def _combine_pieces(route):
    n = route.shape[1]
    pieces, rest = [], route[:N_EXPERTS].T
    for _ in range(3):
        pieces.append(rest.astype(BF16))
        rest = rest - pieces[-1].astype(F32)
    return jnp.concatenate(pieces + [jnp.zeros((n, LANES - 3 * N_EXPERTS), BF16)], axis=1)


def _overflow_flags(route):
    gid = route[N_EXPERTS].astype(jnp.int32).reshape(-1, MOE_ROW_TILE)
    counts = jnp.sum(gid[:, :, None] == jnp.arange(N_GROUPS)[None, None, :], axis=1)
    return (jnp.max(counts, axis=1) > MOE_CAP).astype(jnp.int32)


def _moe_compacted(xm, h2, route, layer, wg, wu, wd, g_final, final_norm):
    n = xm.shape[0]
    tm, cap = MOE_ROW_TILE, MOE_CAP
    nt = n // tm
    comb3 = _combine_pieces(route)
    overflow = _overflow_flags(route)

    tile = lambda width: pl.BlockSpec((tm, width), lambda i: (i, 0))
    route_spec = pl.BlockSpec((ROUTE_ROWS, tm), lambda i: (0, i))
    slots = lambda width: pl.BlockSpec((N_GROUPS, cap, width), lambda i: (0, i, 0))
    rank_spec = pl.BlockSpec((1, 8, tm), lambda i: (i, 0, 0))
    xc, cw, rank = pl.pallas_call(
        _moe_compact_kernel,
        grid=(nt,),
        in_specs=[tile(D_MODEL), route_spec, tile(LANES)],
        out_specs=[slots(D_MODEL), slots(LANES), rank_spec],
        out_shape=[jax.ShapeDtypeStruct((N_GROUPS, nt * cap, D_MODEL), BF16),
                   jax.ShapeDtypeStruct((N_GROUPS, nt * cap, LANES), F32),
                   jax.ShapeDtypeStruct((nt, 8, tm), F32)],
        compiler_params=_params("parallel"),
        name="moe_compact",
    )(h2, route, comb3)

    halves = 4
    blk = nt * cap // halves
    rows_spec = lambda width: pl.BlockSpec((1, blk, width), lambda g, r, j: (g, r, 0))
    expert = lambda rows, cols: pl.BlockSpec(
        (1, 1, rows, cols), lambda g, r, j: (layer, g * EXPERTS_PER_GROUP + j, 0, 0))
    y = pl.pallas_call(
        _moe_expert_kernel,
        grid=(N_GROUPS, halves, EXPERTS_PER_GROUP),
        in_specs=[rows_spec(D_MODEL), rows_spec(LANES), expert(D_MODEL, EXPERT_FF),
                  expert(D_MODEL, EXPERT_FF), expert(EXPERT_FF, D_MODEL)],
        out_specs=rows_spec(D_MODEL),
        out_shape=jax.ShapeDtypeStruct((N_GROUPS, nt * cap, D_MODEL), F32),
        compiler_params=_params("parallel", "parallel", "arbitrary"),
        name="moe_experts",
    )(xc, cw, wg, wu, wd)

    out = pl.pallas_call(
        functools.partial(_moe_scatter_kernel, final_norm=final_norm),
        grid=(nt,),
        in_specs=[tile(D_MODEL), route_spec, rank_spec, slots(D_MODEL), _full((1, D_MODEL))],
        out_specs=tile(D_MODEL),
        out_shape=jax.ShapeDtypeStruct((n, D_MODEL), F32),
        compiler_params=_params("parallel"),
        name="moe_scatter",
    )(xm, route, rank, y, g_final)

    return out, comb3, overflow


def _moe_dense(ids, count, prev, xm, h2, comb3, layer, wg, wu, wd, g_final, final_norm):
    n = xm.shape[0]
    tm = MOE_ROW_TILE
    tile = lambda width: pl.BlockSpec((tm, width), lambda s, e, ids, cnt: (ids[s], 0))
    expert = lambda rows, cols: pl.BlockSpec(
        (1, 1, rows, cols), lambda s, e, ids, cnt: (layer, jnp.where(s < cnt[0], e, 0), 0, 0))
    grid_spec = pltpu.PrefetchScalarGridSpec(
        num_scalar_prefetch=2,
        grid=(ids.shape[0], N_EXPERTS),
        in_specs=[tile(D_MODEL), tile(D_MODEL), tile(LANES),
                  expert(D_MODEL, EXPERT_FF), expert(D_MODEL, EXPERT_FF), expert(EXPERT_FF, D_MODEL),
                  pl.BlockSpec((1, D_MODEL), lambda s, e, ids, cnt: (0, 0)),
                  pl.BlockSpec(memory_space=pl.ANY)],
        out_specs=tile(D_MODEL))
    return pl.pallas_call(
        functools.partial(_moe_dense_kernel, final_norm=final_norm),
        grid_spec=grid_spec,
        out_shape=jax.ShapeDtypeStruct((n, D_MODEL), F32),
        input_output_aliases={9: 0},
        compiler_params=_params("arbitrary", "arbitrary"),
        name="moe_dense",
    )(ids, count, xm, h2, comb3, wg, wu, wd, g_final, prev)


def _moe(xm, h2, route, layer, wg, wu, wd, g_final, final_norm):
    nt = xm.shape[0] // MOE_ROW_TILE
    dense_args = (layer, wg, wu, wd, g_final, final_norm)

    def compacted(_):
        return _moe_compacted(xm, h2, route, layer, wg, wu, wd, g_final, final_norm)

    def with_fix(_):
        out, comb3, overflow = compacted(None)
        order = jnp.argsort(-overflow, stable=True).astype(jnp.int32)[:MOE_FIX_TILES]
        count = jnp.sum(overflow).astype(jnp.int32)
        ids = jnp.where(jnp.arange(MOE_FIX_TILES) < count, order, order[jnp.maximum(count - 1, 0)])
        return _moe_dense(ids, count[None], out, xm, h2, comb3, *dense_args)

    def all_dense(_):
        comb3 = _combine_pieces(route)
        return _moe_dense(jnp.arange(nt, dtype=jnp.int32), jnp.full((1,), nt, jnp.int32), xm, xm, h2,
                          comb3, *dense_args)

    n_over = jnp.sum(_overflow_flags(route))
    case = jnp.where(n_over == 0, 0, jnp.where(n_over <= MOE_FIX_TILES, 1, 2))
    return lax.switch(case, [lambda _: compacted(None)[0], with_fix, all_dense], None)


def _pack_lora(mu_wag, w1, a1, g1, mu_v, v1):
    d = D_MODEL
    if v1 is None:
        mu_v = jnp.zeros((d,), F32)
        v1 = jnp.zeros((d, V_LORA), F32)
    pad = jnp.zeros((d, LORA_HALF - (W_LORA + A_LORA + G_LORA + V_LORA)), F32)
    mus = (mu_wag[0], mu_wag[1], mu_wag[2], mu_v)
    ws = (w1, a1, g1, v1)
    cur = [w * (1.0 - m)[:, None] for w, m in zip(ws, mus)]
    prev = [w * m[:, None] for w, m in zip(ws, mus)]
    return jnp.concatenate(cur + [pad] + prev + [pad], axis=1)


def _pad_rows(w, lo, total=LANES):
    return jnp.zeros((total, w.shape[1]), F32).at[lo:lo + w.shape[0]].set(w)


def kernel(x, positions, norm_mix_g, w_in, gmlp_ln_g, gmlp_ln_b, gmlp_ws, gmlp_bs, rwkv_mu_rkv, rwkv_mu_wag, rwkv_w0, rwkv_w1, rwkv_w2, rwkv_a0, rwkv_a1, rwkv_a2, rwkv_g1, rwkv_g2, rwkv_k_k, rwkv_k_a, rwkv_r_k, rwkv_ln_w, rwkv_ln_b, rwkv_mu_v, rwkv_v0, rwkv_v1, rwkv_v2, w_branch_a, w_branch_b, w_branch_c, w_gate, b_gate, w_out, norm_ffn_g, router_group_w, router_group_b, router_expert_w, router_expert_b, expert_w_gate, expert_w_up, expert_w_down, final_norm_g):
    batch, seq, d = x.shape
    depth = w_in.shape[0]
    n = batch * seq
    x2 = x.reshape(n, d)
    rope = _rope_tables(positions)
    w_in_b, w_gate_b, w_out_b = w_in.astype(BF16), w_gate.astype(BF16), w_out.astype(BF16)
    wba, wbb, wbc = w_branch_a.astype(BF16), w_branch_b.astype(BF16), w_branch_c.astype(BF16)
    v_first = None
    for l in range(depth):
        vres = l > 0
        lora_w = _pack_lora(rwkv_mu_wag[l], rwkv_w1[l], rwkv_a1[l], rwkv_g1[l],
                            rwkv_mu_v[l - 1] if vres else None, rwkv_v1[l - 1] if vres else None)
        ya, qkv0, qkv1, qkv2, rkv, lora = _inproj(
            x2, norm_mix_g[l][None], w_in_b, l, lora_w.astype(BF16), gmlp_ln_g[l][None],
            gmlp_ln_b[l][None], gmlp_ws[l], gmlp_bs[l].T, rope, batch, seq)

        attn = [_dilated_attention(qkv, dil)
                for qkv, (_, dil) in zip((qkv0, qkv1, qkv2), DILATED_PATTERNS)]

        zero = jnp.zeros((C_WIDTH,), F32)
        vecs = jnp.stack([rwkv_w0[l], rwkv_a0[l], rwkv_v0[l - 1] if vres else zero, rwkv_k_k[l],
                          rwkv_k_a[l], rwkv_r_k[l].reshape(C_WIDTH), rwkv_ln_w[l], rwkv_ln_b[l]])
        yc, v_c = _rwkv(rkv, lora, v_first, rwkv_mu_rkv[l], vecs,
                        _pad_rows(rwkv_w2[l], 0), _pad_rows(rwkv_a2[l], W_LORA), rwkv_g2[l],
                        _pad_rows(rwkv_v2[l - 1], 0) if vres else None, batch, seq)
        if l == 0:
            v_first = v_c

        wr = jnp.concatenate([router_group_w[l], router_expert_w[l]], axis=1).T
        wr = jnp.zeros((32, d), F32).at[:wr.shape[0]].set(wr)
        br = jnp.concatenate([router_group_b[l], router_expert_b[l]])
        br = jnp.zeros((32, LANES), F32).at[:br.shape[0], 0].set(br)
        xm, h2, route = _merge(
            x2, norm_mix_g[l][None], ya, attn, yc, l, w_gate_b, b_gate[l][None], wba, wbb, wbc,
            w_out_b, norm_ffn_g[l][None], wr, br, seq)

        x2 = _moe(xm, h2, route, l, expert_w_gate, expert_w_up, expert_w_down, final_norm_g[None],
                  final_norm=(l == depth - 1))
    return x2.reshape(batch, seq, d)
```

```python
import functools

import numpy as np
import jax
import jax.numpy as jnp
from jax import lax
from jax.experimental import pallas as pl
from jax.experimental.pallas import tpu as pltpu

F32 = jnp.float32
BF16 = jnp.bfloat16

D_MODEL = 1024
HEAD_DIM = 64
A_GROUPS = 4
A_WIDTH = 512
CHUNK = 128
DILATED_PATTERNS = ((128, 1), (512, 4), (2048, 16))
B_HEADS_PER_GROUP = 4
B_WIDTH = 768
B_OUT = 256
BLOCK = 128
ROPE_THETA = 500000.0
ROPE_DIMS = 16
C_HEADS = 8
C_WIDTH = 512
W_LORA, A_LORA, V_LORA, G_LORA = 64, 64, 32, 128
RWKV_GN_EPS = HEAD_DIM * 1e-5
IN_COLS = 2 * A_WIDTH + 3 * B_WIDTH + 3 * C_WIDTH
LORA_HALF = 384
LORA_COLS = 2 * LORA_HALF
N_BRANCHES = 3
N_GROUPS = 4
EXPERTS_PER_GROUP = 4
N_EXPERTS = 16
EXPERT_FF = 512
RMS_EPS = 1e-6
LN_EPS = 1e-5

LANES = 128
ROW_TILE = 512
MERGE_ROW_TILE = 512
RWKV_CHUNK = 64
RWKV_BATCH_ROWS = 8
RWKV_SUB_ROWS = 2
MOE_ROW_TILE = 1024
MOE_CAP = 352
MOE_FIX_TILES = 2
ROUTE_ROWS = 24
VMEM_LIMIT = 56 * 1024 * 1024
NEG_BIG = -1e30


def _params(*sem):
    return pltpu.CompilerParams(dimension_semantics=sem, vmem_limit_bytes=VMEM_LIMIT)


def _full(shape):
    nd = len(shape)
    return pl.BlockSpec(shape, lambda *_: (0,) * nd)


def _gelu_tanh(x):
    return 0.5 * x * (1.0 + jnp.tanh(0.7978845608028654 * (x + 0.044715 * (x * x * x))))


def _sigmoid(x):
    return 1.0 / (1.0 + jnp.exp(-x))


def _rmsnorm(x, g):
    return x * lax.rsqrt(jnp.mean(x * x, axis=-1, keepdims=True) + RMS_EPS) * g


def _rope_table_kernel(pos_ref, consts_ref, c_ref, sa_ref, sb_ref):
    ang = pos_ref[...].astype(F32) * consts_ref[0:1, :]
    c_ref[...] = jnp.cos(ang)
    s = jnp.sin(ang)
    sa_ref[...] = -s * consts_ref[1:2, :]
    sb_ref[...] = s * consts_ref[2:3, :]


def _rope_tables(positions):
    n = positions.size
    half = ROPE_DIMS // 2
    inv_freq = ROPE_THETA ** (-jnp.arange(half, dtype=F32) / half)
    lane = np.arange(LANES) % HEAD_DIM
    in_rope = lane < ROPE_DIMS
    invf_row = jnp.where(in_rope, inv_freq[lane % half], 0.0)
    consts = jnp.zeros((8, LANES), F32)
    consts = consts.at[0].set(invf_row)
    consts = consts.at[1].set(jnp.asarray(lane < half, F32))
    consts = consts.at[2].set(jnp.asarray((lane >= half) & in_rope, F32))
    tm = 1024
    out = jax.ShapeDtypeStruct((n, LANES), F32)
    return pl.pallas_call(
        _rope_table_kernel,
        grid=(n // tm,),
        in_specs=[pl.BlockSpec((tm, 1), lambda i: (i, 0)), _full((8, LANES))],
        out_specs=[pl.BlockSpec((tm, LANES), lambda i: (i, 0))] * 3,
        out_shape=[out, out, out],
        compiler_params=_params("parallel"),
        name="rope_tables",
    )(positions.reshape(n, 1), consts)


def _inproj_kernel(x_ref, g_ref, w_ref, wl_ref, lng_ref, lnb_ref, ws_ref, bst_ref, c_ref, sa_ref,
                   sb_ref, ya_ref, qkv0_ref, qkv1_ref, qkv2_ref, rkv_ref, lora_ref, nat_ref):
    tm = x_ref.shape[0]
    hb = _rmsnorm(x_ref[...], g_ref[...]).astype(BF16)

    def proj(lo, width):
        return jnp.dot(hb, w_ref[0, :, lo:lo + width], preferred_element_type=F32)

    q_off = 2 * A_WIDTH
    c_off = q_off + 3 * B_WIDTH
    uv = proj(0, 2 * A_WIDTH)
    rkv_ref[:, 0:C_WIDTH] = proj(c_off, C_WIDTH)
    u = _gelu_tanh(uv[:, :A_WIDTH])
    rkv_ref[:, C_WIDTH:2 * C_WIDTH] = proj(c_off + C_WIDTH, C_WIDTH)
    v = _gelu_tanh(uv[:, A_WIDTH:])
    rkv_ref[:, 2 * C_WIDTH:] = proj(c_off + 2 * C_WIDTH, C_WIDTH)
    mu = jnp.mean(v, axis=-1, keepdims=True)
    d = v - mu
    var = jnp.mean(d * d, axis=-1, keepdims=True)
    lora_ref[...] = jnp.dot(hb, wl_ref[...], preferred_element_type=F32)
    vn = (d * lax.rsqrt(var + LN_EPS) * lng_ref[...] + lnb_ref[...]).astype(BF16)
    row = lax.broadcasted_iota(jnp.int32, (CHUNK, CHUNK), 0)
    col = lax.broadcasted_iota(jnp.int32, (CHUNK, CHUNK), 1)
    for g in range(A_GROUPS):
        wg = jnp.where(row >= col, ws_ref[g], 0.0).astype(BF16)
        bias = bst_ref[:, g:g + 1]
        cs = slice(g * LANES, (g + 1) * LANES)
        for c in range(tm // CHUNK):
            rs = slice(c * CHUNK, (c + 1) * CHUNK)
            mixed = jnp.dot(wg, vn[rs, cs], preferred_element_type=F32) + bias
            ya_ref[rs, cs] = (u[rs, cs] * mixed).astype(BF16)

    cos, sa, sb = c_ref[...], sa_ref[...], sb_ref[...]
    for grp, (qkv_ref, (_, dil)) in enumerate(zip((qkv0_ref, qkv1_ref, qkv2_ref), DILATED_PATTERNS)):
        acc = [proj(q_off + part * B_WIDTH + grp * B_OUT, B_OUT) for part in range(3)]
        for c in range(B_WIDTH // LANES):
            t = acc[c * LANES // B_OUT][:, c * LANES % B_OUT:c * LANES % B_OUT + LANES]
            if c < 2 * B_OUT // LANES:
                t = t * cos + pltpu.roll(t, LANES - 8, 1) * sa + pltpu.roll(t, 8, 1) * sb
            if c < B_OUT // LANES:
                t = t * (HEAD_DIM ** -0.5)
            cs = slice(c * LANES, (c + 1) * LANES)
            if dil == 1:
                qkv_ref[0, 0, :, cs] = t.astype(BF16)
            else:
                nat_ref[c] = t
                for r in range(dil):
                    qkv_ref[0, r, :, cs] = nat_ref[c, pl.ds(r, tm // dil, stride=dil), :].astype(BF16)


def _residue_spec(tm, dil, width, tiles_per_seq):
    return pl.BlockSpec((1, dil, tm // dil, width),
                        lambda i: (i // tiles_per_seq, 0, i % tiles_per_seq, 0))


def _inproj(x2, g, w_in, layer, w_lora, ln_g, ln_b, ws, bs_t, rope, batch, seq):
    n = x2.shape[0]
    tm = ROW_TILE
    row = lambda width: pl.BlockSpec((tm, width), lambda i: (i, 0))
    dils = [d for _, d in DILATED_PATTERNS]
    return pl.pallas_call(
        _inproj_kernel,
        grid=(n // tm,),
        in_specs=[row(D_MODEL), _full((1, D_MODEL)),
                  pl.BlockSpec((1, D_MODEL, IN_COLS), lambda i: (layer, 0, 0)),
                  _full((D_MODEL, LORA_COLS)),
                  _full((1, A_WIDTH)), _full((1, A_WIDTH)), _full((A_GROUPS, CHUNK, CHUNK)),
                  _full((CHUNK, A_GROUPS)), row(LANES), row(LANES), row(LANES)],
        out_specs=[row(A_WIDTH)] + [_residue_spec(tm, d, B_WIDTH, seq // tm) for d in dils]
        + [row(3 * C_WIDTH), row(LORA_COLS)],
        out_shape=[jax.ShapeDtypeStruct((n, A_WIDTH), BF16)]
        + [jax.ShapeDtypeStruct((batch, d, seq // d, B_WIDTH), BF16) for d in dils]
        + [jax.ShapeDtypeStruct((n, 3 * C_WIDTH), F32),
           jax.ShapeDtypeStruct((n, LORA_COLS), F32)],
        scratch_shapes=[pltpu.VMEM((B_WIDTH // LANES, tm, LANES), F32)],
        compiler_params=_params("parallel"),
        name="inproj",
    )(x2, g, w_in, w_lora, ln_g, ln_b, ws, bs_t, *rope)


def _attn_blocks(blocks):
    lo = lax.broadcasted_iota(jnp.int32, (BLOCK, LANES), 1) < HEAD_DIM
    pairs = range(B_OUT // LANES)
    heads = range(B_HEADS_PER_GROUP)
    scores = []
    for q, kw, _, valid in blocks:
        for p in pairs:
            ps = slice(p * LANES, (p + 1) * LANES)
            qp = q[:, ps]
            zero = jnp.zeros_like(qp)
            for part in (jnp.where(lo, qp, zero), jnp.where(lo, zero, qp)):
                s = lax.dot_general(part, kw[:, ps], NT, preferred_element_type=F32)
                scores.append(jnp.where(valid, s, NEG_BIG))
    m = [jnp.max(s, axis=-1, keepdims=True) for s in scores]
    e = [jnp.exp(s - mx) for s, mx in zip(scores, m)]
    l = [jnp.sum(t, axis=-1, keepdims=True) for t in e]
    pv = [jnp.dot(e[len(heads) * b + h].astype(BF16),
                  blocks[b][2][:, (h // 2) * LANES:(h // 2 + 1) * LANES],
                  preferred_element_type=F32)
          for b in range(len(blocks)) for h in heads]
    lse = [mx + jnp.log(t) for mx, t in zip(m, l)]
    results = []
    for b in range(len(blocks)):
        at = lambda p, h: len(heads) * b + 2 * p + h
        out = [jnp.where(lo, pv[at(p, 0)] / l[at(p, 0)], pv[at(p, 1)] / l[at(p, 1)]) for p in pairs]
        ls = [jnp.where(lo, lse[at(p, 0)], lse[at(p, 1)]) for p in pairs]
        results.append((jnp.concatenate(out, axis=-1), jnp.concatenate(ls, axis=-1)))
    return results


def _attn_kernel(qkv_ref, o_ref, l_ref):
    dil, sub = qkv_ref.shape[1], qkv_ref.shape[2]
    nb = sub // BLOCK
    qs, ks, vs = (slice(i * B_OUT, (i + 1) * B_OUT) for i in range(3))
    qi = lax.broadcasted_iota(jnp.int32, (BLOCK, BLOCK), 0)
    kj = lax.broadcasted_iota(jnp.int32, (BLOCK, BLOCK), 1)
    qi2 = lax.broadcasted_iota(jnp.int32, (BLOCK, 2 * BLOCK), 0)
    kj2 = lax.broadcasted_iota(jnp.int32, (BLOCK, 2 * BLOCK), 1)
    valid2 = (kj2 >= qi2) & (kj2 <= qi2 + BLOCK)

    def first_block(r):
        rows = pl.ds(0, BLOCK)
        return rows, (qkv_ref[0, r, rows, qs], qkv_ref[0, r, rows, ks], qkv_ref[0, r, rows, vs],
                      kj <= qi)

    def later_block(r, n):
        rows = pl.ds(pl.multiple_of(n * BLOCK, BLOCK), BLOCK)
        win = pl.ds(pl.multiple_of((n - 1) * BLOCK, BLOCK), 2 * BLOCK)
        return rows, (qkv_ref[0, r, rows, qs], qkv_ref[0, r, win, ks], qkv_ref[0, r, win, vs], valid2)

    def run(work):
        for (o, l), (r, rows, _) in zip(_attn_blocks([w[2] for w in work]), work):
            o_ref[0, r, rows, :] = o
            l_ref[0, r, rows, :] = l

    if nb == 1:
        def body(i, carry):
            run([(r,) + first_block(r) for r in (2 * i, 2 * i + 1)])
            return carry
        lax.fori_loop(0, dil // 2, body, 0)
    else:
        for r in range(dil):
            run([(r,) + first_block(r), (r,) + later_block(r, 1)])

            def body(i, carry):
                run([(r,) + later_block(r, n) for n in (2 * i, 2 * i + 1)])
                return carry
            lax.fori_loop(1, nb // 2, body, 0)


def _dilated_attention(qkv, dilation):
    batch, _, sub, _ = qkv.shape
    ospec = pl.BlockSpec((1, dilation, sub, B_OUT), lambda b: (b, 0, 0, 0))
    oshape = jax.ShapeDtypeStruct((batch, dilation, sub, B_OUT), F32)
    return pl.pallas_call(
        _attn_kernel,
        grid=(batch,),
        in_specs=[pl.BlockSpec((1, dilation, sub, B_WIDTH), lambda b: (b, 0, 0, 0))],
        out_specs=[ospec, ospec],
        out_shape=[oshape, oshape],
        compiler_params=_params("parallel"),
        name=f"dilated_attn_d{dilation}",
    )(qkv)


def _shift_rows(x, prev_row):
    rolled = pltpu.roll(x, 1, 0)
    first = lax.broadcasted_iota(jnp.int32, x.shape, 0) == 0
    return jnp.where(first, prev_row, rolled)


NN = (((1,), (0,)), ((), ()))
NT = (((1,), (1,)), ((), ()))
TN = (((0,), (0,)), ((), ()))


def _dot(a, b, dims=NN):
    return lax.dot_general(a.astype(BF16), b.astype(BF16), dims, preferred_element_type=F32)


def _split(a, pieces):
    out = []
    for _ in range(pieces):
        p = a.astype(BF16)
        out.append(p)
        a = a - p.astype(F32)
    return out


def _dot3(a, b, dims=NN):
    (ah, al), (bh, bl) = _split(a, 2), _split(b, 2)
    return _dot(ah, bh, dims) + _dot(ah, bl, dims) + _dot(al, bh, dims)


def _head_sum(x):
    lo = lax.broadcasted_iota(jnp.int32, (x.shape[0], LANES), 1) < HEAD_DIM
    outs = []
    for c in range(x.shape[1] // LANES):
        xc = x[:, c * LANES:(c + 1) * LANES]
        s_lo = jnp.sum(jnp.where(lo, xc, 0.0), axis=-1, keepdims=True)
        s_hi = jnp.sum(jnp.where(lo, 0.0, xc), axis=-1, keepdims=True)
        outs.append(jnp.where(lo, s_lo, s_hi))
    return jnp.concatenate(outs, axis=-1)


def _finish(gen):
    try:
        while True:
            next(gen)
    except StopIteration as stop:
        return stop.value


def _interleave(first, second):
    values, gens = [None, None], [first, second]
    live = [True, True]
    while any(live):
        for i, gen in enumerate(gens):
            if live[i]:
                try:
                    next(gen)
                except StopIteration as stop:
                    values[i], live[i] = stop.value, False
    return values


def _rwkv_kernel(*refs, has_vres):
    if has_vres:
        (rkv_ref, lora_ref, vf_ref, mu_ref, vec_ref, w2_ref, a2_ref, g2_ref, v2_ref,
         yc_ref, vout_ref, state_ref, prkv_ref, plora_ref, y_ref) = refs
    else:
        (rkv_ref, lora_ref, mu_ref, vec_ref, w2_ref, a2_ref, g2_ref,
         yc_ref, vout_ref, state_ref, prkv_ref, plora_ref, y_ref) = refs
    nbat, ch = rkv_ref.shape[0], rkv_ref.shape[1]
    sub = RWKV_SUB_ROWS
    rows = sub * ch
    hd = HEAD_DIM
    npair = C_HEADS // 2
    row_of = [slice(j * ch, (j + 1) * ch) for j in range(sub)]

    @pl.when(pl.program_id(1) == 0)
    def _():
        state_ref[...] = jnp.zeros_like(state_ref)
        prkv_ref[...] = jnp.zeros_like(prkv_ref)
        plora_ref[...] = jnp.zeros_like(plora_ref)

    w0, a0, v0 = vec_ref[0:1, :], vec_ref[1:2, :], vec_ref[2:3, :]
    k_k, k_a, r_k = vec_ref[3:4, :], vec_ref[4:5, :], vec_ref[5:6, :]
    ln_w, ln_b = vec_ref[6:7, :], vec_ref[7:8, :]

    ti = lax.broadcasted_iota(jnp.int32, (rows, rows), 0)
    si = lax.broadcasted_iota(jnp.int32, (rows, rows), 1)
    tri = ((si <= ti) & (si // ch == ti // ch)).astype(BF16)
    unit = (lax.broadcasted_iota(jnp.int32, (8, LANES), 0) == 0).astype(BF16)
    lane = lax.broadcasted_iota(jnp.int32, (ch, LANES), 1)
    lo = lane < hd
    lo2 = lax.broadcasted_iota(jnp.int32, (2 * ch, LANES), 1) < hd
    row4 = lax.broadcasted_iota(jnp.int32, (4 * ch, LANES), 0)
    col4 = lax.broadcasted_iota(jnp.int32, (4 * ch, LANES), 1) % ch
    t4 = row4 % ch
    tri_mask = (col4 < t4) | ((col4 == t4) & ((row4 // ch) % 2 == 1))
    eye_right = (lane - hd == lax.broadcasted_iota(jnp.int32, (ch, LANES), 0)).astype(F32)
    brow = lax.broadcasted_iota(jnp.int32, (LANES, LANES), 0) // hd
    bcol = lax.broadcasted_iota(jnp.int32, (LANES, LANES), 1) // hd
    block_diag = brow == bcol
    zeros = jnp.zeros((ch, LANES), F32)

    def prepare(grp):
        bs = slice(grp * sub, (grp + 1) * sub)

        def shifted(x, prev_ref):
            parts = []
            for j, rs in enumerate(row_of):
                slot = 8 * (grp * sub + j)
                parts.append(_shift_rows(x[rs], prev_ref[slot:slot + 1, :]))
                prev_ref[slot:slot + 1, :] = x[(j + 1) * ch - 1:(j + 1) * ch, :]
            return jnp.concatenate(parts, axis=0)

        rkv = rkv_ref[bs].reshape(rows, 3 * C_WIDTH)
        rkv_prev = shifted(rkv, prkv_ref)
        lora_all = lora_ref[bs].reshape(rows, LORA_COLS)
        lora = lora_all[:, :LORA_HALF] + shifted(lora_all[:, LORA_HALF:], plora_ref)
        yield

        def mix(i):
            cur = rkv[:, i * C_WIDTH:(i + 1) * C_WIDTH]
            prev = rkv_prev[:, i * C_WIDTH:(i + 1) * C_WIDTH]
            return cur + (prev - cur) * mu_ref[i:i + 1, :]

        r, k, v = mix(0), mix(1), mix(2)
        yield
        wa = lora[:, 0:LANES]
        z = w0 + _dot3(jnp.tanh(wa), w2_ref[...])
        log_decay = -float(np.exp(-0.5)) * _sigmoid(z)
        yield
        a = _sigmoid(a0 + _dot(wa, a2_ref[...]))
        gate = _dot(_sigmoid(lora[:, LANES:2 * LANES]), g2_ref[...])
        if has_vres:
            vmix = _sigmoid(v0 + _dot(lora[:, 2 * LANES:3 * LANES], v2_ref[...]))
            v = v + (vf_ref[bs].reshape(rows, C_WIDTH) - v) * vmix
        vout_ref[bs] = v.reshape(sub, ch, C_WIDTH)
        yield

        kk = k * k_k
        kk = kk * lax.rsqrt(jnp.maximum(_head_sum(kk * kk), 1e-24))
        yield
        k = k * (1.0 + (a - 1.0) * k_a)
        bonus = _head_sum(r * k * r_k) * v
        b = kk * a
        yield

        cum = functools.reduce(lambda s, t: s + t, [_dot(tri, p) for p in _split(log_decay, 3)])
        ends = [cum[(j + 1) * ch - 1:(j + 1) * ch, :] for j in range(sub)]
        cum_end = jnp.concatenate([jnp.broadcast_to(e, (ch, C_WIDTH)) for e in ends], axis=0)
        yield
        e_neg = jnp.exp(-cum)
        e_end = jnp.exp(cum_end - cum)
        kt = kk * jnp.exp(cum - log_decay)
        yield
        rt = r * jnp.exp(cum)
        bt = b * e_neg
        kq = k * e_neg
        yield
        kh = k * e_end
        bh = b * e_end
        w_end_t = [functools.reduce(lambda s, t: s + t,
                                    [_dot(jnp.broadcast_to(p, (8, C_WIDTH)), unit, TN)
                                     for p in _split(jnp.exp(e), 2)])
                   for e in ends]
        yield
        return dict(kt=kt, rt=rt, bt=bt, kq=kq, kh=kh, bh=bh, v=v, bonus=bonus, gate=gate,
                    w_end_t=w_end_t)

    def solve(grp, t):
        chains = [(j, p) for j in range(sub) for p in range(npair)]
        at = lambda x, c: x[row_of[c[0]], c[1] * LANES:(c[1] + 1) * LANES]
        idx = range(len(chains))
        slot_of = lambda c: (grp * sub + c[0]) * npair + c[1]
        s_bd = [state_ref[slot_of(c)] for c in chains]
        l2 = [jnp.concatenate([at(t['kt'], c), at(t['rt'], c)], axis=0) for c in chains]
        ss = [_dot(l2[i], s_bd[i]) for i in idx]
        yield
        g = [jnp.where(tri_mask,
                       _dot(jnp.concatenate([jnp.where(lo2, l2[i], 0.0),
                                             jnp.where(lo2, 0.0, l2[i])], axis=0),
                            jnp.concatenate([at(t['bt'], c), at(t['kq'], c)], axis=0), NT), 0.0)
             for i, c in zip(idx, chains)]
        yield
        xv = [_dot(jnp.concatenate([g[i][0:ch], g[i][2 * ch:3 * ch]], axis=0),
                   jnp.concatenate([zeros, at(t['v'], c)], axis=0))
              for i, c in zip(idx, chains)]
        yield
        heads = [(i, h) for i in idx for h in range(2)]
        n0 = [-g[i][2 * h * ch:(2 * h + 1) * ch] for i, h in heads]
        w = [_dot(n[:, :ch], jnp.where(lo, n, eye_right)) + jnp.where(lo, 0.0, eye_right)
             for n in n0]
        yield
        for _ in range(int(np.log2(ch)) - 1):
            w = [_dot(x[:, :ch], x) + jnp.where(lo, 0.0, x) for x in w]
            yield
        us = [_dot(w[2 * i + h],
                   jnp.concatenate([zeros, ss[i][0:ch] + xv[i][h * ch:(h + 1) * ch]], axis=0))
              for i, h in heads]
        u = [jnp.where(lo, us[2 * i], us[2 * i + 1]) for i in idx]
        yield
        yv = [_dot(jnp.concatenate([g[i][ch:2 * ch], g[i][3 * ch:4 * ch]], axis=0),
                   jnp.concatenate([-u[i], at(t['v'], c)], axis=0))
              for i, c in zip(idx, chains)]
        yield
        upd = [_dot(jnp.concatenate([at(t['kh'], c), at(t['bh'], c)], axis=0),
                    jnp.concatenate([at(t['v'], c), -u[i]], axis=0), TN)
               for i, c in zip(idx, chains)]
        yield
        ys = slice(grp * rows, (grp + 1) * rows)
        for i, c in zip(idx, chains):
            ps = slice(c[1] * LANES, (c[1] + 1) * LANES)
            y_ref[grp * rows + c[0] * ch:grp * rows + (c[0] + 1) * ch, ps] = (
                ss[i][ch:2 * ch] + jnp.where(lo, yv[i][0:ch], yv[i][ch:2 * ch]))
            state_ref[slot_of(c)] = (t['w_end_t'][c[0]][ps, :] * s_bd[i]
                                     + jnp.where(block_diag, upd[i], 0.0))
        yield
        y = y_ref[ys, :]
        d = y - _head_sum(y) * (1.0 / hd)
        var = _head_sum(d * d) * (1.0 / hd)
        yn = d * lax.rsqrt(var + RWKV_GN_EPS) * ln_w + ln_b
        yc_ref[grp * sub:(grp + 1) * sub] = (
            ((yn + t['bonus']) * t['gate']).astype(BF16).reshape(sub, ch, C_WIDTH))

    groups = nbat // sub
    prepared = _finish(prepare(0))
    for grp in range(groups):
        if grp + 1 < groups:
            _, prepared_next = _interleave(solve(grp, prepared), prepare(grp + 1))
            prepared = prepared_next
        else:
            _finish(solve(grp, prepared))


def _rwkv(rkv, lora, v_first, mu_rkv, vecs, w2, a2, g2, v2, batch, seq):
    ch = RWKV_CHUNK
    nbat = RWKV_BATCH_ROWS
    has_vres = v_first is not None
    per_seq = lambda t: t.reshape(batch, seq, t.shape[-1])
    blk = lambda width: pl.BlockSpec((nbat, ch, width), lambda i, c: (i, c, 0))
    ins = [per_seq(rkv), per_seq(lora)] + ([per_seq(v_first)] if has_vres else []) \
        + [mu_rkv, vecs, w2, a2, g2] + ([v2] if has_vres else [])
    specs = [blk(3 * C_WIDTH), blk(LORA_COLS)] + ([blk(C_WIDTH)] if has_vres else []) \
        + [_full((3, C_WIDTH)), _full((8, C_WIDTH)), _full((LANES, C_WIDTH)),
           _full((LANES, C_WIDTH)), _full((LANES, C_WIDTH))] \
        + ([_full((LANES, C_WIDTH))] if has_vres else [])
    yc, v_out = pl.pallas_call(
        functools.partial(_rwkv_kernel, has_vres=has_vres),
        grid=(batch // nbat, seq // ch),
        in_specs=specs,
        out_specs=[blk(C_WIDTH), blk(C_WIDTH)],
        out_shape=[jax.ShapeDtypeStruct((batch, seq, C_WIDTH), BF16),
                   jax.ShapeDtypeStruct((batch, seq, C_WIDTH), F32)],
        scratch_shapes=[pltpu.VMEM((nbat * C_HEADS // 2, LANES, LANES), F32),
                        pltpu.VMEM((8 * nbat, 3 * C_WIDTH), F32),
                        pltpu.VMEM((8 * nbat, LORA_HALF), F32),
                        pltpu.VMEM((nbat * ch, C_WIDTH), F32)],
        compiler_params=_params("parallel", "arbitrary"),
        name="rwkv7_chunked",
    )(*ins)
    return yc.reshape(batch * seq, C_WIDTH), v_out.reshape(batch * seq, C_WIDTH)


def _merge_kernel(x_ref, g_ref, ya_ref, o0_ref, l0_ref, o1_ref, l1_ref, o2_ref, l2_ref, yc_ref,
                  wg_ref, bg_ref, wa_ref, wb_ref, wc_ref, wo_ref, g2_ref, wr_ref, br_ref,
                  xm_ref, h2_ref, comb_ref, nat_ref):
    x = x_ref[...]
    tm = x.shape[0]
    hb = _rmsnorm(x, g_ref[...]).astype(BF16)

    def token_order(ref, slot):
        dil = ref.shape[1]
        if dil == 1:
            return ref[0, 0]
        chunks = []
        for c in range(B_OUT // LANES):
            for r in range(dil):
                nat_ref[slot, c, pl.ds(r, tm // dil, stride=dil), :] = \
                    ref[0, r, :, c * LANES:(c + 1) * LANES]
            chunks.append(nat_ref[slot, c])
        return jnp.concatenate(chunks, axis=-1)

    def attention_branch():
        l0, l1, l2 = token_order(l0_ref, 0), token_order(l1_ref, 0), token_order(l2_ref, 1)
        lm = jnp.maximum(jnp.maximum(l0, l1), l2)
        e0, e1, e2 = jnp.exp(l0 - lm), jnp.exp(l1 - lm), jnp.exp(l2 - lm)
        o0, o1, o2 = token_order(o0_ref, 0), token_order(o1_ref, 2), token_order(o2_ref, 3)
        return ((e0 * o0 + e1 * o1 + e2 * o2) / (e0 + e1 + e2)).astype(BF16)

    inputs = {0: ya_ref[...], 2: yc_ref[...]}
    weights = (wa_ref, wb_ref, wc_ref)
    width = 2 * LANES
    units = [(j, c) for c in range(D_MODEL // width) for j in range(N_BRANCHES)]

    def gate_pre(unit):
        j, c = unit
        lo_col = j * D_MODEL + c * width
        return jnp.dot(hb, wg_ref[0, :, lo_col:lo_col + width], preferred_element_type=F32)

    pre = gate_pre(units[0])
    chunks = [None] * (D_MODEL // width)
    for i, (j, c) in enumerate(units):
        nxt = gate_pre(units[i + 1]) if i + 1 < len(units) else None
        lo_col = j * D_MODEL + c * width
        gate = _sigmoid(pre + bg_ref[:, lo_col:lo_col + width])
        if j not in inputs:
            inputs[j] = attention_branch()
        term = gate * jnp.dot(inputs[j], weights[j][0, :, c * width:(c + 1) * width],
                              preferred_element_type=F32)
        chunks[c] = term if chunks[c] is None else chunks[c] + term
        pre = nxt
    merged = jnp.concatenate([t.astype(BF16) for t in chunks], axis=-1)
    xm = x + jnp.dot(merged, wo_ref[0], preferred_element_type=F32)
    xm_ref[...] = xm

    h2 = _rmsnorm(xm, g2_ref[...])
    h2_ref[...] = h2.astype(BF16)

    logits = _dot3(wr_ref[...], h2, NT) + br_ref[:, 0:1]
    gl = [logits[g:g + 1, :] for g in range(N_GROUPS)]
    gmax = functools.reduce(jnp.maximum, gl)
    gsum = functools.reduce(lambda s, t: s + t, [jnp.exp(t - gmax) for t in gl])
    gp = 1.0 / gsum
    taken = jnp.zeros_like(gmax, dtype=jnp.bool_)
    sel = []
    for g in range(N_GROUPS):
        pick = (gl[g] == gmax) & jnp.logical_not(taken)
        sel.append(pick)
        taken = taken | pick
    el = []
    for e in range(EXPERTS_PER_GROUP):
        acc = jnp.zeros_like(gmax)
        for g in range(N_GROUPS):
            r0 = N_GROUPS + g * EXPERTS_PER_GROUP + e
            acc = jnp.where(sel[g], logits[r0:r0 + 1, :], acc)
        el.append(acc)
    emax = functools.reduce(jnp.maximum, el)
    ex = [jnp.exp(t - emax) for t in el]
    esum = functools.reduce(lambda s, t: s + t, ex)
    p = [t / esum for t in ex]
    p1 = functools.reduce(jnp.maximum, p)
    taken = jnp.zeros_like(gmax, dtype=jnp.bool_)
    first = []
    for e in range(EXPERTS_PER_GROUP):
        pick = (p[e] == p1) & jnp.logical_not(taken)
        first.append(pick)
        taken = taken | pick
    rest = [jnp.where(first[e], -1.0, p[e]) for e in range(EXPERTS_PER_GROUP)]
    p2 = functools.reduce(jnp.maximum, rest)
    taken = jnp.zeros_like(gmax, dtype=jnp.bool_)
    second = []
    for e in range(EXPERTS_PER_GROUP):
        pick = (rest[e] == p2) & jnp.logical_not(taken)
        second.append(pick)
        taken = taken | pick
    denom = p1 + p2
    for e in range(EXPERTS_PER_GROUP):
        in_group = jnp.where(first[e], p1 / denom, jnp.where(second[e], p2 / denom, 0.0))
        for g in range(N_GROUPS):
            r0 = g * EXPERTS_PER_GROUP + e
            comb_ref[r0:r0 + 1, :] = jnp.where(sel[g], gp * in_group, 0.0)
    gid = functools.reduce(lambda s, t: s + t,
                           [jnp.where(sel[g], float(g), 0.0) for g in range(N_GROUPS)])
    comb_ref[N_EXPERTS:N_EXPERTS + 1, :] = gid
    comb_ref[N_EXPERTS + 1:, :] = jnp.zeros((ROUTE_ROWS - N_EXPERTS - 1, gid.shape[1]), F32)


def _merge(x2, g, ya, attn, yc, layer, wg, bg, wa, wb, wc, wo, g2, wr, br, seq):
    n = x2.shape[0]
    tm = MERGE_ROW_TILE
    row = lambda width: pl.BlockSpec((tm, width), lambda i: (i, 0))
    of_layer = lambda rows, cols: pl.BlockSpec((1, rows, cols), lambda i: (layer, 0, 0))
    attn_flat = [t for pair in attn for t in pair]
    attn_specs = [_residue_spec(tm, t.shape[1], B_OUT, seq // tm) for t in attn_flat]
    return pl.pallas_call(
        _merge_kernel,
        grid=(n // tm,),
        in_specs=[row(D_MODEL), _full((1, D_MODEL)), row(A_WIDTH)] + attn_specs
        + [row(C_WIDTH), of_layer(D_MODEL, 3 * D_MODEL), _full((1, 3 * D_MODEL)),
           of_layer(A_WIDTH, D_MODEL), of_layer(B_OUT, D_MODEL), of_layer(C_WIDTH, D_MODEL),
           of_layer(D_MODEL, D_MODEL), _full((1, D_MODEL)), _full((32, D_MODEL)), _full((32, LANES))],
        out_specs=[row(D_MODEL), row(D_MODEL), pl.BlockSpec((ROUTE_ROWS, tm), lambda i: (0, i))],
        out_shape=[jax.ShapeDtypeStruct((n, D_MODEL), F32),
                   jax.ShapeDtypeStruct((n, D_MODEL), BF16),
                   jax.ShapeDtypeStruct((ROUTE_ROWS, n), F32)],
        scratch_shapes=[pltpu.VMEM((4, B_OUT // LANES, tm, LANES), F32)],
        compiler_params=_params("parallel"),
        name="merge_router",
    )(x2, g, ya, *attn_flat, yc, wg, bg, wa, wb, wc, wo, g2, wr, br)


def _group_ranks(gid):
    t = gid.shape[1]
    member = gid == lax.broadcasted_iota(jnp.int32, (8, t), 0).astype(F32)
    before = (lax.broadcasted_iota(jnp.int32, (t, t), 0)
              < lax.broadcasted_iota(jnp.int32, (t, t), 1)).astype(BF16)
    ranks = jnp.dot(member.astype(BF16), before, preferred_element_type=F32)
    return jnp.sum(jnp.where(member, ranks, 0.0), axis=0, keepdims=True)


def _group_onehots(gid, rank, cap):
    slot = lax.broadcasted_iota(jnp.int32, (cap, gid.shape[1]), 0).astype(F32)
    return [jnp.where(jnp.where(gid == float(g), rank, -1.0) == slot, 1.0, 0.0).astype(BF16)
            for g in range(N_GROUPS)]


def _expert_column(cw3, e):
    lane = lax.broadcasted_iota(jnp.int32, cw3.shape, 1)
    mine = (lane % N_EXPERTS == e) & (lane < 3 * N_EXPERTS)
    return jnp.sum(jnp.where(mine, cw3, 0.0), axis=-1, keepdims=True)


def _ffn(h, cw, wg, wu, wd):
    gate = jnp.dot(h, wg, preferred_element_type=F32)
    up = jnp.dot(h, wu, preferred_element_type=F32)
    hid = gate * _sigmoid(gate) * up * cw
    return jnp.dot(hid.astype(BF16), wd, preferred_element_type=F32)


def _moe_compact_kernel(h2_ref, route_ref, comb3_ref, xc_ref, cw_ref, rank_ref):
    gid = route_ref[N_EXPERTS:N_EXPERTS + 1, :]
    rank = _group_ranks(gid)
    rank_ref[0] = jnp.broadcast_to(rank, rank_ref.shape[1:])
    for g, p in enumerate(_group_onehots(gid, rank, xc_ref.shape[1])):
        xc_ref[g] = jnp.dot(p, h2_ref[...], preferred_element_type=F32).astype(BF16)
        cw_ref[g] = jnp.dot(p, comb3_ref[...], preferred_element_type=F32)


def _moe_expert_kernel(xc_ref, cw_ref, wg_ref, wu_ref, wd_ref, y_ref, hid_ref, wd_all_ref):
    j = pl.program_id(2)
    e = pl.program_id(0) * EXPERTS_PER_GROUP + j
    h = xc_ref[0]
    gate = jnp.dot(h, wg_ref[0, 0].astype(BF16), preferred_element_type=F32)
    up = jnp.dot(h, wu_ref[0, 0].astype(BF16), preferred_element_type=F32)
    hid_ref[j] = (gate * _sigmoid(gate) * up * _expert_column(cw_ref[0], e)).astype(BF16)
    wd_all_ref[j] = wd_ref[0, 0].astype(BF16)

    @pl.when(j == EXPERTS_PER_GROUP - 1)
    def _():
        hid = jnp.concatenate([hid_ref[k] for k in range(EXPERTS_PER_GROUP)], axis=-1)
        wd = wd_all_ref[...].reshape(EXPERTS_PER_GROUP * EXPERT_FF, D_MODEL)
        y_ref[0] = jnp.dot(hid, wd, preferred_element_type=F32).astype(BF16)


def _moe_scatter_kernel(xm_ref, route_ref, rank_ref, y_ref, gf_ref, out_ref, *, final_norm):
    cap = y_ref.shape[1]
    gid = route_ref[N_EXPERTS:N_EXPERTS + 1, :]
    p = jnp.concatenate(_group_onehots(gid, rank_ref[0, 0:1, :], cap), axis=0)
    y = y_ref[...].reshape(N_GROUPS * cap, D_MODEL)
    res = xm_ref[...] + lax.dot_general(p, y, TN, preferred_element_type=F32)
    out_ref[...] = _rmsnorm(res, gf_ref[...]) if final_norm else res


def _moe_dense_kernel(ids_ref, cnt_ref, xm_ref, h2_ref, comb3_ref, wg_ref, wu_ref, wd_ref, gf_ref,
                      prev_ref, out_ref, *, final_norm):
    del ids_ref, prev_ref
    e = pl.program_id(1)

    @pl.when(pl.program_id(0) < cnt_ref[0])
    def _():
        @pl.when(e == 0)
        def _():
            out_ref[...] = xm_ref[...]

        wg, wu, wd = (r[0, 0].astype(BF16) for r in (wg_ref, wu_ref, wd_ref))
        cw = _expert_column(comb3_ref[...].astype(F32), e)
        out_ref[...] += _ffn(h2_ref[...], cw, wg, wu, wd)

        if final_norm:
            @pl.when(e == N_EXPERTS - 1)
            def _():
                out_ref[...] = _rmsnorm(out_ref[...], gf_ref[...])


def _combine_pieces(route):
    n = route.shape[1]
    pieces, rest = [], route[:N_EXPERTS].T
    for _ in range(3):
        pieces.append(rest.astype(BF16))
        rest = rest - pieces[-1].astype(F32)
    return jnp.concatenate(pieces + [jnp.zeros((n, LANES - 3 * N_EXPERTS), BF16)], axis=1)


def _overflow_flags(route):
    gid = route[N_EXPERTS].astype(jnp.int32).reshape(-1, MOE_ROW_TILE)
    counts = jnp.sum(gid[:, :, None] == jnp.arange(N_GROUPS)[None, None, :], axis=1)
    return (jnp.max(counts, axis=1) > MOE_CAP).astype(jnp.int32)


def _moe_compacted(xm, h2, route, layer, wg, wu, wd, g_final, final_norm):
    n = xm.shape[0]
    tm, cap = MOE_ROW_TILE, MOE_CAP
    nt = n // tm
    comb3 = _combine_pieces(route)
    overflow = _overflow_flags(route)

    tile = lambda width: pl.BlockSpec((tm, width), lambda i: (i, 0))
    route_spec = pl.BlockSpec((ROUTE_ROWS, tm), lambda i: (0, i))
    slots = lambda width: pl.BlockSpec((N_GROUPS, cap, width), lambda i: (0, i, 0))
    rank_spec = pl.BlockSpec((1, 8, tm), lambda i: (i, 0, 0))
    xc, cw, rank = pl.pallas_call(
        _moe_compact_kernel,
        grid=(nt,),
        in_specs=[tile(D_MODEL), route_spec, tile(LANES)],
        out_specs=[slots(D_MODEL), slots(LANES), rank_spec],
        out_shape=[jax.ShapeDtypeStruct((N_GROUPS, nt * cap, D_MODEL), BF16),
                   jax.ShapeDtypeStruct((N_GROUPS, nt * cap, LANES), F32),
                   jax.ShapeDtypeStruct((nt, 8, tm), F32)],
        compiler_params=_params("parallel"),
        name="moe_compact",
    )(h2, route, comb3)

    halves = 4
    blk = nt * cap // halves
    rows_spec = lambda width: pl.BlockSpec((1, blk, width), lambda g, r, j: (g, r, 0))
    expert = lambda rows, cols: pl.BlockSpec(
        (1, 1, rows, cols), lambda g, r, j: (layer, g * EXPERTS_PER_GROUP + j, 0, 0))
    y = pl.pallas_call(
        _moe_expert_kernel,
        grid=(N_GROUPS, halves, EXPERTS_PER_GROUP),
        in_specs=[rows_spec(D_MODEL), rows_spec(LANES), expert(D_MODEL, EXPERT_FF),
                  expert(D_MODEL, EXPERT_FF), expert(EXPERT_FF, D_MODEL)],
        out_specs=rows_spec(D_MODEL),
        out_shape=jax.ShapeDtypeStruct((N_GROUPS, nt * cap, D_MODEL), BF16),
        scratch_shapes=[pltpu.VMEM((EXPERTS_PER_GROUP, blk, EXPERT_FF), BF16),
                        pltpu.VMEM((EXPERTS_PER_GROUP, EXPERT_FF, D_MODEL), BF16)],
        compiler_params=_params("parallel", "parallel", "arbitrary"),
        name="moe_experts",
    )(xc, cw, wg, wu, wd)

    out = pl.pallas_call(
        functools.partial(_moe_scatter_kernel, final_norm=final_norm),
        grid=(nt,),
        in_specs=[tile(D_MODEL), route_spec, rank_spec, slots(D_MODEL), _full((1, D_MODEL))],
        out_specs=tile(D_MODEL),
        out_shape=jax.ShapeDtypeStruct((n, D_MODEL), F32),
        compiler_params=_params("parallel"),
        name="moe_scatter",
    )(xm, route, rank, y, g_final)

    return out, comb3, overflow


def _moe_dense(ids, count, prev, xm, h2, comb3, layer, wg, wu, wd, g_final, final_norm):
    n = xm.shape[0]
    tm = MOE_ROW_TILE
    tile = lambda width: pl.BlockSpec((tm, width), lambda s, e, ids, cnt: (ids[s], 0))
    expert = lambda rows, cols: pl.BlockSpec(
        (1, 1, rows, cols), lambda s, e, ids, cnt: (layer, jnp.where(s < cnt[0], e, 0), 0, 0))
    grid_spec = pltpu.PrefetchScalarGridSpec(
        num_scalar_prefetch=2,
        grid=(ids.shape[0], N_EXPERTS),
        in_specs=[tile(D_MODEL), tile(D_MODEL), tile(LANES),
                  expert(D_MODEL, EXPERT_FF), expert(D_MODEL, EXPERT_FF), expert(EXPERT_FF, D_MODEL),
                  pl.BlockSpec((1, D_MODEL), lambda s, e, ids, cnt: (0, 0)),
                  pl.BlockSpec(memory_space=pl.ANY)],
        out_specs=tile(D_MODEL))
    return pl.pallas_call(
        functools.partial(_moe_dense_kernel, final_norm=final_norm),
        grid_spec=grid_spec,
        out_shape=jax.ShapeDtypeStruct((n, D_MODEL), F32),
        input_output_aliases={9: 0},
        compiler_params=_params("arbitrary", "arbitrary"),
        name="moe_dense",
    )(ids, count, xm, h2, comb3, wg, wu, wd, g_final, prev)


def _moe(xm, h2, route, layer, wg, wu, wd, g_final, final_norm):
    nt = xm.shape[0] // MOE_ROW_TILE
    dense_args = (layer, wg, wu, wd, g_final, final_norm)

    def compacted(_):
        return _moe_compacted(xm, h2, route, layer, wg, wu, wd, g_final, final_norm)

    def with_fix(_):
        out, comb3, overflow = compacted(None)
        order = jnp.argsort(-overflow, stable=True).astype(jnp.int32)[:MOE_FIX_TILES]
        count = jnp.sum(overflow).astype(jnp.int32)
        ids = jnp.where(jnp.arange(MOE_FIX_TILES) < count, order, order[jnp.maximum(count - 1, 0)])
        return _moe_dense(ids, count[None], out, xm, h2, comb3, *dense_args)

    def all_dense(_):
        comb3 = _combine_pieces(route)
        return _moe_dense(jnp.arange(nt, dtype=jnp.int32), jnp.full((1,), nt, jnp.int32), xm, xm, h2,
                          comb3, *dense_args)

    n_over = jnp.sum(_overflow_flags(route))
    case = jnp.where(n_over == 0, 0, jnp.where(n_over <= MOE_FIX_TILES, 1, 2))
    return lax.switch(case, [lambda _: compacted(None)[0], with_fix, all_dense], None)


def _pack_lora(mu_wag, w1, a1, g1, mu_v, v1):
    d = D_MODEL
    if v1 is None:
        mu_v = jnp.zeros((d,), F32)
        v1 = jnp.zeros((d, V_LORA), F32)
    pad = jnp.zeros((d, LORA_HALF - (W_LORA + A_LORA + G_LORA + V_LORA)), F32)
    mus = (mu_wag[0], mu_wag[1], mu_wag[2], mu_v)
    ws = (w1, a1, g1, v1)
    cur = [w * (1.0 - m)[:, None] for w, m in zip(ws, mus)]
    prev = [w * m[:, None] for w, m in zip(ws, mus)]
    return jnp.concatenate(cur + [pad] + prev + [pad], axis=1)


def _pad_rows(w, lo, total=LANES):
    return jnp.zeros((total, w.shape[1]), F32).at[lo:lo + w.shape[0]].set(w)


def kernel(x, positions, norm_mix_g, w_in, gmlp_ln_g, gmlp_ln_b, gmlp_ws, gmlp_bs, rwkv_mu_rkv, rwkv_mu_wag, rwkv_w0, rwkv_w1, rwkv_w2, rwkv_a0, rwkv_a1, rwkv_a2, rwkv_g1, rwkv_g2, rwkv_k_k, rwkv_k_a, rwkv_r_k, rwkv_ln_w, rwkv_ln_b, rwkv_mu_v, rwkv_v0, rwkv_v1, rwkv_v2, w_branch_a, w_branch_b, w_branch_c, w_gate, b_gate, w_out, norm_ffn_g, router_group_w, router_group_b, router_expert_w, router_expert_b, expert_w_gate, expert_w_up, expert_w_down, final_norm_g):
    batch, seq, d = x.shape
    depth = w_in.shape[0]
    n = batch * seq
    x2 = x.reshape(n, d)
    rope = _rope_tables(positions)
    w_in_b, w_gate_b, w_out_b = w_in.astype(BF16), w_gate.astype(BF16), w_out.astype(BF16)
    wba, wbb, wbc = w_branch_a.astype(BF16), w_branch_b.astype(BF16), w_branch_c.astype(BF16)
    v_first = None
    for l in range(depth):
        vres = l > 0
        lora_w = _pack_lora(rwkv_mu_wag[l], rwkv_w1[l], rwkv_a1[l], rwkv_g1[l],
                            rwkv_mu_v[l - 1] if vres else None, rwkv_v1[l - 1] if vres else None)
        ya, qkv0, qkv1, qkv2, rkv, lora = _inproj(
            x2, norm_mix_g[l][None], w_in_b, l, lora_w.astype(BF16), gmlp_ln_g[l][None],
            gmlp_ln_b[l][None], gmlp_ws[l], gmlp_bs[l].T, rope, batch, seq)

        attn = [_dilated_attention(qkv, dil)
                for qkv, (_, dil) in zip((qkv0, qkv1, qkv2), DILATED_PATTERNS)]

        zero = jnp.zeros((C_WIDTH,), F32)
        vecs = jnp.stack([rwkv_w0[l], rwkv_a0[l], rwkv_v0[l - 1] if vres else zero, rwkv_k_k[l],
                          rwkv_k_a[l], rwkv_r_k[l].reshape(C_WIDTH), rwkv_ln_w[l], rwkv_ln_b[l]])
        yc, v_c = _rwkv(rkv, lora, v_first, rwkv_mu_rkv[l], vecs,
                        _pad_rows(rwkv_w2[l], 0), _pad_rows(rwkv_a2[l], W_LORA), rwkv_g2[l],
                        _pad_rows(rwkv_v2[l - 1], 0) if vres else None, batch, seq)
        if l == 0:
            v_first = v_c

        wr = jnp.concatenate([router_group_w[l], router_expert_w[l]], axis=1).T
        wr = jnp.zeros((32, d), F32).at[:wr.shape[0]].set(wr)
        br = jnp.concatenate([router_group_b[l], router_expert_b[l]])
        br = jnp.zeros((32, LANES), F32).at[:br.shape[0], 0].set(br)
        xm, h2, route = _merge(
            x2, norm_mix_g[l][None], ya, attn, yc, l, w_gate_b, b_gate[l][None], wba, wbb, wbc,
            w_out_b, norm_ffn_g[l][None], wr, br, seq)

        x2 = _moe(xm, h2, route, l, expert_w_gate, expert_w_up, expert_w_down, final_norm_g[None],
                  final_norm=(l == depth - 1))
    return x2.reshape(batch, seq, d)
```

```python
import functools

import numpy as np
import jax
import jax.numpy as jnp
from jax import lax
from jax.experimental import pallas as pl
from jax.experimental.pallas import tpu as pltpu

F32 = jnp.float32
BF16 = jnp.bfloat16

D_MODEL = 1024
HEAD_DIM = 64
A_GROUPS = 4
A_WIDTH = 512
CHUNK = 128
DILATED_PATTERNS = ((128, 1), (512, 4), (2048, 16))
B_HEADS_PER_GROUP = 4
B_WIDTH = 768
B_OUT = 256
BLOCK = 128
ROPE_THETA = 500000.0
ROPE_DIMS = 16
C_HEADS = 8
C_WIDTH = 512
W_LORA, A_LORA, V_LORA, G_LORA = 64, 64, 32, 128
RWKV_GN_EPS = HEAD_DIM * 1e-5
IN_COLS = 2 * A_WIDTH + 3 * B_WIDTH + 3 * C_WIDTH
LORA_HALF = 384
LORA_COLS = 2 * LORA_HALF
N_BRANCHES = 3
N_GROUPS = 4
EXPERTS_PER_GROUP = 4
N_EXPERTS = 16
EXPERT_FF = 512
RMS_EPS = 1e-6
LN_EPS = 1e-5

LANES = 128
ROW_TILE = 512
MERGE_ROW_TILE = 512
ATTN_BLOCKS_TOGETHER = 4
RWKV_CHUNK = 64
RWKV_BATCH_ROWS = 8
RWKV_SUB_ROWS = 2
MOE_ROW_TILE = 1024
MOE_CAP = 352
MOE_FIX_TILES = 2
ROUTE_ROWS = 24
VMEM_LIMIT = 56 * 1024 * 1024
NEG_BIG = -1e30


def _params(*sem):
    return pltpu.CompilerParams(dimension_semantics=sem, vmem_limit_bytes=VMEM_LIMIT)


def _full(shape):
    nd = len(shape)
    return pl.BlockSpec(shape, lambda *_: (0,) * nd)


def _gelu_tanh(x):
    return 0.5 * x * (1.0 + jnp.tanh(0.7978845608028654 * (x + 0.044715 * (x * x * x))))


def _sigmoid(x):
    return 0.5 * jnp.tanh(0.5 * x) + 0.5


def _rmsnorm(x, g):
    return x * lax.rsqrt(jnp.mean(x * x, axis=-1, keepdims=True) + RMS_EPS) * g


def _rope_table_kernel(pos_ref, consts_ref, c_ref, sa_ref, sb_ref):
    ang = pos_ref[...].astype(F32) * consts_ref[0:1, :]
    c_ref[...] = jnp.cos(ang)
    s = jnp.sin(ang)
    sa_ref[...] = -s * consts_ref[1:2, :]
    sb_ref[...] = s * consts_ref[2:3, :]


def _rope_tables(positions):
    n = positions.size
    half = ROPE_DIMS // 2
    inv_freq = ROPE_THETA ** (-jnp.arange(half, dtype=F32) / half)
    lane = np.arange(LANES) % HEAD_DIM
    in_rope = lane < ROPE_DIMS
    invf_row = jnp.where(in_rope, inv_freq[lane % half], 0.0)
    consts = jnp.zeros((8, LANES), F32)
    consts = consts.at[0].set(invf_row)
    consts = consts.at[1].set(jnp.asarray(lane < half, F32))
    consts = consts.at[2].set(jnp.asarray((lane >= half) & in_rope, F32))
    tm = 1024
    out = jax.ShapeDtypeStruct((n, LANES), F32)
    return pl.pallas_call(
        _rope_table_kernel,
        grid=(n // tm,),
        in_specs=[pl.BlockSpec((tm, 1), lambda i: (i, 0)), _full((8, LANES))],
        out_specs=[pl.BlockSpec((tm, LANES), lambda i: (i, 0))] * 3,
        out_shape=[out, out, out],
        compiler_params=_params("parallel"),
        name="rope_tables",
    )(positions.reshape(n, 1), consts)


def _inproj_kernel(x_ref, g_ref, w_ref, wl_ref, lng_ref, lnb_ref, ws_ref, bst_ref, c_ref, sa_ref,
                   sb_ref, ya_ref, qkv0_ref, qkv1_ref, qkv2_ref, rkv_ref, lora_ref, nat_ref):
    tm = x_ref.shape[0]
    hb = _rmsnorm(x_ref[...], g_ref[...]).astype(BF16)

    def proj(lo, width):
        return jnp.dot(hb, w_ref[0, :, lo:lo + width], preferred_element_type=F32)

    q_off = 2 * A_WIDTH
    c_off = q_off + 3 * B_WIDTH
    uv = proj(0, 2 * A_WIDTH)
    rkv_ref[:, 0:C_WIDTH] = proj(c_off, C_WIDTH)
    u = _gelu_tanh(uv[:, :A_WIDTH])
    rkv_ref[:, C_WIDTH:2 * C_WIDTH] = proj(c_off + C_WIDTH, C_WIDTH)
    v = _gelu_tanh(uv[:, A_WIDTH:])
    rkv_ref[:, 2 * C_WIDTH:] = proj(c_off + 2 * C_WIDTH, C_WIDTH)
    mu = jnp.mean(v, axis=-1, keepdims=True)
    d = v - mu
    var = jnp.mean(d * d, axis=-1, keepdims=True)
    lora_ref[...] = jnp.dot(hb, wl_ref[...], preferred_element_type=F32)
    vn = (d * lax.rsqrt(var + LN_EPS) * lng_ref[...] + lnb_ref[...]).astype(BF16)
    row = lax.broadcasted_iota(jnp.int32, (CHUNK, CHUNK), 0)
    col = lax.broadcasted_iota(jnp.int32, (CHUNK, CHUNK), 1)
    for g in range(A_GROUPS):
        wg = jnp.where(row >= col, ws_ref[g], 0.0).astype(BF16)
        bias = bst_ref[:, g:g + 1]
        cs = slice(g * LANES, (g + 1) * LANES)
        for c in range(tm // CHUNK):
            rs = slice(c * CHUNK, (c + 1) * CHUNK)
            mixed = jnp.dot(wg, vn[rs, cs], preferred_element_type=F32) + bias
            ya_ref[rs, cs] = (u[rs, cs] * mixed).astype(BF16)

    cos, sa, sb = c_ref[...], sa_ref[...], sb_ref[...]
    for grp, (qkv_ref, (_, dil)) in enumerate(zip((qkv0_ref, qkv1_ref, qkv2_ref), DILATED_PATTERNS)):
        acc = [proj(q_off + part * B_WIDTH + grp * B_OUT, B_OUT) for part in range(3)]
        for c in range(B_WIDTH // LANES):
            t = acc[c * LANES // B_OUT][:, c * LANES % B_OUT:c * LANES % B_OUT + LANES]
            if c < 2 * B_OUT // LANES:
                t = t * cos + pltpu.roll(t, LANES - 8, 1) * sa + pltpu.roll(t, 8, 1) * sb
            if c < B_OUT // LANES:
                t = t * (HEAD_DIM ** -0.5)
            cs = slice(c * LANES, (c + 1) * LANES)
            if dil == 1:
                qkv_ref[0, 0, :, cs] = t.astype(BF16)
            else:
                nat_ref[c] = t
                for r in range(dil):
                    qkv_ref[0, r, :, cs] = nat_ref[c, pl.ds(r, tm // dil, stride=dil), :].astype(BF16)


def _residue_spec(tm, dil, width, tiles_per_seq):
    return pl.BlockSpec((1, dil, tm // dil, width),
                        lambda i: (i // tiles_per_seq, 0, i % tiles_per_seq, 0))


def _inproj(x2, g, w_in, layer, w_lora, ln_g, ln_b, ws, bs_t, rope, batch, seq):
    n = x2.shape[0]
    tm = ROW_TILE
    row = lambda width: pl.BlockSpec((tm, width), lambda i: (i, 0))
    dils = [d for _, d in DILATED_PATTERNS]
    return pl.pallas_call(
        _inproj_kernel,
        grid=(n // tm,),
        in_specs=[row(D_MODEL), _full((1, D_MODEL)),
                  pl.BlockSpec((1, D_MODEL, IN_COLS), lambda i: (layer, 0, 0)),
                  _full((D_MODEL, LORA_COLS)),
                  _full((1, A_WIDTH)), _full((1, A_WIDTH)), _full((A_GROUPS, CHUNK, CHUNK)),
                  _full((CHUNK, A_GROUPS)), row(LANES), row(LANES), row(LANES)],
        out_specs=[row(A_WIDTH)] + [_residue_spec(tm, d, B_WIDTH, seq // tm) for d in dils]
        + [row(3 * C_WIDTH), row(LORA_COLS)],
        out_shape=[jax.ShapeDtypeStruct((n, A_WIDTH), BF16)]
        + [jax.ShapeDtypeStruct((batch, d, seq // d, B_WIDTH), BF16) for d in dils]
        + [jax.ShapeDtypeStruct((n, 3 * C_WIDTH), F32),
           jax.ShapeDtypeStruct((n, LORA_COLS), F32)],
        scratch_shapes=[pltpu.VMEM((B_WIDTH // LANES, tm, LANES), F32)],
        compiler_params=_params("parallel"),
        name="inproj",
    )(x2, g, w_in, w_lora, ln_g, ln_b, ws, bs_t, *rope)


def _attn_blocks(blocks):
    lo = lax.broadcasted_iota(jnp.int32, (BLOCK, LANES), 1) < HEAD_DIM
    pairs = range(B_OUT // LANES)
    heads = range(B_HEADS_PER_GROUP)
    scores = []
    for q, kw, _, valid in blocks:
        for p in pairs:
            ps = slice(p * LANES, (p + 1) * LANES)
            qp = q[:, ps]
            zero = jnp.zeros_like(qp)
            for part in (jnp.where(lo, qp, zero), jnp.where(lo, zero, qp)):
                s = lax.dot_general(part, kw[:, ps], NT, preferred_element_type=F32)
                scores.append(jnp.where(valid, s, NEG_BIG))
    m = [jnp.max(s, axis=-1, keepdims=True) for s in scores]
    e = [jnp.exp(s - mx) for s, mx in zip(scores, m)]
    l = [jnp.sum(t, axis=-1, keepdims=True) for t in e]
    pv = [jnp.dot(e[len(heads) * b + h].astype(BF16),
                  blocks[b][2][:, (h // 2) * LANES:(h // 2 + 1) * LANES],
                  preferred_element_type=F32)
          for b in range(len(blocks)) for h in heads]
    lse = [mx + jnp.log(t) for mx, t in zip(m, l)]
    results = []
    for b in range(len(blocks)):
        at = lambda p, h: len(heads) * b + 2 * p + h
        out = [jnp.where(lo, pv[at(p, 0)] / l[at(p, 0)], pv[at(p, 1)] / l[at(p, 1)]) for p in pairs]
        ls = [jnp.where(lo, lse[at(p, 0)], lse[at(p, 1)]) for p in pairs]
        results.append((jnp.concatenate(out, axis=-1), jnp.concatenate(ls, axis=-1)))
    return results


def _attn_kernel(qkv_ref, o_ref, l_ref):
    dil, sub = qkv_ref.shape[1], qkv_ref.shape[2]
    nb = sub // BLOCK
    qs, ks, vs = (slice(i * B_OUT, (i + 1) * B_OUT) for i in range(3))
    qi = lax.broadcasted_iota(jnp.int32, (BLOCK, BLOCK), 0)
    kj = lax.broadcasted_iota(jnp.int32, (BLOCK, BLOCK), 1)
    qi2 = lax.broadcasted_iota(jnp.int32, (BLOCK, 2 * BLOCK), 0)
    kj2 = lax.broadcasted_iota(jnp.int32, (BLOCK, 2 * BLOCK), 1)
    valid2 = (kj2 >= qi2) & (kj2 <= qi2 + BLOCK)

    def first_block(r):
        rows = pl.ds(0, BLOCK)
        return rows, (qkv_ref[0, r, rows, qs], qkv_ref[0, r, rows, ks], qkv_ref[0, r, rows, vs],
                      kj <= qi)

    def later_block(r, n):
        rows = pl.ds(pl.multiple_of(n * BLOCK, BLOCK), BLOCK)
        win = pl.ds(pl.multiple_of((n - 1) * BLOCK, BLOCK), 2 * BLOCK)
        return rows, (qkv_ref[0, r, rows, qs], qkv_ref[0, r, win, ks], qkv_ref[0, r, win, vs], valid2)

    def run(work):
        for (o, l), (r, rows, _) in zip(_attn_blocks([w[2] for w in work]), work):
            o_ref[0, r, rows, :] = o
            l_ref[0, r, rows, :] = l

    together = ATTN_BLOCKS_TOGETHER
    if nb == 1:
        def body(i, carry):
            run([(r,) + first_block(r) for r in (together * i + t for t in range(together))])
            return carry
        lax.fori_loop(0, dil // together, body, 0)
    else:
        for r in range(dil):
            run([(r,) + first_block(r)] + [(r,) + later_block(r, n) for n in range(1, together)])

            def body(i, carry):
                run([(r,) + later_block(r, together * i + t) for t in range(together)])
                return carry
            lax.fori_loop(1, nb // together, body, 0)


def _dilated_attention(qkv, dilation):
    batch, _, sub, _ = qkv.shape
    ospec = pl.BlockSpec((1, dilation, sub, B_OUT), lambda b: (b, 0, 0, 0))
    oshape = jax.ShapeDtypeStruct((batch, dilation, sub, B_OUT), F32)
    return pl.pallas_call(
        _attn_kernel,
        grid=(batch,),
        in_specs=[pl.BlockSpec((1, dilation, sub, B_WIDTH), lambda b: (b, 0, 0, 0))],
        out_specs=[ospec, ospec],
        out_shape=[oshape, oshape],
        compiler_params=_params("parallel"),
        name=f"dilated_attn_d{dilation}",
    )(qkv)


def _shift_rows(x, prev_row):
    rolled = pltpu.roll(x, 1, 0)
    first = lax.broadcasted_iota(jnp.int32, x.shape, 0) == 0
    return jnp.where(first, prev_row, rolled)


NN = (((1,), (0,)), ((), ()))
NT = (((1,), (1,)), ((), ()))
TN = (((0,), (0,)), ((), ()))


def _dot(a, b, dims=NN):
    return lax.dot_general(a.astype(BF16), b.astype(BF16), dims, preferred_element_type=F32)


def _split(a, pieces):
    out = []
    for _ in range(pieces):
        p = a.astype(BF16)
        out.append(p)
        a = a - p.astype(F32)
    return out


def _dot3(a, b, dims=NN):
    (ah, al), (bh, bl) = _split(a, 2), _split(b, 2)
    return _dot(ah, bh, dims) + _dot(ah, bl, dims) + _dot(al, bh, dims)


def _head_sum(x):
    lo = lax.broadcasted_iota(jnp.int32, (x.shape[0], LANES), 1) < HEAD_DIM
    outs = []
    for c in range(x.shape[1] // LANES):
        xc = x[:, c * LANES:(c + 1) * LANES]
        s_lo = jnp.sum(jnp.where(lo, xc, 0.0), axis=-1, keepdims=True)
        s_hi = jnp.sum(jnp.where(lo, 0.0, xc), axis=-1, keepdims=True)
        outs.append(jnp.where(lo, s_lo, s_hi))
    return jnp.concatenate(outs, axis=-1)


def _finish(gen):
    try:
        while True:
            next(gen)
    except StopIteration as stop:
        return stop.value


def _interleave(first, second):
    values, gens = [None, None], [first, second]
    live = [True, True]
    while any(live):
        for i, gen in enumerate(gens):
            if live[i]:
                try:
                    next(gen)
                except StopIteration as stop:
                    values[i], live[i] = stop.value, False
    return values


def _rwkv_kernel(*refs, has_vres):
    if has_vres:
        (rkv_ref, lora_ref, vf_ref, mu_ref, vec_ref, w2_ref, a2_ref, g2_ref, v2_ref,
         yc_ref, vout_ref, state_ref, prkv_ref, plora_ref, y_ref) = refs
    else:
        (rkv_ref, lora_ref, mu_ref, vec_ref, w2_ref, a2_ref, g2_ref,
         yc_ref, vout_ref, state_ref, prkv_ref, plora_ref, y_ref) = refs
    nbat, ch = rkv_ref.shape[0], rkv_ref.shape[1]
    sub = RWKV_SUB_ROWS
    rows = sub * ch
    hd = HEAD_DIM
    npair = C_HEADS // 2
    row_of = [slice(j * ch, (j + 1) * ch) for j in range(sub)]

    @pl.when(pl.program_id(1) == 0)
    def _():
        state_ref[...] = jnp.zeros_like(state_ref)
        prkv_ref[...] = jnp.zeros_like(prkv_ref)
        plora_ref[...] = jnp.zeros_like(plora_ref)

    w0, a0, v0 = vec_ref[0:1, :], vec_ref[1:2, :], vec_ref[2:3, :]
    k_k, k_a, r_k = vec_ref[3:4, :], vec_ref[4:5, :], vec_ref[5:6, :]
    ln_w, ln_b = vec_ref[6:7, :], vec_ref[7:8, :]

    ti = lax.broadcasted_iota(jnp.int32, (rows, rows), 0)
    si = lax.broadcasted_iota(jnp.int32, (rows, rows), 1)
    tri = ((si <= ti) & (si // ch == ti // ch)).astype(BF16)
    unit = (lax.broadcasted_iota(jnp.int32, (8, LANES), 0) == 0).astype(BF16)
    lane = lax.broadcasted_iota(jnp.int32, (ch, LANES), 1)
    lo = lane < hd
    lo2 = lax.broadcasted_iota(jnp.int32, (2 * ch, LANES), 1) < hd
    row4 = lax.broadcasted_iota(jnp.int32, (4 * ch, LANES), 0)
    col4 = lax.broadcasted_iota(jnp.int32, (4 * ch, LANES), 1) % ch
    t4 = row4 % ch
    tri_mask = (col4 < t4) | ((col4 == t4) & ((row4 // ch) % 2 == 1))
    eye_right = (lane - hd == lax.broadcasted_iota(jnp.int32, (ch, LANES), 0)).astype(F32)
    brow = lax.broadcasted_iota(jnp.int32, (LANES, LANES), 0) // hd
    bcol = lax.broadcasted_iota(jnp.int32, (LANES, LANES), 1) // hd
    block_diag = brow == bcol
    zeros = jnp.zeros((ch, LANES), F32)

    def prepare(grp):
        bs = slice(grp * sub, (grp + 1) * sub)

        def shifted(x, prev_ref):
            parts = []
            for j, rs in enumerate(row_of):
                slot = 8 * (grp * sub + j)
                parts.append(_shift_rows(x[rs], prev_ref[slot:slot + 1, :]))
                prev_ref[slot:slot + 1, :] = x[(j + 1) * ch - 1:(j + 1) * ch, :]
            return jnp.concatenate(parts, axis=0)

        rkv = rkv_ref[bs].reshape(rows, 3 * C_WIDTH)
        rkv_prev = shifted(rkv, prkv_ref)
        lora_all = lora_ref[bs].reshape(rows, LORA_COLS)
        lora = lora_all[:, :LORA_HALF] + shifted(lora_all[:, LORA_HALF:], plora_ref)
        yield

        def mix(i):
            cur = rkv[:, i * C_WIDTH:(i + 1) * C_WIDTH]
            prev = rkv_prev[:, i * C_WIDTH:(i + 1) * C_WIDTH]
            return cur + (prev - cur) * mu_ref[i:i + 1, :]

        r, k, v = mix(0), mix(1), mix(2)
        yield
        wa = lora[:, 0:LANES]
        z = w0 + _dot3(jnp.tanh(wa), w2_ref[...])
        log_decay = -float(np.exp(-0.5)) * _sigmoid(z)
        yield
        a = _sigmoid(a0 + _dot(wa, a2_ref[...]))
        gate = _dot(_sigmoid(lora[:, LANES:2 * LANES]), g2_ref[...])
        if has_vres:
            vmix = _sigmoid(v0 + _dot(lora[:, 2 * LANES:3 * LANES], v2_ref[...]))
            v = v + (vf_ref[bs].reshape(rows, C_WIDTH) - v) * vmix
        vout_ref[bs] = v.reshape(sub, ch, C_WIDTH)
        yield

        kk = k * k_k
        kk = kk * lax.rsqrt(jnp.maximum(_head_sum(kk * kk), 1e-24))
        yield
        k = k * (1.0 + (a - 1.0) * k_a)
        bonus = _head_sum(r * k * r_k) * v
        b = kk * a
        yield

        cum = functools.reduce(lambda s, t: s + t, [_dot(tri, p) for p in _split(log_decay, 3)])
        ends = [cum[(j + 1) * ch - 1:(j + 1) * ch, :] for j in range(sub)]
        cum_end = jnp.concatenate([jnp.broadcast_to(e, (ch, C_WIDTH)) for e in ends], axis=0)
        yield
        e_neg = jnp.exp(-cum)
        e_end = jnp.exp(cum_end - cum)
        kt = kk * jnp.exp(cum - log_decay)
        yield
        rt = r * jnp.exp(cum)
        bt = b * e_neg
        kq = k * e_neg
        yield
        kh = k * e_end
        bh = b * e_end
        w_end_t = [functools.reduce(lambda s, t: s + t,
                                    [_dot(jnp.broadcast_to(p, (8, C_WIDTH)), unit, TN)
                                     for p in _split(jnp.exp(e), 2)])
                   for e in ends]
        yield
        return dict(kt=kt, rt=rt, bt=bt, kq=kq, kh=kh, bh=bh, v=v, bonus=bonus, gate=gate,
                    w_end_t=w_end_t)

    def solve(grp, t):
        chains = [(j, p) for j in range(sub) for p in range(npair)]
        at = lambda x, c: x[row_of[c[0]], c[1] * LANES:(c[1] + 1) * LANES]
        idx = range(len(chains))
        slot_of = lambda c: (grp * sub + c[0]) * npair + c[1]
        s_bd = [state_ref[slot_of(c)] for c in chains]
        l2 = [jnp.concatenate([at(t['kt'], c), at(t['rt'], c)], axis=0) for c in chains]
        ss = [_dot(l2[i], s_bd[i]) for i in idx]
        yield
        g = [jnp.where(tri_mask,
                       _dot(jnp.concatenate([jnp.where(lo2, l2[i], 0.0),
                                             jnp.where(lo2, 0.0, l2[i])], axis=0),
                            jnp.concatenate([at(t['bt'], c), at(t['kq'], c)], axis=0), NT), 0.0)
             for i, c in zip(idx, chains)]
        yield
        xv = [_dot(jnp.concatenate([g[i][0:ch], g[i][2 * ch:3 * ch]], axis=0),
                   jnp.concatenate([zeros, at(t['v'], c)], axis=0))
              for i, c in zip(idx, chains)]
        yield
        heads = [(i, h) for i in idx for h in range(2)]
        n0 = [-g[i][2 * h * ch:(2 * h + 1) * ch] for i, h in heads]
        w = [_dot(n[:, :ch], jnp.where(lo, n, eye_right)) + jnp.where(lo, 0.0, eye_right)
             for n in n0]
        yield
        for _ in range(int(np.log2(ch)) - 1):
            w = [_dot(x[:, :ch], x) + jnp.where(lo, 0.0, x) for x in w]
            yield
        us = [_dot(w[2 * i + h],
                   jnp.concatenate([zeros, ss[i][0:ch] + xv[i][h * ch:(h + 1) * ch]], axis=0))
              for i, h in heads]
        u = [jnp.where(lo, us[2 * i], us[2 * i + 1]) for i in idx]
        yield
        yv = [_dot(jnp.concatenate([g[i][ch:2 * ch], g[i][3 * ch:4 * ch]], axis=0),
                   jnp.concatenate([-u[i], at(t['v'], c)], axis=0))
              for i, c in zip(idx, chains)]
        yield
        upd = [_dot(jnp.concatenate([at(t['kh'], c), at(t['bh'], c)], axis=0),
                    jnp.concatenate([at(t['v'], c), -u[i]], axis=0), TN)
               for i, c in zip(idx, chains)]
        yield
        ys = slice(grp * rows, (grp + 1) * rows)
        for i, c in zip(idx, chains):
            ps = slice(c[1] * LANES, (c[1] + 1) * LANES)
            y_ref[grp * rows + c[0] * ch:grp * rows + (c[0] + 1) * ch, ps] = (
                ss[i][ch:2 * ch] + jnp.where(lo, yv[i][0:ch], yv[i][ch:2 * ch]))
            state_ref[slot_of(c)] = (t['w_end_t'][c[0]][ps, :] * s_bd[i]
                                     + jnp.where(block_diag, upd[i], 0.0))
        yield
        y = y_ref[ys, :]
        d = y - _head_sum(y) * (1.0 / hd)
        var = _head_sum(d * d) * (1.0 / hd)
        yn = d * lax.rsqrt(var + RWKV_GN_EPS) * ln_w + ln_b
        yc_ref[grp * sub:(grp + 1) * sub] = (
            ((yn + t['bonus']) * t['gate']).astype(BF16).reshape(sub, ch, C_WIDTH))

    groups = nbat // sub
    prepared = _finish(prepare(0))
    for grp in range(groups):
        if grp + 1 < groups:
            _, prepared_next = _interleave(solve(grp, prepared), prepare(grp + 1))
            prepared = prepared_next
        else:
            _finish(solve(grp, prepared))


def _rwkv(rkv, lora, v_first, mu_rkv, vecs, w2, a2, g2, v2, batch, seq):
    ch = RWKV_CHUNK
    nbat = RWKV_BATCH_ROWS
    has_vres = v_first is not None
    per_seq = lambda t: t.reshape(batch, seq, t.shape[-1])
    blk = lambda width: pl.BlockSpec((nbat, ch, width), lambda i, c: (i, c, 0))
    ins = [per_seq(rkv), per_seq(lora)] + ([per_seq(v_first)] if has_vres else []) \
        + [mu_rkv, vecs, w2, a2, g2] + ([v2] if has_vres else [])
    specs = [blk(3 * C_WIDTH), blk(LORA_COLS)] + ([blk(C_WIDTH)] if has_vres else []) \
        + [_full((3, C_WIDTH)), _full((8, C_WIDTH)), _full((LANES, C_WIDTH)),
           _full((LANES, C_WIDTH)), _full((LANES, C_WIDTH))] \
        + ([_full((LANES, C_WIDTH))] if has_vres else [])
    yc, v_out = pl.pallas_call(
        functools.partial(_rwkv_kernel, has_vres=has_vres),
        grid=(batch // nbat, seq // ch),
        in_specs=specs,
        out_specs=[blk(C_WIDTH), blk(C_WIDTH)],
        out_shape=[jax.ShapeDtypeStruct((batch, seq, C_WIDTH), BF16),
                   jax.ShapeDtypeStruct((batch, seq, C_WIDTH), F32)],
        scratch_shapes=[pltpu.VMEM((nbat * C_HEADS // 2, LANES, LANES), F32),
                        pltpu.VMEM((8 * nbat, 3 * C_WIDTH), F32),
                        pltpu.VMEM((8 * nbat, LORA_HALF), F32),
                        pltpu.VMEM((nbat * ch, C_WIDTH), F32)],
        compiler_params=_params("parallel", "arbitrary"),
        name="rwkv7_chunked",
    )(*ins)
    return yc.reshape(batch * seq, C_WIDTH), v_out.reshape(batch * seq, C_WIDTH)


def _merge_kernel(x_ref, g_ref, ya_ref, o0_ref, l0_ref, o1_ref, l1_ref, o2_ref, l2_ref, yc_ref,
                  wg_ref, bg_ref, wa_ref, wb_ref, wc_ref, wo_ref, g2_ref, wr_ref, br_ref,
                  xm_ref, h2_ref, comb_ref, nat_ref):
    x = x_ref[...]
    tm = x.shape[0]
    hb = _rmsnorm(x, g_ref[...]).astype(BF16)

    def token_order(ref, slot):
        dil = ref.shape[1]
        if dil == 1:
            return ref[0, 0]
        chunks = []
        for c in range(B_OUT // LANES):
            for r in range(dil):
                nat_ref[slot, c, pl.ds(r, tm // dil, stride=dil), :] = \
                    ref[0, r, :, c * LANES:(c + 1) * LANES]
            chunks.append(nat_ref[slot, c])
        return jnp.concatenate(chunks, axis=-1)

    def attention_branch():
        l0, l1, l2 = token_order(l0_ref, 0), token_order(l1_ref, 0), token_order(l2_ref, 1)
        lm = jnp.maximum(jnp.maximum(l0, l1), l2)
        e0, e1, e2 = jnp.exp(l0 - lm), jnp.exp(l1 - lm), jnp.exp(l2 - lm)
        o0, o1, o2 = token_order(o0_ref, 0), token_order(o1_ref, 2), token_order(o2_ref, 3)
        return ((e0 * o0 + e1 * o1 + e2 * o2) / (e0 + e1 + e2)).astype(BF16)

    inputs = {0: ya_ref[...], 2: yc_ref[...]}
    weights = (wa_ref, wb_ref, wc_ref)
    width = 2 * LANES
    units = [(j, c) for c in range(D_MODEL // width) for j in range(N_BRANCHES)]

    def gate_pre(unit):
        j, c = unit
        lo_col = j * D_MODEL + c * width
        return jnp.dot(hb, wg_ref[0, :, lo_col:lo_col + width], preferred_element_type=F32)

    pre = gate_pre(units[0])
    chunks = [None] * (D_MODEL // width)
    for i, (j, c) in enumerate(units):
        nxt = gate_pre(units[i + 1]) if i + 1 < len(units) else None
        lo_col = j * D_MODEL + c * width
        gate = _sigmoid(pre + bg_ref[:, lo_col:lo_col + width])
        if j not in inputs:
            inputs[j] = attention_branch()
        term = gate * jnp.dot(inputs[j], weights[j][0, :, c * width:(c + 1) * width],
                              preferred_element_type=F32)
        chunks[c] = term if chunks[c] is None else chunks[c] + term
        pre = nxt
    merged = jnp.concatenate([t.astype(BF16) for t in chunks], axis=-1)
    xm = x + jnp.dot(merged, wo_ref[0], preferred_element_type=F32)
    xm_ref[...] = xm

    h2 = _rmsnorm(xm, g2_ref[...])
    h2_ref[...] = h2.astype(BF16)

    logits = _dot3(wr_ref[...], h2, NT) + br_ref[:, 0:1]
    gl = [logits[g:g + 1, :] for g in range(N_GROUPS)]
    gmax = functools.reduce(jnp.maximum, gl)
    gsum = functools.reduce(lambda s, t: s + t, [jnp.exp(t - gmax) for t in gl])
    gp = 1.0 / gsum
    taken = jnp.zeros_like(gmax, dtype=jnp.bool_)
    sel = []
    for g in range(N_GROUPS):
        pick = (gl[g] == gmax) & jnp.logical_not(taken)
        sel.append(pick)
        taken = taken | pick
    el = []
    for e in range(EXPERTS_PER_GROUP):
        acc = jnp.zeros_like(gmax)
        for g in range(N_GROUPS):
            r0 = N_GROUPS + g * EXPERTS_PER_GROUP + e
            acc = jnp.where(sel[g], logits[r0:r0 + 1, :], acc)
        el.append(acc)
    emax = functools.reduce(jnp.maximum, el)
    ex = [jnp.exp(t - emax) for t in el]
    esum = functools.reduce(lambda s, t: s + t, ex)
    p = [t / esum for t in ex]
    p1 = functools.reduce(jnp.maximum, p)
    taken = jnp.zeros_like(gmax, dtype=jnp.bool_)
    first = []
    for e in range(EXPERTS_PER_GROUP):
        pick = (p[e] == p1) & jnp.logical_not(taken)
        first.append(pick)
        taken = taken | pick
    rest = [jnp.where(first[e], -1.0, p[e]) for e in range(EXPERTS_PER_GROUP)]
    p2 = functools.reduce(jnp.maximum, rest)
    taken = jnp.zeros_like(gmax, dtype=jnp.bool_)
    second = []
    for e in range(EXPERTS_PER_GROUP):
        pick = (rest[e] == p2) & jnp.logical_not(taken)
        second.append(pick)
        taken = taken | pick
    denom = p1 + p2
    for e in range(EXPERTS_PER_GROUP):
        in_group = jnp.where(first[e], p1 / denom, jnp.where(second[e], p2 / denom, 0.0))
        for g in range(N_GROUPS):
            r0 = g * EXPERTS_PER_GROUP + e
            comb_ref[r0:r0 + 1, :] = jnp.where(sel[g], gp * in_group, 0.0)
    gid = functools.reduce(lambda s, t: s + t,
                           [jnp.where(sel[g], float(g), 0.0) for g in range(N_GROUPS)])
    comb_ref[N_EXPERTS:N_EXPERTS + 1, :] = gid
    comb_ref[N_EXPERTS + 1:, :] = jnp.zeros((ROUTE_ROWS - N_EXPERTS - 1, gid.shape[1]), F32)


def _merge(x2, g, ya, attn, yc, layer, wg, bg, wa, wb, wc, wo, g2, wr, br, seq):
    n = x2.shape[0]
    tm = MERGE_ROW_TILE
    row = lambda width: pl.BlockSpec((tm, width), lambda i: (i, 0))
    of_layer = lambda rows, cols: pl.BlockSpec((1, rows, cols), lambda i: (layer, 0, 0))
    attn_flat = [t for pair in attn for t in pair]
    attn_specs = [_residue_spec(tm, t.shape[1], B_OUT, seq // tm) for t in attn_flat]
    return pl.pallas_call(
        _merge_kernel,
        grid=(n // tm,),
        in_specs=[row(D_MODEL), _full((1, D_MODEL)), row(A_WIDTH)] + attn_specs
        + [row(C_WIDTH), of_layer(D_MODEL, 3 * D_MODEL), _full((1, 3 * D_MODEL)),
           of_layer(A_WIDTH, D_MODEL), of_layer(B_OUT, D_MODEL), of_layer(C_WIDTH, D_MODEL),
           of_layer(D_MODEL, D_MODEL), _full((1, D_MODEL)), _full((32, D_MODEL)), _full((32, LANES))],
        out_specs=[row(D_MODEL), row(D_MODEL), pl.BlockSpec((ROUTE_ROWS, tm), lambda i: (0, i))],
        out_shape=[jax.ShapeDtypeStruct((n, D_MODEL), F32),
                   jax.ShapeDtypeStruct((n, D_MODEL), BF16),
                   jax.ShapeDtypeStruct((ROUTE_ROWS, n), F32)],
        scratch_shapes=[pltpu.VMEM((4, B_OUT // LANES, tm, LANES), F32)],
        compiler_params=_params("parallel"),
        name="merge_router",
    )(x2, g, ya, *attn_flat, yc, wg, bg, wa, wb, wc, wo, g2, wr, br)


def _group_ranks(gid):
    t = gid.shape[1]
    member = gid == lax.broadcasted_iota(jnp.int32, (8, t), 0).astype(F32)
    before = (lax.broadcasted_iota(jnp.int32, (t, t), 0)
              < lax.broadcasted_iota(jnp.int32, (t, t), 1)).astype(BF16)
    ranks = jnp.dot(member.astype(BF16), before, preferred_element_type=F32)
    return jnp.sum(jnp.where(member, ranks, 0.0), axis=0, keepdims=True)


def _group_onehots(gid, rank, cap):
    slot = lax.broadcasted_iota(jnp.int32, (cap, gid.shape[1]), 0).astype(F32)
    return [jnp.where(jnp.where(gid == float(g), rank, -1.0) == slot, 1.0, 0.0).astype(BF16)
            for g in range(N_GROUPS)]


def _expert_column(cw3, e):
    lane = lax.broadcasted_iota(jnp.int32, cw3.shape, 1)
    mine = (lane % N_EXPERTS == e) & (lane < 3 * N_EXPERTS)
    return jnp.sum(jnp.where(mine, cw3, 0.0), axis=-1, keepdims=True)


def _ffn(h, cw, wg, wu, wd):
    gate = jnp.dot(h, wg, preferred_element_type=F32)
    up = jnp.dot(h, wu, preferred_element_type=F32)
    hid = gate * _sigmoid(gate) * up * cw
    return jnp.dot(hid.astype(BF16), wd, preferred_element_type=F32)


def _moe_compact_kernel(h2_ref, route_ref, comb3_ref, xc_ref, cw_ref, rank_ref):
    gid = route_ref[N_EXPERTS:N_EXPERTS + 1, :]
    rank = _group_ranks(gid)
    rank_ref[0] = jnp.broadcast_to(rank, rank_ref.shape[1:])
    for g, p in enumerate(_group_onehots(gid, rank, xc_ref.shape[1])):
        xc_ref[g] = jnp.dot(p, h2_ref[...], preferred_element_type=F32).astype(BF16)
        cw_ref[g] = jnp.dot(p, comb3_ref[...], preferred_element_type=F32)


def _moe_expert_kernel(xc_ref, cw_ref, wg_ref, wu_ref, wd_ref, y_ref, hid_ref, wd_all_ref):
    j = pl.program_id(2)
    e = pl.program_id(0) * EXPERTS_PER_GROUP + j
    h = xc_ref[0]
    gate = jnp.dot(h, wg_ref[0, 0].astype(BF16), preferred_element_type=F32)
    up = jnp.dot(h, wu_ref[0, 0].astype(BF16), preferred_element_type=F32)
    hid_ref[j] = (gate * _sigmoid(gate) * up * _expert_column(cw_ref[0], e)).astype(BF16)
    wd_all_ref[j] = wd_ref[0, 0].astype(BF16)

    @pl.when(j == EXPERTS_PER_GROUP - 1)
    def _():
        hid = jnp.concatenate([hid_ref[k] for k in range(EXPERTS_PER_GROUP)], axis=-1)
        wd = wd_all_ref[...].reshape(EXPERTS_PER_GROUP * EXPERT_FF, D_MODEL)
        y_ref[0] = jnp.dot(hid, wd, preferred_element_type=F32).astype(BF16)


def _moe_scatter_kernel(xm_ref, route_ref, rank_ref, y_ref, gf_ref, out_ref, *, final_norm):
    cap = y_ref.shape[1]
    gid = route_ref[N_EXPERTS:N_EXPERTS + 1, :]
    p = jnp.concatenate(_group_onehots(gid, rank_ref[0, 0:1, :], cap), axis=0)
    y = y_ref[...].reshape(N_GROUPS * cap, D_MODEL)
    res = xm_ref[...] + lax.dot_general(p, y, TN, preferred_element_type=F32)
    out_ref[...] = _rmsnorm(res, gf_ref[...]) if final_norm else res


def _moe_dense_kernel(ids_ref, cnt_ref, xm_ref, h2_ref, comb3_ref, wg_ref, wu_ref, wd_ref, gf_ref,
                      prev_ref, out_ref, *, final_norm):
    del ids_ref, prev_ref
    e = pl.program_id(1)

    @pl.when(pl.program_id(0) < cnt_ref[0])
    def _():
        @pl.when(e == 0)
        def _():
            out_ref[...] = xm_ref[...]

        wg, wu, wd = (r[0, 0].astype(BF16) for r in (wg_ref, wu_ref, wd_ref))
        cw = _expert_column(comb3_ref[...].astype(F32), e)
        out_ref[...] += _ffn(h2_ref[...], cw, wg, wu, wd)

        if final_norm:
            @pl.when(e == N_EXPERTS - 1)
            def _():
                out_ref[...] = _rmsnorm(out_ref[...], gf_ref[...])


def _combine_pieces(route):
    n = route.shape[1]
    pieces, rest = [], route[:N_EXPERTS].T
    for _ in range(3):
        pieces.append(rest.astype(BF16))
        rest = rest - pieces[-1].astype(F32)
    return jnp.concatenate(pieces + [jnp.zeros((n, LANES - 3 * N_EXPERTS), BF16)], axis=1)


def _overflow_flags(route):
    gid = route[N_EXPERTS].astype(jnp.int32).reshape(-1, MOE_ROW_TILE)
    counts = jnp.sum(gid[:, :, None] == jnp.arange(N_GROUPS)[None, None, :], axis=1)
    return (jnp.max(counts, axis=1) > MOE_CAP).astype(jnp.int32)


def _moe_compacted(xm, h2, route, layer, wg, wu, wd, g_final, final_norm):
    n = xm.shape[0]
    tm, cap = MOE_ROW_TILE, MOE_CAP
    nt = n // tm
    comb3 = _combine_pieces(route)
    overflow = _overflow_flags(route)

    tile = lambda width: pl.BlockSpec((tm, width), lambda i: (i, 0))
    route_spec = pl.BlockSpec((ROUTE_ROWS, tm), lambda i: (0, i))
    slots = lambda width: pl.BlockSpec((N_GROUPS, cap, width), lambda i: (0, i, 0))
    rank_spec = pl.BlockSpec((1, 8, tm), lambda i: (i, 0, 0))
    xc, cw, rank = pl.pallas_call(
        _moe_compact_kernel,
        grid=(nt,),
        in_specs=[tile(D_MODEL), route_spec, tile(LANES)],
        out_specs=[slots(D_MODEL), slots(LANES), rank_spec],
        out_shape=[jax.ShapeDtypeStruct((N_GROUPS, nt * cap, D_MODEL), BF16),
                   jax.ShapeDtypeStruct((N_GROUPS, nt * cap, LANES), F32),
                   jax.ShapeDtypeStruct((nt, 8, tm), F32)],
        compiler_params=_params("parallel"),
        name="moe_compact",
    )(h2, route, comb3)

    halves = 4
    blk = nt * cap // halves
    rows_spec = lambda width: pl.BlockSpec((1, blk, width), lambda g, r, j: (g, r, 0))
    expert = lambda rows, cols: pl.BlockSpec(
        (1, 1, rows, cols), lambda g, r, j: (layer, g * EXPERTS_PER_GROUP + j, 0, 0))
    y = pl.pallas_call(
        _moe_expert_kernel,
        grid=(N_GROUPS, halves, EXPERTS_PER_GROUP),
        in_specs=[rows_spec(D_MODEL), rows_spec(LANES), expert(D_MODEL, EXPERT_FF),
                  expert(D_MODEL, EXPERT_FF), expert(EXPERT_FF, D_MODEL)],
        out_specs=rows_spec(D_MODEL),
        out_shape=jax.ShapeDtypeStruct((N_GROUPS, nt * cap, D_MODEL), BF16),
        scratch_shapes=[pltpu.VMEM((EXPERTS_PER_GROUP, blk, EXPERT_FF), BF16),
                        pltpu.VMEM((EXPERTS_PER_GROUP, EXPERT_FF, D_MODEL), BF16)],
        compiler_params=_params("parallel", "parallel", "arbitrary"),
        name="moe_experts",
    )(xc, cw, wg, wu, wd)

    out = pl.pallas_call(
        functools.partial(_moe_scatter_kernel, final_norm=final_norm),
        grid=(nt,),
        in_specs=[tile(D_MODEL), route_spec, rank_spec, slots(D_MODEL), _full((1, D_MODEL))],
        out_specs=tile(D_MODEL),
        out_shape=jax.ShapeDtypeStruct((n, D_MODEL), F32),
        compiler_params=_params("parallel"),
        name="moe_scatter",
    )(xm, route, rank, y, g_final)

    return out, comb3, overflow


def _moe_dense(ids, count, prev, xm, h2, comb3, layer, wg, wu, wd, g_final, final_norm):
    n = xm.shape[0]
    tm = MOE_ROW_TILE
    tile = lambda width: pl.BlockSpec((tm, width), lambda s, e, ids, cnt: (ids[s], 0))
    expert = lambda rows, cols: pl.BlockSpec(
        (1, 1, rows, cols), lambda s, e, ids, cnt: (layer, jnp.where(s < cnt[0], e, 0), 0, 0))
    grid_spec = pltpu.PrefetchScalarGridSpec(
        num_scalar_prefetch=2,
        grid=(ids.shape[0], N_EXPERTS),
        in_specs=[tile(D_MODEL), tile(D_MODEL), tile(LANES),
                  expert(D_MODEL, EXPERT_FF), expert(D_MODEL, EXPERT_FF), expert(EXPERT_FF, D_MODEL),
                  pl.BlockSpec((1, D_MODEL), lambda s, e, ids, cnt: (0, 0)),
                  pl.BlockSpec(memory_space=pl.ANY)],
        out_specs=tile(D_MODEL))
    return pl.pallas_call(
        functools.partial(_moe_dense_kernel, final_norm=final_norm),
        grid_spec=grid_spec,
        out_shape=jax.ShapeDtypeStruct((n, D_MODEL), F32),
        input_output_aliases={9: 0},
        compiler_params=_params("arbitrary", "arbitrary"),
        name="moe_dense",
    )(ids, count, xm, h2, comb3, wg, wu, wd, g_final, prev)


def _moe(xm, h2, route, layer, wg, wu, wd, g_final, final_norm):
    nt = xm.shape[0] // MOE_ROW_TILE
    dense_args = (layer, wg, wu, wd, g_final, final_norm)

    def compacted(_):
        return _moe_compacted(xm, h2, route, layer, wg, wu, wd, g_final, final_norm)

    def with_fix(_):
        out, comb3, overflow = compacted(None)
        order = jnp.argsort(-overflow, stable=True).astype(jnp.int32)[:MOE_FIX_TILES]
        count = jnp.sum(overflow).astype(jnp.int32)
        ids = jnp.where(jnp.arange(MOE_FIX_TILES) < count, order, order[jnp.maximum(count - 1, 0)])
        return _moe_dense(ids, count[None], out, xm, h2, comb3, *dense_args)

    def all_dense(_):
        comb3 = _combine_pieces(route)
        return _moe_dense(jnp.arange(nt, dtype=jnp.int32), jnp.full((1,), nt, jnp.int32), xm, xm, h2,
                          comb3, *dense_args)

    n_over = jnp.sum(_overflow_flags(route))
    case = jnp.where(n_over == 0, 0, jnp.where(n_over <= MOE_FIX_TILES, 1, 2))
    return lax.switch(case, [lambda _: compacted(None)[0], with_fix, all_dense], None)


def _pack_lora(mu_wag, w1, a1, g1, mu_v, v1):
    d = D_MODEL
    if v1 is None:
        mu_v = jnp.zeros((d,), F32)
        v1 = jnp.zeros((d, V_LORA), F32)
    pad = jnp.zeros((d, LORA_HALF - (W_LORA + A_LORA + G_LORA + V_LORA)), F32)
    mus = (mu_wag[0], mu_wag[1], mu_wag[2], mu_v)
    ws = (w1, a1, g1, v1)
    cur = [w * (1.0 - m)[:, None] for w, m in zip(ws, mus)]
    prev = [w * m[:, None] for w, m in zip(ws, mus)]
    return jnp.concatenate(cur + [pad] + prev + [pad], axis=1)


def _pad_rows(w, lo, total=LANES):
    return jnp.zeros((total, w.shape[1]), F32).at[lo:lo + w.shape[0]].set(w)


def kernel(x, positions, norm_mix_g, w_in, gmlp_ln_g, gmlp_ln_b, gmlp_ws, gmlp_bs, rwkv_mu_rkv, rwkv_mu_wag, rwkv_w0, rwkv_w1, rwkv_w2, rwkv_a0, rwkv_a1, rwkv_a2, rwkv_g1, rwkv_g2, rwkv_k_k, rwkv_k_a, rwkv_r_k, rwkv_ln_w, rwkv_ln_b, rwkv_mu_v, rwkv_v0, rwkv_v1, rwkv_v2, w_branch_a, w_branch_b, w_branch_c, w_gate, b_gate, w_out, norm_ffn_g, router_group_w, router_group_b, router_expert_w, router_expert_b, expert_w_gate, expert_w_up, expert_w_down, final_norm_g):
    batch, seq, d = x.shape
    depth = w_in.shape[0]
    n = batch * seq
    x2 = x.reshape(n, d)
    rope = _rope_tables(positions)
    w_in_b, w_gate_b, w_out_b = w_in.astype(BF16), w_gate.astype(BF16), w_out.astype(BF16)
    wba, wbb, wbc = w_branch_a.astype(BF16), w_branch_b.astype(BF16), w_branch_c.astype(BF16)
    v_first = None
    for l in range(depth):
        vres = l > 0
        lora_w = _pack_lora(rwkv_mu_wag[l], rwkv_w1[l], rwkv_a1[l], rwkv_g1[l],
                            rwkv_mu_v[l - 1] if vres else None, rwkv_v1[l - 1] if vres else None)
        ya, qkv0, qkv1, qkv2, rkv, lora = _inproj(
            x2, norm_mix_g[l][None], w_in_b, l, lora_w.astype(BF16), gmlp_ln_g[l][None],
            gmlp_ln_b[l][None], gmlp_ws[l], gmlp_bs[l].T, rope, batch, seq)

        attn = [_dilated_attention(qkv, dil)
                for qkv, (_, dil) in zip((qkv0, qkv1, qkv2), DILATED_PATTERNS)]

        zero = jnp.zeros((C_WIDTH,), F32)
        vecs = jnp.stack([rwkv_w0[l], rwkv_a0[l], rwkv_v0[l - 1] if vres else zero, rwkv_k_k[l],
                          rwkv_k_a[l], rwkv_r_k[l].reshape(C_WIDTH), rwkv_ln_w[l], rwkv_ln_b[l]])
        yc, v_c = _rwkv(rkv, lora, v_first, rwkv_mu_rkv[l], vecs,
                        _pad_rows(rwkv_w2[l], 0), _pad_rows(rwkv_a2[l], W_LORA), rwkv_g2[l],
                        _pad_rows(rwkv_v2[l - 1], 0) if vres else None, batch, seq)
        if l == 0:
            v_first = v_c

        wr = jnp.concatenate([router_group_w[l], router_expert_w[l]], axis=1).T
        wr = jnp.zeros((32, d), F32).at[:wr.shape[0]].set(wr)
        br = jnp.concatenate([router_group_b[l], router_expert_b[l]])
        br = jnp.zeros((32, LANES), F32).at[:br.shape[0], 0].set(br)
        xm, h2, route = _merge(
            x2, norm_mix_g[l][None], ya, attn, yc, l, w_gate_b, b_gate[l][None], wba, wbb, wbc,
            w_out_b, norm_ffn_g[l][None], wr, br, seq)

        x2 = _moe(xm, h2, route, l, expert_w_gate, expert_w_up, expert_w_down, final_norm_g[None],
                  final_norm=(l == depth - 1))
    return x2.reshape(batch, seq, d)
```

```python
import functools

import numpy as np
import jax
import jax.numpy as jnp
from jax import lax
from jax.experimental import pallas as pl
from jax.experimental.pallas import tpu as pltpu

F32 = jnp.float32
BF16 = jnp.bfloat16

D_MODEL = 1024
HEAD_DIM = 64
A_GROUPS = 4
A_WIDTH = 512
CHUNK = 128
DILATED_PATTERNS = ((128, 1), (512, 4), (2048, 16))
B_HEADS_PER_GROUP = 4
B_WIDTH = 768
B_OUT = 256
BLOCK = 128
ROPE_THETA = 500000.0
ROPE_DIMS = 16
C_HEADS = 8
C_WIDTH = 512
W_LORA, A_LORA, V_LORA, G_LORA = 64, 64, 32, 128
RWKV_GN_EPS = HEAD_DIM * 1e-5
IN_COLS = 2 * A_WIDTH + 3 * B_WIDTH + 3 * C_WIDTH
LORA_HALF = 384
LORA_COLS = 2 * LORA_HALF
N_BRANCHES = 3
N_GROUPS = 4
EXPERTS_PER_GROUP = 4
N_EXPERTS = 16
EXPERT_FF = 512
RMS_EPS = 1e-6
LN_EPS = 1e-5

LANES = 128
ROW_TILE = 512
MERGE_ROW_TILE = 1024
ATTN_BLOCKS_TOGETHER = 4
RWKV_CHUNK = 64
RWKV_BATCH_ROWS = 8
RWKV_SUB_ROWS = 2
MOE_ROW_TILE = 1024
MOE_CAP = 352
MOE_FIX_TILES = 2
ROUTE_ROWS = 24
VMEM_LIMIT = 58 * 1024 * 1024
NEG_BIG = -1e30


def _params(*sem):
    return pltpu.CompilerParams(dimension_semantics=sem, vmem_limit_bytes=VMEM_LIMIT)


def _full(shape):
    nd = len(shape)
    return pl.BlockSpec(shape, lambda *_: (0,) * nd)


def _gelu_tanh(x):
    return 0.5 * x * (1.0 + jnp.tanh(0.7978845608028654 * (x + 0.044715 * (x * x * x))))


def _sigmoid(x):
    return 0.5 * jnp.tanh(0.5 * x) + 0.5


def _rmsnorm(x, g):
    return x * lax.rsqrt(jnp.mean(x * x, axis=-1, keepdims=True) + RMS_EPS) * g


def _rope_table_kernel(pos_ref, consts_ref, c_ref, sa_ref, sb_ref):
    ang = pos_ref[...].astype(F32) * consts_ref[0:1, :]
    c_ref[...] = jnp.cos(ang)
    s = jnp.sin(ang)
    sa_ref[...] = -s * consts_ref[1:2, :]
    sb_ref[...] = s * consts_ref[2:3, :]


def _rope_tables(positions):
    n = positions.size
    half = ROPE_DIMS // 2
    inv_freq = ROPE_THETA ** (-jnp.arange(half, dtype=F32) / half)
    lane = np.arange(LANES) % HEAD_DIM
    in_rope = lane < ROPE_DIMS
    invf_row = jnp.where(in_rope, inv_freq[lane % half], 0.0)
    consts = jnp.zeros((8, LANES), F32)
    consts = consts.at[0].set(invf_row)
    consts = consts.at[1].set(jnp.asarray(lane < half, F32))
    consts = consts.at[2].set(jnp.asarray((lane >= half) & in_rope, F32))
    tm = 1024
    out = jax.ShapeDtypeStruct((n, LANES), F32)
    return pl.pallas_call(
        _rope_table_kernel,
        grid=(n // tm,),
        in_specs=[pl.BlockSpec((tm, 1), lambda i: (i, 0)), _full((8, LANES))],
        out_specs=[pl.BlockSpec((tm, LANES), lambda i: (i, 0))] * 3,
        out_shape=[out, out, out],
        compiler_params=_params("parallel"),
        name="rope_tables",
    )(positions.reshape(n, 1), consts)


def _inproj_kernel(x_ref, g_ref, w_ref, wl_ref, lng_ref, lnb_ref, ws_ref, bst_ref, c_ref, sa_ref,
                   sb_ref, ya_ref, qkv0_ref, qkv1_ref, qkv2_ref, rkv_ref, lora_ref, nat_ref):
    tm = x_ref.shape[0]
    hb = _rmsnorm(x_ref[...], g_ref[...]).astype(BF16)

    def proj(lo, width):
        return jnp.dot(hb, w_ref[0, :, lo:lo + width], preferred_element_type=F32)

    q_off = 2 * A_WIDTH
    c_off = q_off + 3 * B_WIDTH
    uv = proj(0, 2 * A_WIDTH)
    rkv_ref[:, 0:C_WIDTH] = proj(c_off, C_WIDTH)
    u = _gelu_tanh(uv[:, :A_WIDTH])
    rkv_ref[:, C_WIDTH:2 * C_WIDTH] = proj(c_off + C_WIDTH, C_WIDTH)
    v = _gelu_tanh(uv[:, A_WIDTH:])
    rkv_ref[:, 2 * C_WIDTH:] = proj(c_off + 2 * C_WIDTH, C_WIDTH)
    mu = jnp.mean(v, axis=-1, keepdims=True)
    d = v - mu
    var = jnp.mean(d * d, axis=-1, keepdims=True)
    lora_ref[...] = jnp.dot(hb, wl_ref[...], preferred_element_type=F32)
    vn = (d * lax.rsqrt(var + LN_EPS) * lng_ref[...] + lnb_ref[...]).astype(BF16)
    row = lax.broadcasted_iota(jnp.int32, (CHUNK, CHUNK), 0)
    col = lax.broadcasted_iota(jnp.int32, (CHUNK, CHUNK), 1)
    for g in range(A_GROUPS):
        wg = jnp.where(row >= col, ws_ref[g], 0.0).astype(BF16)
        bias = bst_ref[:, g:g + 1]
        cs = slice(g * LANES, (g + 1) * LANES)
        for c in range(tm // CHUNK):
            rs = slice(c * CHUNK, (c + 1) * CHUNK)
            mixed = jnp.dot(wg, vn[rs, cs], preferred_element_type=F32) + bias
            ya_ref[rs, cs] = (u[rs, cs] * mixed).astype(BF16)

    cos, sa, sb = c_ref[...], sa_ref[...], sb_ref[...]
    for grp, (qkv_ref, (_, dil)) in enumerate(zip((qkv0_ref, qkv1_ref, qkv2_ref), DILATED_PATTERNS)):
        acc = [proj(q_off + part * B_WIDTH + grp * B_OUT, B_OUT) for part in range(3)]
        for c in range(B_WIDTH // LANES):
            t = acc[c * LANES // B_OUT][:, c * LANES % B_OUT:c * LANES % B_OUT + LANES]
            if c < 2 * B_OUT // LANES:
                t = t * cos + pltpu.roll(t, LANES - 8, 1) * sa + pltpu.roll(t, 8, 1) * sb
            if c < B_OUT // LANES:
                t = t * (HEAD_DIM ** -0.5)
            cs = slice(c * LANES, (c + 1) * LANES)
            if dil == 1:
                qkv_ref[0, 0, :, cs] = t.astype(BF16)
            else:
                nat_ref[c] = t
                for r in range(dil):
                    qkv_ref[0, r, :, cs] = nat_ref[c, pl.ds(r, tm // dil, stride=dil), :].astype(BF16)


def _residue_spec(tm, dil, width, tiles_per_seq):
    return pl.BlockSpec((1, dil, tm // dil, width),
                        lambda i: (i // tiles_per_seq, 0, i % tiles_per_seq, 0))


def _inproj(x2, g, w_in, layer, w_lora, ln_g, ln_b, ws, bs_t, rope, batch, seq):
    n = x2.shape[0]
    tm = ROW_TILE
    row = lambda width: pl.BlockSpec((tm, width), lambda i: (i, 0))
    dils = [d for _, d in DILATED_PATTERNS]
    return pl.pallas_call(
        _inproj_kernel,
        grid=(n // tm,),
        in_specs=[row(D_MODEL), _full((1, D_MODEL)),
                  pl.BlockSpec((1, D_MODEL, IN_COLS), lambda i: (layer, 0, 0)),
                  _full((D_MODEL, LORA_COLS)),
                  _full((1, A_WIDTH)), _full((1, A_WIDTH)), _full((A_GROUPS, CHUNK, CHUNK)),
                  _full((CHUNK, A_GROUPS)), row(LANES), row(LANES), row(LANES)],
        out_specs=[row(A_WIDTH)] + [_residue_spec(tm, d, B_WIDTH, seq // tm) for d in dils]
        + [row(3 * C_WIDTH), row(LORA_COLS)],
        out_shape=[jax.ShapeDtypeStruct((n, A_WIDTH), BF16)]
        + [jax.ShapeDtypeStruct((batch, d, seq // d, B_WIDTH), BF16) for d in dils]
        + [jax.ShapeDtypeStruct((n, 3 * C_WIDTH), F32),
           jax.ShapeDtypeStruct((n, LORA_COLS), F32)],
        scratch_shapes=[pltpu.VMEM((B_WIDTH // LANES, tm, LANES), F32)],
        compiler_params=_params("parallel"),
        name="inproj",
    )(x2, g, w_in, w_lora, ln_g, ln_b, ws, bs_t, *rope)


def _attn_blocks(blocks):
    lo = lax.broadcasted_iota(jnp.int32, (BLOCK, LANES), 1) < HEAD_DIM
    pairs = range(B_OUT // LANES)
    heads = range(B_HEADS_PER_GROUP)
    scores = []
    for q, kw, _, valid in blocks:
        for p in pairs:
            ps = slice(p * LANES, (p + 1) * LANES)
            qp = q[:, ps]
            zero = jnp.zeros_like(qp)
            for part in (jnp.where(lo, qp, zero), jnp.where(lo, zero, qp)):
                s = lax.dot_general(part, kw[:, ps], NT, preferred_element_type=F32)
                scores.append(jnp.where(valid, s, NEG_BIG))
    m = [jnp.max(s, axis=-1, keepdims=True) for s in scores]
    e = [jnp.exp(s - mx) for s, mx in zip(scores, m)]
    l = [jnp.sum(t, axis=-1, keepdims=True) for t in e]
    pv = [jnp.dot(e[len(heads) * b + h].astype(BF16),
                  blocks[b][2][:, (h // 2) * LANES:(h // 2 + 1) * LANES],
                  preferred_element_type=F32)
          for b in range(len(blocks)) for h in heads]
    lse = [mx + jnp.log(t) for mx, t in zip(m, l)]
    results = []
    for b in range(len(blocks)):
        at = lambda p, h: len(heads) * b + 2 * p + h
        out = [jnp.where(lo, pv[at(p, 0)] / l[at(p, 0)], pv[at(p, 1)] / l[at(p, 1)]) for p in pairs]
        ls = [jnp.where(lo, lse[at(p, 0)], lse[at(p, 1)]) for p in pairs]
        results.append((jnp.concatenate(out, axis=-1), jnp.concatenate(ls, axis=-1)))
    return results


def _attn_kernel(qkv_ref, o_ref, l_ref):
    dil, sub = qkv_ref.shape[1], qkv_ref.shape[2]
    nb = sub // BLOCK
    qs, ks, vs = (slice(i * B_OUT, (i + 1) * B_OUT) for i in range(3))
    qi = lax.broadcasted_iota(jnp.int32, (BLOCK, BLOCK), 0)
    kj = lax.broadcasted_iota(jnp.int32, (BLOCK, BLOCK), 1)
    qi2 = lax.broadcasted_iota(jnp.int32, (BLOCK, 2 * BLOCK), 0)
    kj2 = lax.broadcasted_iota(jnp.int32, (BLOCK, 2 * BLOCK), 1)
    valid2 = (kj2 >= qi2) & (kj2 <= qi2 + BLOCK)

    def first_block(r):
        rows = pl.ds(0, BLOCK)
        return rows, (qkv_ref[0, r, rows, qs], qkv_ref[0, r, rows, ks], qkv_ref[0, r, rows, vs],
                      kj <= qi)

    def later_block(r, n):
        rows = pl.ds(pl.multiple_of(n * BLOCK, BLOCK), BLOCK)
        win = pl.ds(pl.multiple_of((n - 1) * BLOCK, BLOCK), 2 * BLOCK)
        return rows, (qkv_ref[0, r, rows, qs], qkv_ref[0, r, win, ks], qkv_ref[0, r, win, vs], valid2)

    def run(work):
        for (o, l), (r, rows, _) in zip(_attn_blocks([w[2] for w in work]), work):
            o_ref[0, r, rows, :] = o
            l_ref[0, r, rows, :] = l

    together = ATTN_BLOCKS_TOGETHER
    if nb == 1:
        def body(i, carry):
            run([(r,) + first_block(r) for r in (together * i + t for t in range(together))])
            return carry
        lax.fori_loop(0, dil // together, body, 0)
    else:
        for r in range(dil):
            run([(r,) + first_block(r)] + [(r,) + later_block(r, n) for n in range(1, together)])

            def body(i, carry):
                run([(r,) + later_block(r, together * i + t) for t in range(together)])
                return carry
            lax.fori_loop(1, nb // together, body, 0)


def _dilated_attention(qkv, dilation):
    batch, _, sub, _ = qkv.shape
    ospec = pl.BlockSpec((1, dilation, sub, B_OUT), lambda b: (b, 0, 0, 0))
    oshape = jax.ShapeDtypeStruct((batch, dilation, sub, B_OUT), F32)
    return pl.pallas_call(
        _attn_kernel,
        grid=(batch,),
        in_specs=[pl.BlockSpec((1, dilation, sub, B_WIDTH), lambda b: (b, 0, 0, 0))],
        out_specs=[ospec, ospec],
        out_shape=[oshape, oshape],
        compiler_params=_params("parallel"),
        name=f"dilated_attn_d{dilation}",
    )(qkv)


def _shift_rows(x, prev_row):
    rolled = pltpu.roll(x, 1, 0)
    first = lax.broadcasted_iota(jnp.int32, x.shape, 0) == 0
    return jnp.where(first, prev_row, rolled)


NN = (((1,), (0,)), ((), ()))
NT = (((1,), (1,)), ((), ()))
TN = (((0,), (0,)), ((), ()))


def _dot(a, b, dims=NN):
    return lax.dot_general(a.astype(BF16), b.astype(BF16), dims, preferred_element_type=F32)


def _split(a, pieces):
    out = []
    for _ in range(pieces):
        p = a.astype(BF16)
        out.append(p)
        a = a - p.astype(F32)
    return out


def _dot3(a, b, dims=NN):
    (ah, al), (bh, bl) = _split(a, 2), _split(b, 2)
    return _dot(ah, bh, dims) + _dot(ah, bl, dims) + _dot(al, bh, dims)


def _head_sum(x):
    lo = lax.broadcasted_iota(jnp.int32, (x.shape[0], LANES), 1) < HEAD_DIM
    outs = []
    for c in range(x.shape[1] // LANES):
        xc = x[:, c * LANES:(c + 1) * LANES]
        s_lo = jnp.sum(jnp.where(lo, xc, 0.0), axis=-1, keepdims=True)
        s_hi = jnp.sum(jnp.where(lo, 0.0, xc), axis=-1, keepdims=True)
        outs.append(jnp.where(lo, s_lo, s_hi))
    return jnp.concatenate(outs, axis=-1)


def _finish(gen):
    try:
        while True:
            next(gen)
    except StopIteration as stop:
        return stop.value


def _interleave(first, second):
    values, gens = [None, None], [first, second]
    live = [True, True]
    while any(live):
        for i, gen in enumerate(gens):
            if live[i]:
                try:
                    next(gen)
                except StopIteration as stop:
                    values[i], live[i] = stop.value, False
    return values


def _rwkv_kernel(*refs, has_vres):
    if has_vres:
        (rkv_ref, lora_ref, vf_ref, mu_ref, vec_ref, w2_ref, a2_ref, g2_ref, v2_ref,
         yc_ref, vout_ref, state_ref, prkv_ref, plora_ref, y_ref) = refs
    else:
        (rkv_ref, lora_ref, mu_ref, vec_ref, w2_ref, a2_ref, g2_ref,
         yc_ref, vout_ref, state_ref, prkv_ref, plora_ref, y_ref) = refs
    nbat, ch = rkv_ref.shape[0], rkv_ref.shape[1]
    sub = RWKV_SUB_ROWS
    rows = sub * ch
    hd = HEAD_DIM
    npair = C_HEADS // 2
    row_of = [slice(j * ch, (j + 1) * ch) for j in range(sub)]

    @pl.when(pl.program_id(1) == 0)
    def _():
        state_ref[...] = jnp.zeros_like(state_ref)
        prkv_ref[...] = jnp.zeros_like(prkv_ref)
        plora_ref[...] = jnp.zeros_like(plora_ref)

    w0, a0, v0 = vec_ref[0:1, :], vec_ref[1:2, :], vec_ref[2:3, :]
    k_k, k_a, r_k = vec_ref[3:4, :], vec_ref[4:5, :], vec_ref[5:6, :]
    ln_w, ln_b = vec_ref[6:7, :], vec_ref[7:8, :]

    ti = lax.broadcasted_iota(jnp.int32, (rows, rows), 0)
    si = lax.broadcasted_iota(jnp.int32, (rows, rows), 1)
    tri = ((si <= ti) & (si // ch == ti // ch)).astype(BF16)
    unit = (lax.broadcasted_iota(jnp.int32, (8, LANES), 0) == 0).astype(BF16)
    lane = lax.broadcasted_iota(jnp.int32, (ch, LANES), 1)
    lo = lane < hd
    lo2 = lax.broadcasted_iota(jnp.int32, (2 * ch, LANES), 1) < hd
    row4 = lax.broadcasted_iota(jnp.int32, (4 * ch, LANES), 0)
    col4 = lax.broadcasted_iota(jnp.int32, (4 * ch, LANES), 1) % ch
    t4 = row4 % ch
    tri_mask = (col4 < t4) | ((col4 == t4) & ((row4 // ch) % 2 == 1))
    eye_right = (lane - hd == lax.broadcasted_iota(jnp.int32, (ch, LANES), 0)).astype(F32)
    brow = lax.broadcasted_iota(jnp.int32, (LANES, LANES), 0) // hd
    bcol = lax.broadcasted_iota(jnp.int32, (LANES, LANES), 1) // hd
    block_diag = brow == bcol
    zeros = jnp.zeros((ch, LANES), F32)

    def prepare(grp):
        bs = slice(grp * sub, (grp + 1) * sub)

        def shifted(x, prev_ref):
            parts = []
            for j, rs in enumerate(row_of):
                slot = 8 * (grp * sub + j)
                parts.append(_shift_rows(x[rs], prev_ref[slot:slot + 1, :]))
                prev_ref[slot:slot + 1, :] = x[(j + 1) * ch - 1:(j + 1) * ch, :]
            return jnp.concatenate(parts, axis=0)

        rkv = rkv_ref[bs].reshape(rows, 3 * C_WIDTH)
        rkv_prev = shifted(rkv, prkv_ref)
        lora_all = lora_ref[bs].reshape(rows, LORA_COLS)
        lora = lora_all[:, :LORA_HALF] + shifted(lora_all[:, LORA_HALF:], plora_ref)
        yield

        def mix(i):
            cur = rkv[:, i * C_WIDTH:(i + 1) * C_WIDTH]
            prev = rkv_prev[:, i * C_WIDTH:(i + 1) * C_WIDTH]
            return cur + (prev - cur) * mu_ref[i:i + 1, :]

        r, k, v = mix(0), mix(1), mix(2)
        yield
        wa = lora[:, 0:LANES]
        z = w0 + _dot3(jnp.tanh(wa), w2_ref[...])
        log_decay = -float(np.exp(-0.5)) * _sigmoid(z)
        yield
        a = _sigmoid(a0 + _dot(wa, a2_ref[...]))
        gate = _dot(_sigmoid(lora[:, LANES:2 * LANES]), g2_ref[...])
        if has_vres:
            vmix = _sigmoid(v0 + _dot(lora[:, 2 * LANES:3 * LANES], v2_ref[...]))
            v = v + (vf_ref[bs].reshape(rows, C_WIDTH) - v) * vmix
        vout_ref[bs] = v.reshape(sub, ch, C_WIDTH)
        yield

        kk = k * k_k
        kk = kk * lax.rsqrt(jnp.maximum(_head_sum(kk * kk), 1e-24))
        yield
        k = k * (1.0 + (a - 1.0) * k_a)
        bonus = _head_sum(r * k * r_k) * v
        b = kk * a
        yield

        cum = functools.reduce(lambda s, t: s + t, [_dot(tri, p) for p in _split(log_decay, 3)])
        ends = [cum[(j + 1) * ch - 1:(j + 1) * ch, :] for j in range(sub)]
        cum_end = jnp.concatenate([jnp.broadcast_to(e, (ch, C_WIDTH)) for e in ends], axis=0)
        yield
        e_neg = jnp.exp(-cum)
        e_end = jnp.exp(cum_end - cum)
        kt = kk * jnp.exp(cum - log_decay)
        yield
        rt = r * jnp.exp(cum)
        bt = b * e_neg
        kq = k * e_neg
        yield
        kh = k * e_end
        bh = b * e_end
        w_end_t = [functools.reduce(lambda s, t: s + t,
                                    [_dot(jnp.broadcast_to(p, (8, C_WIDTH)), unit, TN)
                                     for p in _split(jnp.exp(e), 2)])
                   for e in ends]
        yield
        return dict(kt=kt, rt=rt, bt=bt, kq=kq, kh=kh, bh=bh, v=v, bonus=bonus, gate=gate,
                    w_end_t=w_end_t)

    def solve(grp, t):
        chains = [(j, p) for j in range(sub) for p in range(npair)]
        at = lambda x, c: x[row_of[c[0]], c[1] * LANES:(c[1] + 1) * LANES]
        idx = range(len(chains))
        slot_of = lambda c: (grp * sub + c[0]) * npair + c[1]
        s_bd = [state_ref[slot_of(c)] for c in chains]
        l2 = [jnp.concatenate([at(t['kt'], c), at(t['rt'], c)], axis=0) for c in chains]
        ss = [_dot(l2[i], s_bd[i]) for i in idx]
        yield
        g = [jnp.where(tri_mask,
                       _dot(jnp.concatenate([jnp.where(lo2, l2[i], 0.0),
                                             jnp.where(lo2, 0.0, l2[i])], axis=0),
                            jnp.concatenate([at(t['bt'], c), at(t['kq'], c)], axis=0), NT), 0.0)
             for i, c in zip(idx, chains)]
        yield
        xv = [_dot(jnp.concatenate([g[i][0:ch], g[i][2 * ch:3 * ch]], axis=0),
                   jnp.concatenate([zeros, at(t['v'], c)], axis=0))
              for i, c in zip(idx, chains)]
        yield
        heads = [(i, h) for i in idx for h in range(2)]
        n0 = [-g[i][2 * h * ch:(2 * h + 1) * ch] for i, h in heads]
        w = [_dot(n[:, :ch], jnp.where(lo, n, eye_right)) + jnp.where(lo, 0.0, eye_right)
             for n in n0]
        yield
        for _ in range(int(np.log2(ch)) - 1):
            w = [_dot(x[:, :ch], x) + jnp.where(lo, 0.0, x) for x in w]
            yield
        us = [_dot(w[2 * i + h],
                   jnp.concatenate([zeros, ss[i][0:ch] + xv[i][h * ch:(h + 1) * ch]], axis=0))
              for i, h in heads]
        u = [jnp.where(lo, us[2 * i], us[2 * i + 1]) for i in idx]
        yield
        yv = [_dot(jnp.concatenate([g[i][ch:2 * ch], g[i][3 * ch:4 * ch]], axis=0),
                   jnp.concatenate([-u[i], at(t['v'], c)], axis=0))
              for i, c in zip(idx, chains)]
        yield
        upd = [_dot(jnp.concatenate([at(t['kh'], c), at(t['bh'], c)], axis=0),
                    jnp.concatenate([at(t['v'], c), -u[i]], axis=0), TN)
               for i, c in zip(idx, chains)]
        yield
        ys = slice(grp * rows, (grp + 1) * rows)
        for i, c in zip(idx, chains):
            ps = slice(c[1] * LANES, (c[1] + 1) * LANES)
            y_ref[grp * rows + c[0] * ch:grp * rows + (c[0] + 1) * ch, ps] = (
                ss[i][ch:2 * ch] + jnp.where(lo, yv[i][0:ch], yv[i][ch:2 * ch]))
            state_ref[slot_of(c)] = (t['w_end_t'][c[0]][ps, :] * s_bd[i]
                                     + jnp.where(block_diag, upd[i], 0.0))
        yield
        y = y_ref[ys, :]
        d = y - _head_sum(y) * (1.0 / hd)
        var = _head_sum(d * d) * (1.0 / hd)
        yn = d * lax.rsqrt(var + RWKV_GN_EPS) * ln_w + ln_b
        yc_ref[grp * sub:(grp + 1) * sub] = (
            ((yn + t['bonus']) * t['gate']).astype(BF16).reshape(sub, ch, C_WIDTH))

    groups = nbat // sub
    prepared = _finish(prepare(0))
    for grp in range(groups):
        if grp + 1 < groups:
            _, prepared_next = _interleave(solve(grp, prepared), prepare(grp + 1))
            prepared = prepared_next
        else:
            _finish(solve(grp, prepared))


def _rwkv(rkv, lora, v_first, mu_rkv, vecs, w2, a2, g2, v2, batch, seq):
    ch = RWKV_CHUNK
    nbat = RWKV_BATCH_ROWS
    has_vres = v_first is not None
    per_seq = lambda t: t.reshape(batch, seq, t.shape[-1])
    blk = lambda width: pl.BlockSpec((nbat, ch, width), lambda i, c: (i, c, 0))
    ins = [per_seq(rkv), per_seq(lora)] + ([per_seq(v_first)] if has_vres else []) \
        + [mu_rkv, vecs, w2, a2, g2] + ([v2] if has_vres else [])
    specs = [blk(3 * C_WIDTH), blk(LORA_COLS)] + ([blk(C_WIDTH)] if has_vres else []) \
        + [_full((3, C_WIDTH)), _full((8, C_WIDTH)), _full((LANES, C_WIDTH)),
           _full((LANES, C_WIDTH)), _full((LANES, C_WIDTH))] \
        + ([_full((LANES, C_WIDTH))] if has_vres else [])
    yc, v_out = pl.pallas_call(
        functools.partial(_rwkv_kernel, has_vres=has_vres),
        grid=(batch // nbat, seq // ch),
        in_specs=specs,
        out_specs=[blk(C_WIDTH), blk(C_WIDTH)],
        out_shape=[jax.ShapeDtypeStruct((batch, seq, C_WIDTH), BF16),
                   jax.ShapeDtypeStruct((batch, seq, C_WIDTH), F32)],
        scratch_shapes=[pltpu.VMEM((nbat * C_HEADS // 2, LANES, LANES), F32),
                        pltpu.VMEM((8 * nbat, 3 * C_WIDTH), F32),
                        pltpu.VMEM((8 * nbat, LORA_HALF), F32),
                        pltpu.VMEM((nbat * ch, C_WIDTH), F32)],
        compiler_params=_params("parallel", "arbitrary"),
        name="rwkv7_chunked",
    )(*ins)
    return yc.reshape(batch * seq, C_WIDTH), v_out.reshape(batch * seq, C_WIDTH)


def _merge_kernel(x_ref, g_ref, ya_ref, o0_ref, l0_ref, o1_ref, l1_ref, o2_ref, l2_ref, yc_ref,
                  wg_ref, bg_ref, wa_ref, wb_ref, wc_ref, wo_ref, g2_ref, wr_ref, br_ref,
                  xm_ref, h2_ref, comb_ref, nat_ref):
    x = x_ref[...]
    tm = x.shape[0]
    hb = _rmsnorm(x, g_ref[...]).astype(BF16)

    def token_order(ref, slot):
        dil = ref.shape[1]
        if dil == 1:
            return ref[0, 0]
        chunks = []
        for c in range(B_OUT // LANES):
            for r in range(dil):
                nat_ref[slot, c, pl.ds(r, tm // dil, stride=dil), :] = \
                    ref[0, r, :, c * LANES:(c + 1) * LANES]
            chunks.append(nat_ref[slot, c])
        return jnp.concatenate(chunks, axis=-1)

    def attention_branch():
        l0, l1, l2 = token_order(l0_ref, 0), token_order(l1_ref, 0), token_order(l2_ref, 1)
        lm = jnp.maximum(jnp.maximum(l0, l1), l2)
        e0, e1, e2 = jnp.exp(l0 - lm), jnp.exp(l1 - lm), jnp.exp(l2 - lm)
        o0, o1, o2 = token_order(o0_ref, 0), token_order(o1_ref, 2), token_order(o2_ref, 3)
        return ((e0 * o0 + e1 * o1 + e2 * o2) / (e0 + e1 + e2)).astype(BF16)

    inputs = {0: ya_ref[...], 2: yc_ref[...]}
    weights = (wa_ref, wb_ref, wc_ref)
    width = 2 * LANES
    units = [(j, c) for c in range(D_MODEL // width) for j in range(N_BRANCHES)]

    def gate_pre(unit):
        j, c = unit
        lo_col = j * D_MODEL + c * width
        return jnp.dot(hb, wg_ref[0, :, lo_col:lo_col + width], preferred_element_type=F32)

    pre = gate_pre(units[0])
    chunks = [None] * (D_MODEL // width)
    for i, (j, c) in enumerate(units):
        nxt = gate_pre(units[i + 1]) if i + 1 < len(units) else None
        lo_col = j * D_MODEL + c * width
        gate = _sigmoid(pre + bg_ref[:, lo_col:lo_col + width])
        if j not in inputs:
            inputs[j] = attention_branch()
        term = gate * jnp.dot(inputs[j], weights[j][0, :, c * width:(c + 1) * width],
                              preferred_element_type=F32)
        chunks[c] = term if chunks[c] is None else chunks[c] + term
        pre = nxt
    merged = jnp.concatenate([t.astype(BF16) for t in chunks], axis=-1)
    xm = x + jnp.dot(merged, wo_ref[0], preferred_element_type=F32)
    xm_ref[...] = xm

    h2 = _rmsnorm(xm, g2_ref[...])
    h2_ref[...] = h2.astype(BF16)

    logits = _dot3(wr_ref[...], h2, NT) + br_ref[:, 0:1]
    gl = [logits[g:g + 1, :] for g in range(N_GROUPS)]
    gmax = functools.reduce(jnp.maximum, gl)
    gsum = functools.reduce(lambda s, t: s + t, [jnp.exp(t - gmax) for t in gl])
    gp = 1.0 / gsum
    taken = jnp.zeros_like(gmax, dtype=jnp.bool_)
    sel = []
    for g in range(N_GROUPS):
        pick = (gl[g] == gmax) & jnp.logical_not(taken)
        sel.append(pick)
        taken = taken | pick
    el = []
    for e in range(EXPERTS_PER_GROUP):
        acc = jnp.zeros_like(gmax)
        for g in range(N_GROUPS):
            r0 = N_GROUPS + g * EXPERTS_PER_GROUP + e
            acc = jnp.where(sel[g], logits[r0:r0 + 1, :], acc)
        el.append(acc)
    emax = functools.reduce(jnp.maximum, el)
    ex = [jnp.exp(t - emax) for t in el]
    esum = functools.reduce(lambda s, t: s + t, ex)
    p = [t / esum for t in ex]
    p1 = functools.reduce(jnp.maximum, p)
    taken = jnp.zeros_like(gmax, dtype=jnp.bool_)
    first = []
    for e in range(EXPERTS_PER_GROUP):
        pick = (p[e] == p1) & jnp.logical_not(taken)
        first.append(pick)
        taken = taken | pick
    rest = [jnp.where(first[e], -1.0, p[e]) for e in range(EXPERTS_PER_GROUP)]
    p2 = functools.reduce(jnp.maximum, rest)
    taken = jnp.zeros_like(gmax, dtype=jnp.bool_)
    second = []
    for e in range(EXPERTS_PER_GROUP):
        pick = (rest[e] == p2) & jnp.logical_not(taken)
        second.append(pick)
        taken = taken | pick
    denom = p1 + p2
    for e in range(EXPERTS_PER_GROUP):
        in_group = jnp.where(first[e], p1 / denom, jnp.where(second[e], p2 / denom, 0.0))
        for g in range(N_GROUPS):
            r0 = g * EXPERTS_PER_GROUP + e
            comb_ref[r0:r0 + 1, :] = jnp.where(sel[g], gp * in_group, 0.0)
    gid = functools.reduce(lambda s, t: s + t,
                           [jnp.where(sel[g], float(g), 0.0) for g in range(N_GROUPS)])
    comb_ref[N_EXPERTS:N_EXPERTS + 1, :] = gid
    comb_ref[N_EXPERTS + 1:, :] = jnp.zeros((ROUTE_ROWS - N_EXPERTS - 1, gid.shape[1]), F32)


def _merge(x2, g, ya, attn, yc, layer, wg, bg, wa, wb, wc, wo, g2, wr, br, seq):
    n = x2.shape[0]
    tm = MERGE_ROW_TILE
    row = lambda width: pl.BlockSpec((tm, width), lambda i: (i, 0))
    of_layer = lambda rows, cols: pl.BlockSpec((1, rows, cols), lambda i: (layer, 0, 0),
                                               pipeline_mode=pl.Buffered(1))
    attn_flat = [t for pair in attn for t in pair]
    attn_specs = [_residue_spec(tm, t.shape[1], B_OUT, seq // tm) for t in attn_flat]
    return pl.pallas_call(
        _merge_kernel,
        grid=(n // tm,),
        in_specs=[row(D_MODEL), _full((1, D_MODEL)), row(A_WIDTH)] + attn_specs
        + [row(C_WIDTH), of_layer(D_MODEL, 3 * D_MODEL), _full((1, 3 * D_MODEL)),
           of_layer(A_WIDTH, D_MODEL), of_layer(B_OUT, D_MODEL), of_layer(C_WIDTH, D_MODEL),
           of_layer(D_MODEL, D_MODEL), _full((1, D_MODEL)), _full((32, D_MODEL)), _full((32, LANES))],
        out_specs=[row(D_MODEL), row(D_MODEL), pl.BlockSpec((ROUTE_ROWS, tm), lambda i: (0, i))],
        out_shape=[jax.ShapeDtypeStruct((n, D_MODEL), F32),
                   jax.ShapeDtypeStruct((n, D_MODEL), BF16),
                   jax.ShapeDtypeStruct((ROUTE_ROWS, n), F32)],
        scratch_shapes=[pltpu.VMEM((4, B_OUT // LANES, tm, LANES), F32)],
        compiler_params=_params("parallel"),
        name="merge_router",
    )(x2, g, ya, *attn_flat, yc, wg, bg, wa, wb, wc, wo, g2, wr, br)


def _group_ranks(gid):
    t = gid.shape[1]
    member = gid == lax.broadcasted_iota(jnp.int32, (8, t), 0).astype(F32)
    before = (lax.broadcasted_iota(jnp.int32, (t, t), 0)
              < lax.broadcasted_iota(jnp.int32, (t, t), 1)).astype(BF16)
    ranks = jnp.dot(member.astype(BF16), before, preferred_element_type=F32)
    return jnp.sum(jnp.where(member, ranks, 0.0), axis=0, keepdims=True)


def _group_onehots(gid, rank, cap):
    slot = lax.broadcasted_iota(jnp.int32, (cap, gid.shape[1]), 0).astype(F32)
    return [jnp.where(jnp.where(gid == float(g), rank, -1.0) == slot, 1.0, 0.0).astype(BF16)
            for g in range(N_GROUPS)]


def _expert_column(cw3, e):
    lane = lax.broadcasted_iota(jnp.int32, cw3.shape, 1)
    mine = (lane % N_EXPERTS == e) & (lane < 3 * N_EXPERTS)
    return jnp.sum(jnp.where(mine, cw3, 0.0), axis=-1, keepdims=True)


def _ffn(h, cw, wg, wu, wd):
    gate = jnp.dot(h, wg, preferred_element_type=F32)
    up = jnp.dot(h, wu, preferred_element_type=F32)
    hid = gate * _sigmoid(gate) * up * cw
    return jnp.dot(hid.astype(BF16), wd, preferred_element_type=F32)


def _moe_compact_kernel(h2_ref, route_ref, comb3_ref, xc_ref, cw_ref, rank_ref):
    gid = route_ref[N_EXPERTS:N_EXPERTS + 1, :]
    rank = _group_ranks(gid)
    rank_ref[0] = jnp.broadcast_to(rank, rank_ref.shape[1:])
    for g, p in enumerate(_group_onehots(gid, rank, xc_ref.shape[1])):
        xc_ref[g] = jnp.dot(p, h2_ref[...], preferred_element_type=F32).astype(BF16)
        cw_ref[g] = jnp.dot(p, comb3_ref[...], preferred_element_type=F32)


def _moe_expert_kernel(xc_ref, cw_ref, wg_ref, wu_ref, wd_ref, y_ref, hid_ref, wd_all_ref):
    j = pl.program_id(2)
    e = pl.program_id(0) * EXPERTS_PER_GROUP + j
    h = xc_ref[0]
    gate = jnp.dot(h, wg_ref[0, 0].astype(BF16), preferred_element_type=F32)
    up = jnp.dot(h, wu_ref[0, 0].astype(BF16), preferred_element_type=F32)
    hid_ref[j] = (gate * _sigmoid(gate) * up * _expert_column(cw_ref[0], e)).astype(BF16)
    wd_all_ref[j] = wd_ref[0, 0].astype(BF16)

    @pl.when(j == EXPERTS_PER_GROUP - 1)
    def _():
        hid = jnp.concatenate([hid_ref[k] for k in range(EXPERTS_PER_GROUP)], axis=-1)
        wd = wd_all_ref[...].reshape(EXPERTS_PER_GROUP * EXPERT_FF, D_MODEL)
        y_ref[0] = jnp.dot(hid, wd, preferred_element_type=F32).astype(BF16)


def _moe_scatter_kernel(xm_ref, route_ref, rank_ref, y_ref, gf_ref, out_ref, *, final_norm):
    cap = y_ref.shape[1]
    gid = route_ref[N_EXPERTS:N_EXPERTS + 1, :]
    p = jnp.concatenate(_group_onehots(gid, rank_ref[0, 0:1, :], cap), axis=0)
    y = y_ref[...].reshape(N_GROUPS * cap, D_MODEL)
    res = xm_ref[...] + lax.dot_general(p, y, TN, preferred_element_type=F32)
    out_ref[...] = _rmsnorm(res, gf_ref[...]) if final_norm else res


def _moe_dense_kernel(ids_ref, cnt_ref, xm_ref, h2_ref, comb3_ref, wg_ref, wu_ref, wd_ref, gf_ref,
                      prev_ref, out_ref, *, final_norm):
    del ids_ref, prev_ref
    e = pl.program_id(1)

    @pl.when(pl.program_id(0) < cnt_ref[0])
    def _():
        @pl.when(e == 0)
        def _():
            out_ref[...] = xm_ref[...]

        wg, wu, wd = (r[0, 0].astype(BF16) for r in (wg_ref, wu_ref, wd_ref))
        cw = _expert_column(comb3_ref[...].astype(F32), e)
        out_ref[...] += _ffn(h2_ref[...], cw, wg, wu, wd)

        if final_norm:
            @pl.when(e == N_EXPERTS - 1)
            def _():
                out_ref[...] = _rmsnorm(out_ref[...], gf_ref[...])


def _combine_pieces(route):
    n = route.shape[1]
    pieces, rest = [], route[:N_EXPERTS].T
    for _ in range(3):
        pieces.append(rest.astype(BF16))
        rest = rest - pieces[-1].astype(F32)
    return jnp.concatenate(pieces + [jnp.zeros((n, LANES - 3 * N_EXPERTS), BF16)], axis=1)


def _overflow_flags(route):
    gid = route[N_EXPERTS].astype(jnp.int32).reshape(-1, MOE_ROW_TILE)
    counts = jnp.sum(gid[:, :, None] == jnp.arange(N_GROUPS)[None, None, :], axis=1)
    return (jnp.max(counts, axis=1) > MOE_CAP).astype(jnp.int32)


def _moe_compacted(xm, h2, route, layer, wg, wu, wd, g_final, final_norm):
    n = xm.shape[0]
    tm, cap = MOE_ROW_TILE, MOE_CAP
    nt = n // tm
    comb3 = _combine_pieces(route)
    overflow = _overflow_flags(route)

    tile = lambda width: pl.BlockSpec((tm, width), lambda i: (i, 0))
    route_spec = pl.BlockSpec((ROUTE_ROWS, tm), lambda i: (0, i))
    slots = lambda width: pl.BlockSpec((N_GROUPS, cap, width), lambda i: (0, i, 0))
    rank_spec = pl.BlockSpec((1, 8, tm), lambda i: (i, 0, 0))
    xc, cw, rank = pl.pallas_call(
        _moe_compact_kernel,
        grid=(nt,),
        in_specs=[tile(D_MODEL), route_spec, tile(LANES)],
        out_specs=[slots(D_MODEL), slots(LANES), rank_spec],
        out_shape=[jax.ShapeDtypeStruct((N_GROUPS, nt * cap, D_MODEL), BF16),
                   jax.ShapeDtypeStruct((N_GROUPS, nt * cap, LANES), F32),
                   jax.ShapeDtypeStruct((nt, 8, tm), F32)],
        compiler_params=_params("parallel"),
        name="moe_compact",
    )(h2, route, comb3)

    halves = 4
    blk = nt * cap // halves
    rows_spec = lambda width: pl.BlockSpec((1, blk, width), lambda g, r, j: (g, r, 0))
    expert = lambda rows, cols: pl.BlockSpec(
        (1, 1, rows, cols), lambda g, r, j: (layer, g * EXPERTS_PER_GROUP + j, 0, 0))
    y = pl.pallas_call(
        _moe_expert_kernel,
        grid=(N_GROUPS, halves, EXPERTS_PER_GROUP),
        in_specs=[rows_spec(D_MODEL), rows_spec(LANES), expert(D_MODEL, EXPERT_FF),
                  expert(D_MODEL, EXPERT_FF), expert(EXPERT_FF, D_MODEL)],
        out_specs=rows_spec(D_MODEL),
        out_shape=jax.ShapeDtypeStruct((N_GROUPS, nt * cap, D_MODEL), BF16),
        scratch_shapes=[pltpu.VMEM((EXPERTS_PER_GROUP, blk, EXPERT_FF), BF16),
                        pltpu.VMEM((EXPERTS_PER_GROUP, EXPERT_FF, D_MODEL), BF16)],
        compiler_params=_params("parallel", "parallel", "arbitrary"),
        name="moe_experts",
    )(xc, cw, wg, wu, wd)

    out = pl.pallas_call(
        functools.partial(_moe_scatter_kernel, final_norm=final_norm),
        grid=(nt,),
        in_specs=[tile(D_MODEL), route_spec, rank_spec, slots(D_MODEL), _full((1, D_MODEL))],
        out_specs=tile(D_MODEL),
        out_shape=jax.ShapeDtypeStruct((n, D_MODEL), F32),
        compiler_params=_params("parallel"),
        name="moe_scatter",
    )(xm, route, rank, y, g_final)

    return out, comb3, overflow


def _moe_dense(ids, count, prev, xm, h2, comb3, layer, wg, wu, wd, g_final, final_norm):
    n = xm.shape[0]
    tm = MOE_ROW_TILE
    tile = lambda width: pl.BlockSpec((tm, width), lambda s, e, ids, cnt: (ids[s], 0))
    expert = lambda rows, cols: pl.BlockSpec(
        (1, 1, rows, cols), lambda s, e, ids, cnt: (layer, jnp.where(s < cnt[0], e, 0), 0, 0))
    grid_spec = pltpu.PrefetchScalarGridSpec(
        num_scalar_prefetch=2,
        grid=(ids.shape[0], N_EXPERTS),
        in_specs=[tile(D_MODEL), tile(D_MODEL), tile(LANES),
                  expert(D_MODEL, EXPERT_FF), expert(D_MODEL, EXPERT_FF), expert(EXPERT_FF, D_MODEL),
                  pl.BlockSpec((1, D_MODEL), lambda s, e, ids, cnt: (0, 0)),
                  pl.BlockSpec(memory_space=pl.ANY)],
        out_specs=tile(D_MODEL))
    return pl.pallas_call(
        functools.partial(_moe_dense_kernel, final_norm=final_norm),
        grid_spec=grid_spec,
        out_shape=jax.ShapeDtypeStruct((n, D_MODEL), F32),
        input_output_aliases={9: 0},
        compiler_params=_params("arbitrary", "arbitrary"),
        name="moe_dense",
    )(ids, count, xm, h2, comb3, wg, wu, wd, g_final, prev)


def _moe(xm, h2, route, layer, wg, wu, wd, g_final, final_norm):
    nt = xm.shape[0] // MOE_ROW_TILE
    dense_args = (layer, wg, wu, wd, g_final, final_norm)

    def compacted(_):
        return _moe_compacted(xm, h2, route, layer, wg, wu, wd, g_final, final_norm)

    def with_fix(_):
        out, comb3, overflow = compacted(None)
        order = jnp.argsort(-overflow, stable=True).astype(jnp.int32)[:MOE_FIX_TILES]
        count = jnp.sum(overflow).astype(jnp.int32)
        ids = jnp.where(jnp.arange(MOE_FIX_TILES) < count, order, order[jnp.maximum(count - 1, 0)])
        return _moe_dense(ids, count[None], out, xm, h2, comb3, *dense_args)

    def all_dense(_):
        comb3 = _combine_pieces(route)
        return _moe_dense(jnp.arange(nt, dtype=jnp.int32), jnp.full((1,), nt, jnp.int32), xm, xm, h2,
                          comb3, *dense_args)

    n_over = jnp.sum(_overflow_flags(route))
    case = jnp.where(n_over == 0, 0, jnp.where(n_over <= MOE_FIX_TILES, 1, 2))
    return lax.switch(case, [lambda _: compacted(None)[0], with_fix, all_dense], None)


def _pack_lora(mu_wag, w1, a1, g1, mu_v, v1):
    d = D_MODEL
    if v1 is None:
        mu_v = jnp.zeros((d,), F32)
        v1 = jnp.zeros((d, V_LORA), F32)
    pad = jnp.zeros((d, LORA_HALF - (W_LORA + A_LORA + G_LORA + V_LORA)), F32)
    mus = (mu_wag[0], mu_wag[1], mu_wag[2], mu_v)
    ws = (w1, a1, g1, v1)
    cur = [w * (1.0 - m)[:, None] for w, m in zip(ws, mus)]
    prev = [w * m[:, None] for w, m in zip(ws, mus)]
    return jnp.concatenate(cur + [pad] + prev + [pad], axis=1)


def _pad_rows(w, lo, total=LANES):
    return jnp.zeros((total, w.shape[1]), F32).at[lo:lo + w.shape[0]].set(w)


def kernel(x, positions, norm_mix_g, w_in, gmlp_ln_g, gmlp_ln_b, gmlp_ws, gmlp_bs, rwkv_mu_rkv, rwkv_mu_wag, rwkv_w0, rwkv_w1, rwkv_w2, rwkv_a0, rwkv_a1, rwkv_a2, rwkv_g1, rwkv_g2, rwkv_k_k, rwkv_k_a, rwkv_r_k, rwkv_ln_w, rwkv_ln_b, rwkv_mu_v, rwkv_v0, rwkv_v1, rwkv_v2, w_branch_a, w_branch_b, w_branch_c, w_gate, b_gate, w_out, norm_ffn_g, router_group_w, router_group_b, router_expert_w, router_expert_b, expert_w_gate, expert_w_up, expert_w_down, final_norm_g):
    batch, seq, d = x.shape
    depth = w_in.shape[0]
    n = batch * seq
    x2 = x.reshape(n, d)
    rope = _rope_tables(positions)
    w_in_b, w_gate_b, w_out_b = w_in.astype(BF16), w_gate.astype(BF16), w_out.astype(BF16)
    wba, wbb, wbc = w_branch_a.astype(BF16), w_branch_b.astype(BF16), w_branch_c.astype(BF16)
    v_first = None
    for l in range(depth):
        vres = l > 0
        lora_w = _pack_lora(rwkv_mu_wag[l], rwkv_w1[l], rwkv_a1[l], rwkv_g1[l],
                            rwkv_mu_v[l - 1] if vres else None, rwkv_v1[l - 1] if vres else None)
        ya, qkv0, qkv1, qkv2, rkv, lora = _inproj(
            x2, norm_mix_g[l][None], w_in_b, l, lora_w.astype(BF16), gmlp_ln_g[l][None],
            gmlp_ln_b[l][None], gmlp_ws[l], gmlp_bs[l].T, rope, batch, seq)

        attn = [_dilated_attention(qkv, dil)
                for qkv, (_, dil) in zip((qkv0, qkv1, qkv2), DILATED_PATTERNS)]

        zero = jnp.zeros((C_WIDTH,), F32)
        vecs = jnp.stack([rwkv_w0[l], rwkv_a0[l], rwkv_v0[l - 1] if vres else zero, rwkv_k_k[l],
                          rwkv_k_a[l], rwkv_r_k[l].reshape(C_WIDTH), rwkv_ln_w[l], rwkv_ln_b[l]])
        yc, v_c = _rwkv(rkv, lora, v_first, rwkv_mu_rkv[l], vecs,
                        _pad_rows(rwkv_w2[l], 0), _pad_rows(rwkv_a2[l], W_LORA), rwkv_g2[l],
                        _pad_rows(rwkv_v2[l - 1], 0) if vres else None, batch, seq)
        if l == 0:
            v_first = v_c

        wr = jnp.concatenate([router_group_w[l], router_expert_w[l]], axis=1).T
        wr = jnp.zeros((32, d), F32).at[:wr.shape[0]].set(wr)
        br = jnp.concatenate([router_group_b[l], router_expert_b[l]])
        br = jnp.zeros((32, LANES), F32).at[:br.shape[0], 0].set(br)
        xm, h2, route = _merge(
            x2, norm_mix_g[l][None], ya, attn, yc, l, w_gate_b, b_gate[l][None], wba, wbb, wbc,
            w_out_b, norm_ffn_g[l][None], wr, br, seq)

        x2 = _moe(xm, h2, route, l, expert_w_gate, expert_w_up, expert_w_down, final_norm_g[None],
                  final_norm=(l == depth - 1))
    return x2.reshape(batch, seq, d)
```
